```python
import math
import jax, jax.numpy as jnp
from jax import lax
import numpy as np

D_MODEL = 2048
BATCH = 2
SEQ = 8192
DEPTH = 1

CHUNK = 64
N_META = 16

MIX_W = D_MODEL
SSM_W = MIX_W // 2
SSM_GROUP_CH = 16
SSM_GROUPS = SSM_W // SSM_GROUP_CH
SSM_STATE = 64
DT_MIN = 1e-3
DT_MAX = 1e-1

ATT_W = MIX_W - SSM_W
ATT_HEAD_DIM = 128
ATT_HEADS = ATT_W // ATT_HEAD_DIM
Q_BLOCK = 128
FORGET_BIAS_MIN = 1.0
FORGET_BIAS_MAX = 4.0

IN_PROJ_W = SSM_W + 3 * ATT_W + ATT_HEADS

N_EXPERTS = 64
TOP_K = 8
N_EXPERT_GROUPS = 8
TOPK_GROUPS = 4
EXPERT_FF = 512
SHARED_FF = 512
ROUTED_SCALE = 2.5
EXPERT_BLOCK = 128

LN_EPS = 1e-5
DEEPNORM_ALPHA = (2 * DEPTH) ** 0.25
DEEPNORM_BETA = (8 * DEPTH) ** -0.25

kernel_name = "hymba_s5_fox_deepnorm_moe"


def layer_norm(x, g, b):
    xf = x.astype(jnp.float32)
    mu = jnp.mean(xf, axis=-1, keepdims=True)
    xc = xf - mu
    var = jnp.mean(jnp.square(xc), axis=-1, keepdims=True)
    y = xc * lax.rsqrt(var + LN_EPS)
    return (y * g.astype(jnp.float32) + b.astype(jnp.float32)).astype(x.dtype)


def _linear_recurrence(e1, e2):
    a1, b1 = e1
    a2, b2 = e2
    return a1 * a2, a2 * b1 + b2


def s5_branch(u, a_re, a_im, log_dt, b_re, b_im, c_re, c_im, d_skip, w_glu, b_glu):
    bsz, L, _ = u.shape
    f32 = jnp.float32
    ug = u.reshape(bsz, L, SSM_GROUPS, SSM_GROUP_CH).astype(f32)
    lam = lax.complex(a_re.astype(f32), a_im.astype(f32))
    dt = jnp.exp(log_dt.astype(f32))[:, None]
    a_bar = jnp.exp(lam * dt)
    b_cplx = lax.complex(b_re.astype(f32), b_im.astype(f32))
    b_bar = ((a_bar - 1.0) / lam)[..., None] * b_cplx
    bu = jnp.einsum('blgh,gph->blgp', ug.astype(jnp.complex64), b_bar)
    a_seq = jnp.broadcast_to(a_bar, bu.shape)
    _, states = lax.associative_scan(_linear_recurrence, (a_seq, bu), axis=1)
    c_cplx = lax.complex(c_re.astype(f32), c_im.astype(f32))
    y = jnp.einsum('blgp,ghp->blgh', states, c_cplx).real + d_skip.astype(f32) * ug
    y = jax.nn.gelu(y.reshape(bsz, L, SSM_W))
    y = y * jax.nn.sigmoid(y @ w_glu.astype(f32) + b_glu.astype(f32))
    return y.astype(u.dtype)


def fox_branch(q, k, v, f_logit):
    bsz, L, _ = q.shape
    Lp = -(-L // Q_BLOCK) * Q_BLOCK
    pad = Lp - L
    shp = (bsz, L, ATT_HEADS, ATT_HEAD_DIM)
    padw = ((0, 0), (0, pad), (0, 0), (0, 0))
    q = jnp.pad(q.reshape(shp), padw)
    k = jnp.pad(k.reshape(shp), padw)
    v = jnp.pad(v.reshape(shp), padw)
    log_f = jnp.pad(jax.nn.log_sigmoid(f_logit.astype(jnp.float32)), ((0, 0), (0, pad), (0, 0)))
    cum = jnp.transpose(jnp.cumsum(log_f, axis=1), (0, 2, 1))
    scale = ATT_HEAD_DIM ** -0.5
    key_pos = jnp.arange(Lp)

    def block(i):
        s = i * Q_BLOCK
        qb = lax.dynamic_slice_in_dim(q, s, Q_BLOCK, axis=1)
        cq = lax.dynamic_slice_in_dim(cum, s, Q_BLOCK, axis=2)
        logits = jnp.einsum('bqhd,bkhd->bhqk', qb, k).astype(jnp.float32) * scale
        logits = logits + cq[..., None] - cum[:, :, None, :]
        q_pos = s + jnp.arange(Q_BLOCK)
        mask = key_pos[None, :] <= q_pos[:, None]
        logits = jnp.where(mask, logits, -jnp.inf)
        p = jax.nn.softmax(logits, axis=-1)
        return jnp.einsum('bhqk,bkhd->bqhd', p.astype(v.dtype), v)

    out = lax.map(block, jnp.arange(Lp // Q_BLOCK))
    out = jnp.transpose(out, (1, 0, 2, 3, 4)).reshape(bsz, Lp, ATT_W)
    return out[:, :L]


def hybrid_mixer(h, w_in, b_forget, a_re, a_im, log_dt, b_re, b_im, c_re, c_im, d_skip,
                 w_glu, b_glu, w_out):
    proj = h @ w_in
    u, q, k, v, f_logit = jnp.split(
        proj, [SSM_W, SSM_W + ATT_W, SSM_W + 2 * ATT_W, SSM_W + 3 * ATT_W], axis=-1)
    y_ssm = s5_branch(u, a_re, a_im, log_dt, b_re, b_im, c_re, c_im, d_skip, w_glu, b_glu)
    y_att = fox_branch(q, k, v, f_logit + b_forget)
    y = jnp.concatenate([y_ssm, y_att.astype(y_ssm.dtype)], axis=-1)
    return (y @ w_out).astype(h.dtype)


def moe_ffn(h2d, w_router, router_bias, w_gate_exp, w_up_exp, w_down_exp,
            w_gate_sh, w_up_sh, w_down_sh):
    T, D = h2d.shape
    scores = jax.nn.sigmoid((h2d @ w_router).astype(jnp.float32))
    sel = scores + router_bias.astype(jnp.float32)
    grouped = sel.reshape(T, N_EXPERT_GROUPS, N_EXPERTS // N_EXPERT_GROUPS)
    group_score = jnp.sum(lax.top_k(grouped, 2)[0], axis=-1)
    _, top_groups = lax.top_k(group_score, TOPK_GROUPS)
    group_mask = jnp.sum(jax.nn.one_hot(top_groups, N_EXPERT_GROUPS), axis=1) > 0
    expert_mask = jnp.repeat(group_mask, N_EXPERTS // N_EXPERT_GROUPS, axis=1)
    _, idx = lax.top_k(jnp.where(expert_mask, sel, -jnp.inf), TOP_K)
    gate = jnp.take_along_axis(scores, idx, axis=1)
    gate = gate / jnp.sum(gate, axis=-1, keepdims=True) * ROUTED_SCALE

    n_assign = T * TOP_K
    flat_e = idx.reshape(-1)
    flat_t = jnp.arange(n_assign, dtype=jnp.int32) // TOP_K
    flat_w = gate.reshape(-1)
    order = jnp.argsort(flat_e)
    se, st, sw = flat_e[order], flat_t[order], flat_w[order]
    counts = jnp.bincount(flat_e, length=N_EXPERTS)
    starts = jnp.cumsum(counts) - counts
    pcounts = (counts + EXPERT_BLOCK - 1) // EXPERT_BLOCK * EXPERT_BLOCK
    pends = jnp.cumsum(pcounts)
    pstarts = pends - pcounts
    dest = pstarts[se] + (jnp.arange(n_assign) - starts[se])
    n_blocks = -(-n_assign // EXPERT_BLOCK) + N_EXPERTS
    buf_tok = jnp.full((n_blocks * EXPERT_BLOCK,), T, jnp.int32).at[dest].set(st)
    buf_w = jnp.zeros((n_blocks * EXPERT_BLOCK,), jnp.float32).at[dest].set(sw)
    block_e = jnp.minimum(
        jnp.searchsorted(pends, jnp.arange(n_blocks) * EXPERT_BLOCK, side='right'), N_EXPERTS - 1)
    x_pad = jnp.concatenate([h2d, jnp.zeros((1, D), h2d.dtype)], axis=0)

    def body(b, acc):
        tok = lax.dynamic_slice(buf_tok, (b * EXPERT_BLOCK,), (EXPERT_BLOCK,))
        wt = lax.dynamic_slice(buf_w, (b * EXPERT_BLOCK,), (EXPERT_BLOCK,))
        e = block_e[b]
        xb = x_pad[tok]
        hid = jax.nn.silu(xb @ w_gate_exp[e]) * (xb @ w_up_exp[e])
        yb = (hid @ w_down_exp[e]) * wt[:, None]
        return acc.at[tok].add(yb.astype(acc.dtype))

    acc = lax.fori_loop(0, n_blocks, body, jnp.zeros((T + 1, D), h2d.dtype))
    shared = (jax.nn.silu(h2d @ w_gate_sh) * (h2d @ w_up_sh)) @ w_down_sh
    return (acc[:T] + shared).astype(h2d.dtype)


def setup_inputs(seed: int = 0) -> dict:
    key = jax.random.key(seed)
    ks = iter(jax.random.split(key, 40))
    f32 = jnp.float32

    def nrm(shape, std):
        return std * jax.random.normal(next(ks), shape, f32)

    def uni(shape, lo, hi):
        return jax.random.uniform(next(ks), shape, f32, minval=lo, maxval=hi)

    Ld, D, G, P, Hc = DEPTH, D_MODEL, SSM_GROUPS, SSM_STATE, SSM_GROUP_CH
    n = jnp.arange(P, dtype=f32)
    return {
        "x": nrm((BATCH, SEQ, D), 1.0),
        "meta_tokens": nrm((N_META, D), 1.0),
        "ln_in_g": 1.0 + nrm((D,), 0.02),
        "ln_in_b": nrm((D,), 0.02),
        "w_in": nrm((Ld, D, IN_PROJ_W), D ** -0.5),
        "b_forget": uni((Ld, ATT_HEADS), FORGET_BIAS_MIN, FORGET_BIAS_MAX),
        "ssm_a_re": -0.5 + nrm((Ld, G, P), 0.01),
        "ssm_a_im": math.pi * n + nrm((Ld, G, P), 0.01),
        "ssm_log_dt": uni((Ld, G), math.log(DT_MIN), math.log(DT_MAX)),
        "ssm_b_re": nrm((Ld, G, P, Hc), (2 * Hc) ** -0.5),
        "ssm_b_im": nrm((Ld, G, P, Hc), (2 * Hc) ** -0.5),
        "ssm_c_re": nrm((Ld, G, Hc, P), 0.5),
        "ssm_c_im": nrm((Ld, G, Hc, P), 0.5),
        "ssm_d": nrm((Ld, G, Hc), 1.0),
        "w_glu": nrm((Ld, SSM_W, SSM_W), SSM_W ** -0.5),
        "b_glu": nrm((Ld, SSM_W), 0.02),
        "w_out": nrm((Ld, MIX_W, D), MIX_W ** -0.5 * DEEPNORM_BETA),
        "ln_mix_g": 1.0 + nrm((Ld, D), 0.02),
        "ln_mix_b": nrm((Ld, D), 0.02),
        "w_router": nrm((Ld, D, N_EXPERTS), D ** -0.5),
        "router_bias": nrm((Ld, N_EXPERTS), 0.01),
        "w_gate_exp": nrm((Ld, N_EXPERTS, D, EXPERT_FF), D ** -0.5),
        "w_up_exp": nrm((Ld, N_EXPERTS, D, EXPERT_FF), D ** -0.5),
        "w_down_exp": nrm((Ld, N_EXPERTS, EXPERT_FF, D), EXPERT_FF ** -0.5 * DEEPNORM_BETA),
        "w_gate_sh": nrm((Ld, D, SHARED_FF), D ** -0.5),
        "w_up_sh": nrm((Ld, D, SHARED_FF), D ** -0.5),
        "w_down_sh": nrm((Ld, SHARED_FF, D), SHARED_FF ** -0.5 * DEEPNORM_BETA),
        "ln_ffn_g": 1.0 + nrm((Ld, D), 0.02),
        "ln_ffn_b": nrm((Ld, D), 0.02),
    }


def reference(x, meta_tokens, ln_in_g, ln_in_b, w_in, b_forget, ssm_a_re, ssm_a_im,
              ssm_log_dt, ssm_b_re, ssm_b_im, ssm_c_re, ssm_c_im, ssm_d, w_glu, b_glu,
              w_out, ln_mix_g, ln_mix_b, w_router, router_bias, w_gate_exp, w_up_exp,
              w_down_exp, w_gate_sh, w_up_sh, w_down_sh, ln_ffn_g, ln_ffn_b):
    bsz = x.shape[0]
    meta = jnp.broadcast_to(meta_tokens[None].astype(x.dtype), (bsz, N_META, D_MODEL))
    h = layer_norm(jnp.concatenate([meta, x], axis=1), ln_in_g, ln_in_b)
    for i in range(DEPTH):
        mix = hybrid_mixer(h, w_in[i], b_forget[i], ssm_a_re[i], ssm_a_im[i], ssm_log_dt[i],
                           ssm_b_re[i], ssm_b_im[i], ssm_c_re[i], ssm_c_im[i], ssm_d[i],
                           w_glu[i], b_glu[i], w_out[i])
        h = layer_norm(DEEPNORM_ALPHA * h + mix, ln_mix_g[i], ln_mix_b[i])
        ffn = moe_ffn(h.reshape(-1, D_MODEL), w_router[i], router_bias[i], w_gate_exp[i],
                      w_up_exp[i], w_down_exp[i], w_gate_sh[i], w_up_sh[i],
                      w_down_sh[i]).reshape(h.shape)
        h = layer_norm(DEEPNORM_ALPHA * h + ffn, ln_ffn_g[i], ln_ffn_b[i])
    return h[:, N_META:]
```

```python
import functools
import math

import jax
import jax.numpy as jnp
from jax import lax
from jax.experimental import pallas as pl
from jax.experimental.pallas import tpu as pltpu

N_META = 16
SSM_GROUP_CH = 16
SSM_STATE = 64
ATT_HEAD_DIM = 128
N_EXPERTS = 64
TOP_K = 8
N_EXPERT_GROUPS = 8
TOPK_GROUPS = 4
ROUTED_SCALE = 2.5
LN_EPS = 1e-5
DEPTH = 1
DEEPNORM_ALPHA = (2 * DEPTH) ** 0.25

S5_CHUNK = 16
LANES = 128
NEG_BIG = -1e30

F32 = jnp.float32
BF16 = jnp.bfloat16
HIGHEST = lax.Precision.HIGHEST


def _dot(a, b):
    return jnp.dot(a, b, preferred_element_type=F32)


def _dot_nt(a, b):
    return lax.dot_general(a, b, (((1,), (1,)), ((), ())), preferred_element_type=F32)


def _layer_norm(x, g, b):
    mu = jnp.mean(x, axis=-1, keepdims=True)
    xc = x - mu
    var = jnp.mean(xc * xc, axis=-1, keepdims=True)
    return xc * lax.rsqrt(var + LN_EPS) * g + b


def _params(sem, vmem_mb):
    return pltpu.CompilerParams(dimension_semantics=sem, vmem_limit_bytes=vmem_mb * 1024 * 1024)


def _const_spec(shape):
    nd = len(shape)
    return pl.BlockSpec(shape, lambda *_: (0,) * nd, pipeline_mode=pl.Buffered(1))


def _ln_inproj_kernel(x_ref, g_ref, b_ref, w_ref, bf_ref, h_ref, u_ref, q_ref, k_ref, v_ref, lf_ref,
                      *, ssm_w, att_w):
    h = _layer_norm(x_ref[...], g_ref[...], b_ref[...])
    h_ref[...] = h
    hb = h.astype(BF16)
    o = 0
    u_ref[...] = _dot(hb, w_ref[:, o:o + ssm_w]).astype(BF16)
    o += ssm_w
    q_ref[...] = (_dot(hb, w_ref[:, o:o + att_w]) * (ATT_HEAD_DIM ** -0.5)).astype(BF16)
    o += att_w
    k_ref[...] = _dot(hb, w_ref[:, o:o + att_w]).astype(BF16)
    o += att_w
    v_ref[...] = _dot(hb, w_ref[:, o:o + att_w]).astype(BF16)
    o += att_w
    f = _dot(hb, w_ref[:, o:o + LANES]) + bf_ref[...]
    lf_ref[...] = jnp.minimum(f, 0.0) - jnp.log(1.0 + jnp.exp(-jnp.abs(f)))


def _ln_inproj(x2d, g, b, w_bf, bf_pad, tm, ssm_w, att_w):
    t, d = x2d.shape
    wcols = w_bf.shape[1]
    row = lambda i: (i, 0)
    return pl.pallas_call(
        functools.partial(_ln_inproj_kernel, ssm_w=ssm_w, att_w=att_w),
        grid=(t // tm,),
        in_specs=[
            pl.BlockSpec((tm, d), row),
            _const_spec((1, d)), _const_spec((1, d)),
            _const_spec((d, wcols)), _const_spec((1, LANES)),
        ],
        out_specs=[
            pl.BlockSpec((tm, d), row),
            pl.BlockSpec((tm, ssm_w), row),
            pl.BlockSpec((tm, att_w), row),
            pl.BlockSpec((tm, att_w), row),
            pl.BlockSpec((tm, att_w), row),
            pl.BlockSpec((tm, LANES), row),
        ],
        out_shape=[
            jax.ShapeDtypeStruct((t, d), F32),
            jax.ShapeDtypeStruct((t, ssm_w), BF16),
            jax.ShapeDtypeStruct((t, att_w), BF16),
            jax.ShapeDtypeStruct((t, att_w), BF16),
            jax.ShapeDtypeStruct((t, att_w), BF16),
            jax.ShapeDtypeStruct((t, LANES), F32),
        ],
        compiler_params=_params(("parallel",), 56),
        name="ln_inproj",
    )(x2d, g, b, w_bf, bf_pad)


def _s5_tables(a_re, a_im, log_dt, b_re, b_im, c_re, c_im, d_skip, levels):
    c = S5_CHUNK
    g, p = a_re.shape
    hc = SSM_GROUP_CH
    lam = lax.complex(a_re.astype(F32), a_im.astype(F32))
    dt = jnp.exp(log_dt.astype(F32))[:, None]
    ldt = lam * dt
    abar = jnp.exp(ldt)
    bbar = ((abar - 1.0) / lam)[..., None] * lax.complex(b_re.astype(F32), b_im.astype(F32))
    cc = lax.complex(c_re.astype(F32), c_im.astype(F32))
    tau = jnp.arange(c + 1, dtype=F32)
    apow = jnp.exp(ldt[:, None, :] * tau[None, :, None])

    ca = cc[:, None, :, :] * apow[:, :c, None, :]
    taps = (jnp.einsum('gthp,gpk->gthk', ca.real, bbar.real, precision=HIGHEST)
            - jnp.einsum('gthp,gpk->gthk', ca.imag, bbar.imag, precision=HIGHEST))
    tt = jnp.arange(c)[:, None]
    ss = jnp.arange(c)[None, :]
    lag = tt - ss
    m = taps[:, jnp.clip(lag, 0, c - 1)]
    m = jnp.where((lag >= 0)[None, :, :, None, None], m, 0.0)
    eye_t = (lag == 0).astype(F32)[None, :, :, None, None]
    eye_h = jnp.eye(hc, dtype=F32)[None, None, None, :, :]
    m = m + eye_t * eye_h * d_skip.astype(F32)[:, None, None, :, None]
    m_t = m.transpose(0, 2, 4, 1, 3).reshape(g, c * hc, c * hc)

    bp = apow[:, c - 1 - jnp.arange(c), :][:, :, :, None] * bbar[:, None, :, :]
    bp = bp.transpose(0, 1, 3, 2).reshape(g, c * hc, p)
    bp_t = jnp.concatenate([bp.real, bp.imag], axis=-1)

    cp = cc[:, None, :, :] * apow[:, 1:c + 1, None, :]
    cp = cp.reshape(g, c * hc, p).transpose(0, 2, 1)
    cp_t = jnp.concatenate([cp.real, -cp.imag], axis=1)

    steps = (c * (2.0 ** jnp.arange(levels, dtype=F32)))
    alev = jnp.exp(ldt[:, None, :] * steps[None, :, None])
    pad = ((0, 0), (0, 16 - levels), (0, 0))
    a1 = jnp.pad(jnp.concatenate([alev.real, alev.real], axis=-1), pad)
    a2 = jnp.pad(jnp.concatenate([-alev.imag, alev.imag], axis=-1), pad)
    return m_t.astype(BF16), bp_t.astype(BF16), cp_t.astype(BF16), a1, a2


def _shift_rows(x, sh):
    n = x.shape[0]
    if sh % 8 == 0:
        return jnp.concatenate([jnp.zeros((sh, x.shape[1]), x.dtype), x[:n - sh]], axis=0)
    rolled = pltpu.roll(x, sh, axis=0)
    rows = lax.broadcasted_iota(jnp.int32, x.shape, 0)
    return jnp.where(rows < sh, 0.0, rolled)


def _cmul(a1, a2, x):
    return a1 * x + a2 * pltpu.roll(x, SSM_STATE, axis=1)


def _s5_kernel(u_ref, um_ref, mt_ref, bpt_ref, cpt_ref, a1_ref, a2_ref, y_ref, *, bsz, nc, levels):
    u = u_ref[0]
    bpt = bpt_ref[0]
    w = _dot(u, bpt)
    x_meta = _dot(um_ref[0], bpt)[:8]
    first = lax.broadcasted_iota(jnp.int32, (8, 2 * SSM_STATE), 0) == 0
    inject = jnp.where(first, _cmul(a1_ref[0, 0:1, :], a2_ref[0, 0:1, :], x_meta), 0.0)
    x_first = jnp.where(first, x_meta, 0.0)
    x_in = []
    for b in range(bsz):
        x = w[b * nc:(b + 1) * nc]
        x = jnp.concatenate([x[:8] + inject, x[8:]], axis=0)
        for lvl in range(levels):
            sh = 2 ** lvl
            x = x + _cmul(a1_ref[0, lvl:lvl + 1, :], a2_ref[0, lvl:lvl + 1, :], _shift_rows(x, sh))
        xp = _shift_rows(x, 1)
        x_in.append(jnp.concatenate([xp[:8] + x_first, xp[8:]], axis=0))
    x_in = jnp.concatenate(x_in, axis=0).astype(BF16)
    y = _dot(u, mt_ref[0]) + _dot(x_in, cpt_ref[0])
    y_ref[0] = jax.nn.gelu(y).astype(BF16)


def _s5(u_g, um_g, m_t, bp_t, cp_t, a1, a2, bsz, nc, levels):
    g, rows, cw = u_g.shape
    nl = a1.shape[1]
    grp = lambda i: (i, 0, 0)
    return pl.pallas_call(
        functools.partial(_s5_kernel, bsz=bsz, nc=nc, levels=levels),
        grid=(g,),
        in_specs=[
            pl.BlockSpec((1, rows, cw), grp),
            pl.BlockSpec((1, 16, cw), grp),
            pl.BlockSpec((1, cw, cw), grp),
            pl.BlockSpec((1, cw, 2 * SSM_STATE), grp),
            pl.BlockSpec((1, 2 * SSM_STATE, cw), grp),
            pl.BlockSpec((1, nl, 2 * SSM_STATE), grp),
            pl.BlockSpec((1, nl, 2 * SSM_STATE), grp),
        ],
        out_specs=pl.BlockSpec((1, rows, cw), grp),
        out_shape=jax.ShapeDtypeStruct((g, rows, cw), BF16),
        compiler_params=_params(("parallel",), 32),
        name="s5_scan",
    )(u_g, um_g, m_t, bp_t, cp_t, a1, a2)


def _fox_kernel(q_ref, k_ref, v_ref, ck_ref, km_ref, vm_ref, ckm_ref, o_ref, m_scr, l_scr, acc_scr, *, tq):
    qi = pl.program_id(2)
    q = q_ref[...]

    s = _dot_nt(q, km_ref[...]) - ckm_ref[0]
    m0 = jnp.max(s, axis=1, keepdims=True)
    p = jnp.exp(s - m0)
    m_scr[...] = m0
    l_scr[...] = jnp.sum(p, axis=1, keepdims=True)
    acc_scr[...] = _dot(p.astype(BF16), vm_ref[...])

    def step(j, masked):
        start = pl.multiple_of(j * tq, tq)
        kb = k_ref[pl.ds(start, tq), :]
        vb = v_ref[pl.ds(start, tq), :]
        s = _dot_nt(q, kb) - ck_ref[0, 0, pl.ds(j, 1), :]
        if masked:
            rows = lax.broadcasted_iota(jnp.int32, (tq, tq), 0)
            cols = lax.broadcasted_iota(jnp.int32, (tq, tq), 1)
            s = jnp.where(cols <= rows, s, NEG_BIG)
        m_prev = m_scr[...]
        m_new = jnp.maximum(m_prev, jnp.max(s, axis=1, keepdims=True))
        alpha = jnp.exp(m_prev - m_new)
        p = jnp.exp(s - m_new)
        l_scr[...] = alpha * l_scr[...] + jnp.sum(p, axis=1, keepdims=True)
        acc_scr[...] = alpha * acc_scr[...] + _dot(p.astype(BF16), vb)
        m_scr[...] = m_new

    def body(j, carry):
        step(j, False)
        return carry

    lax.fori_loop(0, qi, body, 0)
    step(qi, True)
    o_ref[...] = (acc_scr[...] / l_scr[...]).astype(BF16)


def _fox(q, k, v, ck, km, vm, ckm, bsz, seq, heads, tq):
    t, aw = q.shape
    nq = seq // tq
    dh = ATT_HEAD_DIM
    return pl.pallas_call(
        functools.partial(_fox_kernel, tq=tq),
        grid=(bsz, heads, nq),
        in_specs=[
            pl.BlockSpec((tq, dh), lambda b, h, i: (b * nq + i, h)),
            pl.BlockSpec((seq, dh), lambda b, h, i: (b, h)),
            pl.BlockSpec((seq, dh), lambda b, h, i: (b, h)),
            pl.BlockSpec((1, 1, nq, tq), lambda b, h, i: (b, h, 0, 0)),
            pl.BlockSpec((LANES, dh), lambda b, h, i: (0, h)),
            pl.BlockSpec((LANES, dh), lambda b, h, i: (0, h)),
            pl.BlockSpec((1, 1, LANES), lambda b, h, i: (h, 0, 0)),
        ],
        out_specs=pl.BlockSpec((tq, dh), lambda b, h, i: (b * nq + i, h)),
        out_shape=jax.ShapeDtypeStruct((t, aw), BF16),
        scratch_shapes=[
            pltpu.VMEM((tq, 1), F32),
            pltpu.VMEM((tq, 1), F32),
            pltpu.VMEM((tq, dh), F32),
        ],
        compiler_params=_params(("parallel", "parallel", "arbitrary"), 32),
        name="fox_attention",
    )(q, k, v, ck, km, vm, ckm)


def _mix_out_kernel(ys_ref, ya_ref, h_ref, wglu_ref, bglu_ref, wout_ref, g_ref, b_ref, wr_ref,
                    h2_ref, h2b_ref, lt_ref, *, ssm_w):
    ys = ys_ref[...]
    z = _dot(ys, wglu_ref[...]) + bglu_ref[...]
    ysf = ys.astype(F32)
    yg = (ysf * jax.nn.sigmoid(z)).astype(BF16)
    mix = _dot(yg, wout_ref[0:ssm_w, :]) + _dot(ya_ref[...], wout_ref[ssm_w:, :])
    h2 = _layer_norm(DEEPNORM_ALPHA * h_ref[...] + mix, g_ref[...], b_ref[...])
    h2_ref[...] = h2
    h2b = h2.astype(BF16)
    h2b_ref[...] = h2b
    lt_ref[...] = _dot_nt(wr_ref[...], h2b)


def _mix_out(ys, ya, h, wglu, bglu, wout, g, b, wr_t, tm):
    t, d = h.shape
    ssm_w = ys.shape[1]
    att_w = ya.shape[1]
    e = wr_t.shape[0]
    row = lambda i: (i, 0)
    return pl.pallas_call(
        functools.partial(_mix_out_kernel, ssm_w=ssm_w),
        grid=(t // tm,),
        in_specs=[
            pl.BlockSpec((tm, ssm_w), row),
            pl.BlockSpec((tm, att_w), row),
            pl.BlockSpec((tm, d), row),
            _const_spec((ssm_w, ssm_w)), _const_spec((1, ssm_w)),
            _const_spec((d, d)), _const_spec((1, d)), _const_spec((1, d)),
            _const_spec((e, d)),
        ],
        out_specs=[
            pl.BlockSpec((tm, d), row),
            pl.BlockSpec((tm, d), row),
            pl.BlockSpec((e, tm), lambda i: (0, i)),
        ],
        out_shape=[
            jax.ShapeDtypeStruct((t, d), F32),
            jax.ShapeDtypeStruct((t, d), BF16),
            jax.ShapeDtypeStruct((e, t), F32),
        ],
        compiler_params=_params(("parallel",), 48),
        name="mix_out",
    )(ys, ya, h, wglu, bglu, wout, g, b, wr_t)


def _route_kernel(lt_ref, bias_ref, idx_ref, gate_ref, rank_ref, cnt_ref, run_scr, *, tn):
    e = N_EXPERTS
    ng = N_EXPERT_GROUPS
    gs = e // ng

    @pl.when(pl.program_id(0) == 0)
    def _():
        run_scr[...] = jnp.zeros_like(run_scr)

    scores = jax.nn.sigmoid(lt_ref[...])
    sel = scores + bias_ref[...]
    sel3 = sel.reshape(ng, gs, tn)
    mem = lax.broadcasted_iota(jnp.int32, (ng, gs, tn), 1)
    m1 = jnp.max(sel3, axis=1, keepdims=True)
    i1 = jnp.min(jnp.where(sel3 == m1, mem, gs), axis=1, keepdims=True)
    m2 = jnp.max(jnp.where(mem == i1, -jnp.inf, sel3), axis=1, keepdims=True)
    gscore = (m1 + m2).reshape(ng, tn)

    gio = lax.broadcasted_iota(jnp.int32, (ng, tn), 0)
    gmask = jnp.zeros((ng, tn), F32)
    for _ in range(TOPK_GROUPS):
        mx = jnp.max(gscore, axis=0, keepdims=True)
        gi = jnp.min(jnp.where(gscore == mx, gio, ng), axis=0, keepdims=True)
        hit = gio == gi
        gmask = jnp.where(hit, 1.0, gmask)
        gscore = jnp.where(hit, -jnp.inf, gscore)

    cand = jnp.where(gmask.reshape(ng, 1, tn) > 0.5, sel3, -jnp.inf).reshape(e, tn)
    eio = lax.broadcasted_iota(jnp.int32, (e, tn), 0)
    hits = []
    chosen = jnp.zeros((e, tn), F32)
    for _ in range(TOP_K):
        mx = jnp.max(cand, axis=0, keepdims=True)
        ei = jnp.min(jnp.where(cand == mx, eio, e), axis=0, keepdims=True)
        hit = eio == ei
        hits.append((ei, hit))
        chosen = jnp.where(hit, 1.0, chosen)
        cand = jnp.where(hit, -jnp.inf, cand)

    onehot = chosen.astype(BF16)
    r = lax.broadcasted_iota(jnp.int32, (tn, tn), 0)
    c = lax.broadcasted_iota(jnp.int32, (tn, tn), 1)
    tri = (r < c).astype(BF16)
    rank = _dot(onehot, tri) + run_scr[:, 0:1]
    run_scr[...] = run_scr[...] + jnp.sum(chosen, axis=1, keepdims=True)
    cnt_ref[...] = run_scr[...]

    gates = [jnp.sum(jnp.where(hit, scores, 0.0), axis=0, keepdims=True) for _, hit in hits]
    total = gates[0]
    for gk in gates[1:]:
        total = total + gk
    for kk, (ei, hit) in enumerate(hits):
        idx_ref[kk:kk + 1, :] = ei
        gate_ref[kk:kk + 1, :] = gates[kk] / total * ROUTED_SCALE
        rank_ref[kk:kk + 1, :] = jnp.sum(jnp.where(hit, rank, 0.0), axis=0, keepdims=True).astype(jnp.int32)


def _route(logits_t, bias, tn):
    e, t = logits_t.shape
    col = lambda i: (0, i)
    return pl.pallas_call(
        functools.partial(_route_kernel, tn=tn),
        grid=(t // tn,),
        in_specs=[pl.BlockSpec((e, tn), col), _const_spec((e, 1))],
        out_specs=[
            pl.BlockSpec((TOP_K, tn), col),
            pl.BlockSpec((TOP_K, tn), col),
            pl.BlockSpec((TOP_K, tn), col),
            pl.BlockSpec((e, LANES), lambda i: (0, 0)),
        ],
        out_shape=[
            jax.ShapeDtypeStruct((TOP_K, t), jnp.int32),
            jax.ShapeDtypeStruct((TOP_K, t), F32),
            jax.ShapeDtypeStruct((TOP_K, t), jnp.int32),
            jax.ShapeDtypeStruct((e, LANES), F32),
        ],
        scratch_shapes=[pltpu.VMEM((e, LANES), F32)],
        compiler_params=_params(("arbitrary",), 32),
        name="route",
    )(logits_t, bias)


def _moe_kernel(be_ref, bv_ref, bf_ref, x_ref, wg_ref, wu_ref, wd_ref, y_ref, wg_s, wu_s, wd_s):
    b = pl.program_id(0)

    @pl.when(bf_ref[b] == 1)
    def _():
        wg_s[...] = wg_ref[0].astype(BF16)
        wu_s[...] = wu_ref[0].astype(BF16)
        wd_s[...] = wd_ref[0].astype(BF16)

    @pl.when(bv_ref[b] == 1)
    def _():
        x = x_ref[...]
        hid = jax.nn.silu(_dot(x, wg_s[...])) * _dot(x, wu_s[...])
        y_ref[...] = _dot(hid.astype(BF16), wd_s[...]).astype(BF16)

    @pl.when(bv_ref[b] == 0)
    def _():
        y_ref[...] = jnp.zeros_like(y_ref)


def _moe_grouped(block_e, block_valid, block_first, xs, wg, wu, wd, bm):
    rows, d = xs.shape
    ff = wg.shape[2]
    n_blocks = rows // bm
    grid_spec = pltpu.PrefetchScalarGridSpec(
        num_scalar_prefetch=3,
        grid=(n_blocks,),
        in_specs=[
            pl.BlockSpec((bm, d), lambda b, be, bv, bf: (b, 0)),
            pl.BlockSpec((1, d, ff), lambda b, be, bv, bf: (be[b], 0, 0)),
            pl.BlockSpec((1, d, ff), lambda b, be, bv, bf: (be[b], 0, 0)),
            pl.BlockSpec((1, ff, d), lambda b, be, bv, bf: (be[b], 0, 0)),
        ],
        out_specs=pl.BlockSpec((bm, d), lambda b, be, bv, bf: (b, 0)),
        scratch_shapes=[
            pltpu.VMEM((d, ff), BF16),
            pltpu.VMEM((d, ff), BF16),
            pltpu.VMEM((ff, d), BF16),
        ],
    )
    return pl.pallas_call(
        _moe_kernel,
        grid_spec=grid_spec,
        out_shape=jax.ShapeDtypeStruct((rows, d), BF16),
        compiler_params=_params(("arbitrary",), 48),
        name="moe_grouped",
    )(block_e, block_valid, block_first, xs, wg, wu, wd)


def _final_kernel(h2_ref, h2b_ref, r_ref, wgu_ref, wd_ref, g_ref, b_ref, o_ref, *, ff):
    x = h2b_ref[...]
    gu = _dot(x, wgu_ref[...])
    hid = jax.nn.silu(gu[:, :ff]) * gu[:, ff:]
    shared = _dot(hid.astype(BF16), wd_ref[...])
    ffn = r_ref[...] + shared
    o_ref[...] = _layer_norm(DEEPNORM_ALPHA * h2_ref[...] + ffn, g_ref[...], b_ref[...])


def _final(h2, h2b, routed, wgu, wd, g, b, tm):
    t, d = h2.shape
    ff = wd.shape[0]
    row = lambda i: (i, 0)
    return pl.pallas_call(
        functools.partial(_final_kernel, ff=ff),
        grid=(t // tm,),
        in_specs=[
            pl.BlockSpec((tm, d), row), pl.BlockSpec((tm, d), row), pl.BlockSpec((tm, d), row),
            _const_spec((d, 2 * ff)), _const_spec((ff, d)), _const_spec((1, d)), _const_spec((1, d)),
        ],
        out_specs=pl.BlockSpec((tm, d), row),
        out_shape=jax.ShapeDtypeStruct((t, d), F32),
        compiler_params=_params(("parallel",), 48),
        name="final",
    )(h2, h2b, routed, wgu, wd, g, b)


def kernel(x, meta_tokens, ln_in_g, ln_in_b, w_in, b_forget, ssm_a_re, ssm_a_im, ssm_log_dt, ssm_b_re, ssm_b_im, ssm_c_re, ssm_c_im, ssm_d, w_glu, b_glu, w_out, ln_mix_g, ln_mix_b, w_router, router_bias, w_gate_exp, w_up_exp, w_down_exp, w_gate_sh, w_up_sh, w_down_sh, ln_ffn_g, ln_ffn_b):
    bsz, seq, d = x.shape
    t = bsz * seq
    ssm_w = w_glu.shape[1]
    heads = b_forget.shape[1]
    att_w = (w_in.shape[2] - ssm_w - heads) // 3
    groups = ssm_a_re.shape[1]
    assert meta_tokens.shape[0] == N_META == S5_CHUNK and att_w == heads * ATT_HEAD_DIM
    nc = seq // S5_CHUNK
    levels = int(math.log2(nc))
    assert 2 ** levels == nc and nc % 8 == 0
    row2 = lambda a: a.reshape(1, -1).astype(F32)

    w_in_bf = jnp.pad(w_in[0], ((0, 0), (0, LANES - heads))).astype(BF16)
    bf_pad = jnp.pad(b_forget[0].astype(F32), (0, LANES - heads)).reshape(1, LANES)
    g_in, b_in = row2(ln_in_g), row2(ln_in_b)
    tm = min(256, t)
    h, u, q, k, v, lf = _ln_inproj(x.reshape(t, d), g_in, b_in, w_in_bf, bf_pad, tm, ssm_w, att_w)
    _, u_m, _, k_m, v_m, lf_m = _ln_inproj(meta_tokens.astype(F32), g_in, b_in, w_in_bf, bf_pad, N_META,
                                           ssm_w, att_w)

    m_t, bp_t, cp_t, a1, a2 = _s5_tables(ssm_a_re[0], ssm_a_im[0], ssm_log_dt[0], ssm_b_re[0], ssm_b_im[0],
                                         ssm_c_re[0], ssm_c_im[0], ssm_d[0], levels)
    cw = S5_CHUNK * SSM_GROUP_CH
    u_g = u.reshape(bsz * nc, S5_CHUNK, groups, SSM_GROUP_CH).transpose(2, 0, 1, 3).reshape(groups, bsz * nc, cw)
    um_g = u_m.reshape(1, S5_CHUNK, groups, SSM_GROUP_CH).transpose(2, 0, 1, 3).reshape(groups, 1, cw)
    um_g = jnp.broadcast_to(um_g, (groups, 16, cw))
    y_g = _s5(u_g, um_g, m_t, bp_t, cp_t, a1, a2, bsz, nc, levels)
    y_ssm = y_g.reshape(groups, bsz * nc, S5_CHUNK, SSM_GROUP_CH).transpose(1, 2, 0, 3).reshape(t, ssm_w)

    tq = min(512, seq)
    lfm = lf_m[:, :heads]
    c_meta = jnp.cumsum(lfm, axis=0) - jnp.sum(lfm, axis=0, keepdims=True)
    ckm = jnp.full((heads, LANES), -NEG_BIG, F32).at[:, :N_META].set(c_meta.T).reshape(heads, 1, LANES)
    c_main = jnp.cumsum(lf[:, :heads].reshape(bsz, seq, heads), axis=1)
    ck = c_main.transpose(0, 2, 1).reshape(bsz, heads, seq // tq, tq)
    km = jnp.pad(k_m, ((0, LANES - N_META), (0, 0)))
    vm = jnp.pad(v_m, ((0, LANES - N_META), (0, 0)))
    y_att = _fox(q, k, v, ck, km, vm, ckm, bsz, seq, heads, tq)

    h2, h2b, logits_t = _mix_out(
        y_ssm, y_att, h, w_glu[0].astype(BF16), row2(b_glu[0]), w_out[0].astype(BF16),
        row2(ln_mix_g[0]), row2(ln_mix_b[0]), w_router[0].T.astype(BF16), tm)

    tn = min(512, t)
    idx_t, gate_t, rank_t, counts = _route(logits_t, router_bias[0].astype(F32).reshape(N_EXPERTS, 1), tn)

    bm = 256
    counts = counts[:, 0].astype(jnp.int32)
    pcounts = (counts + bm - 1) // bm * bm
    pends = jnp.cumsum(pcounts)
    pstarts = pends - pcounts
    pos = pstarts[idx_t] + rank_t
    n_blocks = t * TOP_K // bm + N_EXPERTS
    tok = jnp.broadcast_to(jnp.arange(t, dtype=jnp.int32)[None, :], (TOP_K, t))
    buf_tok = jnp.zeros((n_blocks * bm,), jnp.int32).at[pos.reshape(-1)].set(tok.reshape(-1))
    bstart = jnp.arange(n_blocks, dtype=jnp.int32) * bm
    block_e = jnp.minimum(jnp.searchsorted(pends, bstart, side='right'), N_EXPERTS - 1).astype(jnp.int32)
    block_valid = (bstart < pends[-1]).astype(jnp.int32)
    block_first = jnp.concatenate([jnp.ones((1,), jnp.int32),
                                   (block_e[1:] != block_e[:-1]).astype(jnp.int32)])
    xs = h2b[buf_tok]
    ys = _moe_grouped(block_e, block_valid, block_first, xs, w_gate_exp[0], w_up_exp[0], w_down_exp[0], bm)
    routed = jnp.sum(ys[pos].astype(F32) * gate_t[:, :, None], axis=0)

    wgu_sh = jnp.concatenate([w_gate_sh[0], w_up_sh[0]], axis=1).astype(BF16)
    out = _final(h2, h2b, routed, wgu_sh, w_down_sh[0].astype(BF16), row2(ln_ffn_g[0]), row2(ln_ffn_b[0]), tm)
    return out.reshape(bsz, seq, d)
```

```python
import functools
import math

import jax
import jax.numpy as jnp
from jax import lax
from jax.experimental import pallas as pl
from jax.experimental.pallas import tpu as pltpu

N_META = 16
SSM_GROUP_CH = 16
SSM_STATE = 64
ATT_HEAD_DIM = 128
N_EXPERTS = 64
TOP_K = 8
N_EXPERT_GROUPS = 8
TOPK_GROUPS = 4
ROUTED_SCALE = 2.5
LN_EPS = 1e-5
DEPTH = 1
DEEPNORM_ALPHA = (2 * DEPTH) ** 0.25

S5_CHUNK = 16
LANES = 128
NEG_BIG = -1e30
LOG2E = 1.4426950408889634

F32 = jnp.float32
BF16 = jnp.bfloat16
HIGHEST = lax.Precision.HIGHEST


def _dot(a, b):
    return jnp.dot(a, b, preferred_element_type=F32)


def _dot_nt(a, b):
    return lax.dot_general(a, b, (((1,), (1,)), ((), ())), preferred_element_type=F32)


def _layer_norm(x, g, b):
    mu = jnp.mean(x, axis=-1, keepdims=True)
    xc = x - mu
    var = jnp.mean(xc * xc, axis=-1, keepdims=True)
    return xc * lax.rsqrt(var + LN_EPS) * g + b


def _params(sem, vmem_mb):
    return pltpu.CompilerParams(dimension_semantics=sem, vmem_limit_bytes=vmem_mb * 1024 * 1024)


def _const_spec(shape):
    nd = len(shape)
    return pl.BlockSpec(shape, lambda *_: (0,) * nd, pipeline_mode=pl.Buffered(1))


def _ln_inproj_kernel(x_ref, g_ref, b_ref, w_ref, wvt_ref, bf_ref, h_ref, u_ref, q_ref, k_ref, vt_ref, lf_ref,
                      *, ssm_w, att_w):
    h = _layer_norm(x_ref[...], g_ref[...], b_ref[...])
    h_ref[...] = h
    hb = h.astype(BF16)
    o = 0
    u_ref[...] = _dot(hb, w_ref[:, o:o + ssm_w]).astype(BF16)
    o += ssm_w
    q_ref[...] = (_dot(hb, w_ref[:, o:o + att_w]) * (LOG2E * ATT_HEAD_DIM ** -0.5)).astype(BF16)
    o += att_w
    k_ref[...] = _dot(hb, w_ref[:, o:o + att_w]).astype(BF16)
    o += att_w
    vt_ref[...] = _dot_nt(wvt_ref[...], hb).astype(BF16)
    f = _dot(hb, w_ref[:, o:o + LANES]) + bf_ref[...]
    lf_ref[...] = jnp.minimum(f, 0.0) - jnp.log(1.0 + jnp.exp(-jnp.abs(f)))


def _ln_inproj(x2d, g, b, w_bf, wvt_bf, bf_pad, tm, ssm_w, att_w):
    t, d = x2d.shape
    wcols = w_bf.shape[1]
    row = lambda i: (i, 0)
    return pl.pallas_call(
        functools.partial(_ln_inproj_kernel, ssm_w=ssm_w, att_w=att_w),
        grid=(t // tm,),
        in_specs=[
            pl.BlockSpec((tm, d), row),
            _const_spec((1, d)), _const_spec((1, d)),
            _const_spec((d, wcols)), _const_spec((att_w, d)), _const_spec((1, LANES)),
        ],
        out_specs=[
            pl.BlockSpec((tm, d), row),
            pl.BlockSpec((tm, ssm_w), row),
            pl.BlockSpec((tm, att_w), row),
            pl.BlockSpec((tm, att_w), row),
            pl.BlockSpec((att_w, tm), lambda i: (0, i)),
            pl.BlockSpec((tm, LANES), row),
        ],
        out_shape=[
            jax.ShapeDtypeStruct((t, d), F32),
            jax.ShapeDtypeStruct((t, ssm_w), BF16),
            jax.ShapeDtypeStruct((t, att_w), BF16),
            jax.ShapeDtypeStruct((t, att_w), BF16),
            jax.ShapeDtypeStruct((att_w, t), BF16),
            jax.ShapeDtypeStruct((t, LANES), F32),
        ],
        compiler_params=_params(("parallel",), 56),
        name="ln_inproj",
    )(x2d, g, b, w_bf, wvt_bf, bf_pad)


def _s5_tables(a_re, a_im, log_dt, b_re, b_im, c_re, c_im, d_skip, levels):
    c = S5_CHUNK
    g, p = a_re.shape
    hc = SSM_GROUP_CH
    lam = lax.complex(a_re.astype(F32), a_im.astype(F32))
    dt = jnp.exp(log_dt.astype(F32))[:, None]
    ldt = lam * dt
    abar = jnp.exp(ldt)
    bbar = ((abar - 1.0) / lam)[..., None] * lax.complex(b_re.astype(F32), b_im.astype(F32))
    cc = lax.complex(c_re.astype(F32), c_im.astype(F32))
    tau = jnp.arange(c + 1, dtype=F32)
    apow = jnp.exp(ldt[:, None, :] * tau[None, :, None])

    ca = cc[:, None, :, :] * apow[:, :c, None, :]
    taps = (jnp.einsum('gthp,gpk->gthk', ca.real, bbar.real, precision=HIGHEST)
            - jnp.einsum('gthp,gpk->gthk', ca.imag, bbar.imag, precision=HIGHEST))
    tt = jnp.arange(c)[:, None]
    ss = jnp.arange(c)[None, :]
    lag = tt - ss
    m = taps[:, jnp.clip(lag, 0, c - 1)]
    m = jnp.where((lag >= 0)[None, :, :, None, None], m, 0.0)
    eye_t = (lag == 0).astype(F32)[None, :, :, None, None]
    eye_h = jnp.eye(hc, dtype=F32)[None, None, None, :, :]
    m = m + eye_t * eye_h * d_skip.astype(F32)[:, None, None, :, None]
    m_t = m.transpose(0, 2, 4, 1, 3).reshape(g, c * hc, c * hc)

    bp = apow[:, c - 1 - jnp.arange(c), :][:, :, :, None] * bbar[:, None, :, :]
    bp = bp.transpose(0, 1, 3, 2).reshape(g, c * hc, p)
    bp_t = jnp.concatenate([bp.real, bp.imag], axis=-1)

    cp = cc[:, None, :, :] * apow[:, 1:c + 1, None, :]
    cp = cp.reshape(g, c * hc, p).transpose(0, 2, 1)
    cp_t = jnp.concatenate([cp.real, -cp.imag], axis=1)

    steps = (c * (2.0 ** jnp.arange(levels, dtype=F32)))
    alev = jnp.exp(ldt[:, None, :] * steps[None, :, None])
    pad = ((0, 0), (0, 16 - levels), (0, 0))
    a1 = jnp.pad(jnp.concatenate([alev.real, alev.real], axis=-1), pad)
    a2 = jnp.pad(jnp.concatenate([-alev.imag, alev.imag], axis=-1), pad)
    return m_t.astype(BF16), bp_t.astype(BF16), cp_t.astype(BF16), a1, a2


def _shift_rows(x, sh):
    n = x.shape[0]
    if sh % 8 == 0:
        return jnp.concatenate([jnp.zeros((sh, x.shape[1]), x.dtype), x[:n - sh]], axis=0)
    rolled = pltpu.roll(x, sh, axis=0)
    rows = lax.broadcasted_iota(jnp.int32, x.shape, 0)
    return jnp.where(rows < sh, 0.0, rolled)


def _cmul(a1, a2, x):
    return a1 * x + a2 * pltpu.roll(x, SSM_STATE, axis=1)


def _s5_kernel(u_ref, um_ref, mt_ref, bpt_ref, cpt_ref, a1_ref, a2_ref, y_ref, *, bsz, nc, levels):
    u = u_ref[0]
    bpt = bpt_ref[0]
    w = _dot(u, bpt)
    x_meta = _dot(um_ref[0], bpt)[:8]
    first = lax.broadcasted_iota(jnp.int32, (8, 2 * SSM_STATE), 0) == 0
    inject = jnp.where(first, _cmul(a1_ref[0, 0:1, :], a2_ref[0, 0:1, :], x_meta), 0.0)
    x_first = jnp.where(first, x_meta, 0.0)
    x_in = []
    for b in range(bsz):
        x = w[b * nc:(b + 1) * nc]
        x = jnp.concatenate([x[:8] + inject, x[8:]], axis=0)
        for lvl in range(levels):
            sh = 2 ** lvl
            x = x + _cmul(a1_ref[0, lvl:lvl + 1, :], a2_ref[0, lvl:lvl + 1, :], _shift_rows(x, sh))
        xp = _shift_rows(x, 1)
        x_in.append(jnp.concatenate([xp[:8] + x_first, xp[8:]], axis=0))
    x_in = jnp.concatenate(x_in, axis=0).astype(BF16)
    y = _dot(u, mt_ref[0]) + _dot(x_in, cpt_ref[0])
    y_ref[0] = jax.nn.gelu(y).astype(BF16)


def _s5(u_g, um_g, m_t, bp_t, cp_t, a1, a2, bsz, nc, levels):
    g, rows, cw = u_g.shape
    nl = a1.shape[1]
    grp = lambda i: (i, 0, 0)
    return pl.pallas_call(
        functools.partial(_s5_kernel, bsz=bsz, nc=nc, levels=levels),
        grid=(g,),
        in_specs=[
            pl.BlockSpec((1, rows, cw), grp),
            pl.BlockSpec((1, 16, cw), grp),
            pl.BlockSpec((1, cw, cw), grp),
            pl.BlockSpec((1, cw, 2 * SSM_STATE), grp),
            pl.BlockSpec((1, 2 * SSM_STATE, cw), grp),
            pl.BlockSpec((1, nl, 2 * SSM_STATE), grp),
            pl.BlockSpec((1, nl, 2 * SSM_STATE), grp),
        ],
        out_specs=pl.BlockSpec((1, rows, cw), grp),
        out_shape=jax.ShapeDtypeStruct((g, rows, cw), BF16),
        compiler_params=_params(("parallel",), 32),
        name="s5_scan",
    )(u_g, um_g, m_t, bp_t, cp_t, a1, a2)


def _fox_kernel(q_ref, k_ref, va_ref, ck_ref, km_ref, vam_ref, ckm_ref, o_ref,
                m_scr, acc_scr, s_scr, p_scr, al_scr, *, tq, dh):
    qi = pl.program_id(2)
    q = q_ref[...]
    reps = tq // LANES

    def scores(kb, ckb):
        return _dot_nt(kb, q) - jnp.concatenate([ckb] * reps, axis=1)

    def softmax_update(s):
        m_prev = m_scr[0:1, :]
        m_new = jnp.maximum(m_prev, jnp.max(s, axis=0, keepdims=True))
        m_scr[...] = jnp.broadcast_to(m_new, m_scr.shape)
        return jnp.exp2(m_prev - m_new), jnp.exp2(s - m_new).astype(BF16)

    s = scores(km_ref[...], ckm_ref[0])
    m0 = jnp.max(s, axis=0, keepdims=True)
    m_scr[...] = jnp.broadcast_to(m0, m_scr.shape)
    acc_scr[...] = _dot(vam_ref[0], jnp.exp2(s - m0).astype(BF16))

    start = pl.multiple_of(qi * tq, tq)
    s = scores(k_ref[pl.ds(start, tq), :], ck_ref[0, 0, pl.ds(start, tq), :])
    key = lax.broadcasted_iota(jnp.int32, (tq, tq), 0)
    qry = lax.broadcasted_iota(jnp.int32, (tq, tq), 1)
    alpha, p = softmax_update(jnp.where(key <= qry, s, NEG_BIG))
    acc_scr[...] = alpha * acc_scr[...] + _dot(va_ref[0, 0, qi], p)

    s_scr[1] = jnp.full((tq, tq), NEG_BIG, F32)
    p_scr[0] = jnp.zeros((tq, tq), BF16)
    al_scr[0] = jnp.ones((8, tq), F32)

    def tick(t, slot):
        other = 1 - slot
        jc = jnp.maximum(t - 2, 0)
        acc_scr[...] = al_scr[slot, 0:1, :] * acc_scr[...] + _dot(va_ref[0, 0, jc], p_scr[slot])
        alpha, p = softmax_update(s_scr[other])
        p_scr[other] = p
        al_scr[other] = jnp.broadcast_to(alpha, (8, tq))
        ja = jnp.minimum(t, qi - 1)
        start = pl.multiple_of(ja * tq, tq)
        off = jnp.where(t < qi, 0.0, -NEG_BIG)
        s_scr[slot] = scores(k_ref[pl.ds(start, tq), :], ck_ref[0, 0, pl.ds(start, tq), :] + off)

    def body(i, carry):
        tick(2 * i, 0)
        tick(2 * i + 1, 1)
        return carry

    lax.fori_loop(0, jnp.where(qi > 0, (qi + 3) // 2, 0), body, 0)
    acc = acc_scr[...]
    o_ref[...] = (acc[:dh] / acc[dh:dh + 1]).T.astype(BF16)


def _fox(q, k, va, ck, km, vam, ckm, bsz, seq, heads, tq):
    t, aw = q.shape
    nq = seq // tq
    dh = ATT_HEAD_DIM
    da = va.shape[3]
    return pl.pallas_call(
        functools.partial(_fox_kernel, tq=tq, dh=dh),
        grid=(bsz, heads, nq),
        in_specs=[
            pl.BlockSpec((tq, dh), lambda b, h, i: (b * nq + i, h)),
            pl.BlockSpec((seq, dh), lambda b, h, i: (b, h)),
            pl.BlockSpec((1, 1, nq, da, tq), lambda b, h, i: (b, h, 0, 0, 0)),
            pl.BlockSpec((1, 1, seq, LANES), lambda b, h, i: (b, h, 0, 0)),
            pl.BlockSpec((LANES, dh), lambda b, h, i: (0, h)),
            pl.BlockSpec((1, da, LANES), lambda b, h, i: (h, 0, 0)),
            pl.BlockSpec((1, LANES, LANES), lambda b, h, i: (h, 0, 0)),
        ],
        out_specs=pl.BlockSpec((tq, dh), lambda b, h, i: (b * nq + i, h)),
        out_shape=jax.ShapeDtypeStruct((t, aw), BF16),
        scratch_shapes=[
            pltpu.VMEM((8, tq), F32),
            pltpu.VMEM((da, tq), F32),
            pltpu.VMEM((2, tq, tq), F32),
            pltpu.VMEM((2, tq, tq), BF16),
            pltpu.VMEM((2, 8, tq), F32),
        ],
        compiler_params=_params(("parallel", "parallel", "arbitrary"), 48),
        name="fox_attention",
    )(q, k, va, ck, km, vam, ckm)


def _mix_out_kernel(ys_ref, ya_ref, h_ref, wglu_ref, bglu_ref, wout_ref, g_ref, b_ref, wr_ref,
                    h2_ref, h2b_ref, lt_ref, *, ssm_w):
    ys = ys_ref[...]
    z = _dot(ys, wglu_ref[...]) + bglu_ref[...]
    ysf = ys.astype(F32)
    yg = (ysf * jax.nn.sigmoid(z)).astype(BF16)
    mix = _dot(yg, wout_ref[0:ssm_w, :]) + _dot(ya_ref[...], wout_ref[ssm_w:, :])
    h2 = _layer_norm(DEEPNORM_ALPHA * h_ref[...] + mix, g_ref[...], b_ref[...])
    h2_ref[...] = h2
    h2b = h2.astype(BF16)
    h2b_ref[...] = h2b
    lt_ref[...] = _dot_nt(wr_ref[...], h2b)


def _mix_out(ys, ya, h, wglu, bglu, wout, g, b, wr_t, tm):
    t, d = h.shape
    ssm_w = ys.shape[1]
    att_w = ya.shape[1]
    e = wr_t.shape[0]
    row = lambda i: (i, 0)
    return pl.pallas_call(
        functools.partial(_mix_out_kernel, ssm_w=ssm_w),
        grid=(t // tm,),
        in_specs=[
            pl.BlockSpec((tm, ssm_w), row),
            pl.BlockSpec((tm, att_w), row),
            pl.BlockSpec((tm, d), row),
            _const_spec((ssm_w, ssm_w)), _const_spec((1, ssm_w)),
            _const_spec((d, d)), _const_spec((1, d)), _const_spec((1, d)),
            _const_spec((e, d)),
        ],
        out_specs=[
            pl.BlockSpec((tm, d), row),
            pl.BlockSpec((tm, d), row),
            pl.BlockSpec((e, tm), lambda i: (0, i)),
        ],
        out_shape=[
            jax.ShapeDtypeStruct((t, d), F32),
            jax.ShapeDtypeStruct((t, d), BF16),
            jax.ShapeDtypeStruct((e, t), F32),
        ],
        compiler_params=_params(("parallel",), 48),
        name="mix_out",
    )(ys, ya, h, wglu, bglu, wout, g, b, wr_t)


def _route_kernel(lt_ref, bias_ref, idx_ref, gate_ref, rank_ref, cnt_ref, run_scr, *, tn):
    e = N_EXPERTS
    ng = N_EXPERT_GROUPS
    gs = e // ng

    @pl.when(pl.program_id(0) == 0)
    def _():
        run_scr[...] = jnp.zeros_like(run_scr)

    scores = jax.nn.sigmoid(lt_ref[...])
    sel = scores + bias_ref[...]
    sel3 = sel.reshape(ng, gs, tn)
    mem = lax.broadcasted_iota(jnp.int32, (ng, gs, tn), 1)
    m1 = jnp.max(sel3, axis=1, keepdims=True)
    i1 = jnp.min(jnp.where(sel3 == m1, mem, gs), axis=1, keepdims=True)
    m2 = jnp.max(jnp.where(mem == i1, -jnp.inf, sel3), axis=1, keepdims=True)
    gscore = (m1 + m2).reshape(ng, tn)

    gio = lax.broadcasted_iota(jnp.int32, (ng, tn), 0)
    gmask = jnp.zeros((ng, tn), F32)
    for _ in range(TOPK_GROUPS):
        mx = jnp.max(gscore, axis=0, keepdims=True)
        gi = jnp.min(jnp.where(gscore == mx, gio, ng), axis=0, keepdims=True)
        hit = gio == gi
        gmask = jnp.where(hit, 1.0, gmask)
        gscore = jnp.where(hit, -jnp.inf, gscore)

    cand = jnp.where(gmask.reshape(ng, 1, tn) > 0.5, sel3, -jnp.inf).reshape(e, tn)
    eio = lax.broadcasted_iota(jnp.int32, (e, tn), 0)
    hits = []
    chosen = jnp.zeros((e, tn), F32)
    for _ in range(TOP_K):
        mx = jnp.max(cand, axis=0, keepdims=True)
        ei = jnp.min(jnp.where(cand == mx, eio, e), axis=0, keepdims=True)
        hit = eio == ei
        hits.append((ei, hit))
        chosen = jnp.where(hit, 1.0, chosen)
        cand = jnp.where(hit, -jnp.inf, cand)

    onehot = chosen.astype(BF16)
    r = lax.broadcasted_iota(jnp.int32, (tn, tn), 0)
    c = lax.broadcasted_iota(jnp.int32, (tn, tn), 1)
    tri = (r < c).astype(BF16)
    rank = _dot(onehot, tri) + run_scr[:, 0:1]
    run_scr[...] = run_scr[...] + jnp.sum(chosen, axis=1, keepdims=True)
    cnt_ref[...] = run_scr[...]

    gates = [jnp.sum(jnp.where(hit, scores, 0.0), axis=0, keepdims=True) for _, hit in hits]
    total = gates[0]
    for gk in gates[1:]:
        total = total + gk
    for kk, (ei, hit) in enumerate(hits):
        idx_ref[kk:kk + 1, :] = ei
        gate_ref[kk:kk + 1, :] = gates[kk] / total * ROUTED_SCALE
        rank_ref[kk:kk + 1, :] = jnp.sum(jnp.where(hit, rank, 0.0), axis=0, keepdims=True).astype(jnp.int32)


def _route(logits_t, bias, tn):
    e, t = logits_t.shape
    col = lambda i: (0, i)
    return pl.pallas_call(
        functools.partial(_route_kernel, tn=tn),
        grid=(t // tn,),
        in_specs=[pl.BlockSpec((e, tn), col), _const_spec((e, 1))],
        out_specs=[
            pl.BlockSpec((TOP_K, tn), col),
            pl.BlockSpec((TOP_K, tn), col),
            pl.BlockSpec((TOP_K, tn), col),
            pl.BlockSpec((e, LANES), lambda i: (0, 0)),
        ],
        out_shape=[
            jax.ShapeDtypeStruct((TOP_K, t), jnp.int32),
            jax.ShapeDtypeStruct((TOP_K, t), F32),
            jax.ShapeDtypeStruct((TOP_K, t), jnp.int32),
            jax.ShapeDtypeStruct((e, LANES), F32),
        ],
        scratch_shapes=[pltpu.VMEM((e, LANES), F32)],
        compiler_params=_params(("arbitrary",), 32),
        name="route",
    )(logits_t, bias)


def _moe_kernel(be_ref, bv_ref, bf_ref, x_ref, wg_ref, wu_ref, wd_ref, y_ref, wg_s, wu_s, wd_s):
    b = pl.program_id(0)

    @pl.when(bf_ref[b] == 1)
    def _():
        wg_s[...] = wg_ref[0].astype(BF16)
        wu_s[...] = wu_ref[0].astype(BF16)
        wd_s[...] = wd_ref[0].astype(BF16)

    @pl.when(bv_ref[b] == 1)
    def _():
        x = x_ref[...]
        hid = jax.nn.silu(_dot(x, wg_s[...])) * _dot(x, wu_s[...])
        y_ref[...] = _dot(hid.astype(BF16), wd_s[...]).astype(BF16)

    @pl.when(bv_ref[b] == 0)
    def _():
        y_ref[...] = jnp.zeros_like(y_ref)


def _moe_grouped(block_e, block_valid, block_first, xs, wg, wu, wd, bm):
    rows, d = xs.shape
    ff = wg.shape[2]
    n_blocks = rows // bm
    grid_spec = pltpu.PrefetchScalarGridSpec(
        num_scalar_prefetch=3,
        grid=(n_blocks,),
        in_specs=[
            pl.BlockSpec((bm, d), lambda b, be, bv, bf: (b, 0)),
            pl.BlockSpec((1, d, ff), lambda b, be, bv, bf: (be[b], 0, 0)),
            pl.BlockSpec((1, d, ff), lambda b, be, bv, bf: (be[b], 0, 0)),
            pl.BlockSpec((1, ff, d), lambda b, be, bv, bf: (be[b], 0, 0)),
        ],
        out_specs=pl.BlockSpec((bm, d), lambda b, be, bv, bf: (b, 0)),
        scratch_shapes=[
            pltpu.VMEM((d, ff), BF16),
            pltpu.VMEM((d, ff), BF16),
            pltpu.VMEM((ff, d), BF16),
        ],
    )
    return pl.pallas_call(
        _moe_kernel,
        grid_spec=grid_spec,
        out_shape=jax.ShapeDtypeStruct((rows, d), BF16),
        compiler_params=_params(("arbitrary",), 48),
        name="moe_grouped",
    )(block_e, block_valid, block_first, xs, wg, wu, wd)


def _final_kernel(h2_ref, h2b_ref, r_ref, wgu_ref, wd_ref, g_ref, b_ref, o_ref, *, ff):
    x = h2b_ref[...]
    gu = _dot(x, wgu_ref[...])
    hid = jax.nn.silu(gu[:, :ff]) * gu[:, ff:]
    shared = _dot(hid.astype(BF16), wd_ref[...])
    ffn = r_ref[...] + shared
    o_ref[...] = _layer_norm(DEEPNORM_ALPHA * h2_ref[...] + ffn, g_ref[...], b_ref[...])


def _final(h2, h2b, routed, wgu, wd, g, b, tm):
    t, d = h2.shape
    ff = wd.shape[0]
    row = lambda i: (i, 0)
    return pl.pallas_call(
        functools.partial(_final_kernel, ff=ff),
        grid=(t // tm,),
        in_specs=[
            pl.BlockSpec((tm, d), row), pl.BlockSpec((tm, d), row), pl.BlockSpec((tm, d), row),
            _const_spec((d, 2 * ff)), _const_spec((ff, d)), _const_spec((1, d)), _const_spec((1, d)),
        ],
        out_specs=pl.BlockSpec((tm, d), row),
        out_shape=jax.ShapeDtypeStruct((t, d), F32),
        compiler_params=_params(("parallel",), 48),
        name="final",
    )(h2, h2b, routed, wgu, wd, g, b)


def kernel(x, meta_tokens, ln_in_g, ln_in_b, w_in, b_forget, ssm_a_re, ssm_a_im, ssm_log_dt, ssm_b_re, ssm_b_im, ssm_c_re, ssm_c_im, ssm_d, w_glu, b_glu, w_out, ln_mix_g, ln_mix_b, w_router, router_bias, w_gate_exp, w_up_exp, w_down_exp, w_gate_sh, w_up_sh, w_down_sh, ln_ffn_g, ln_ffn_b):
    bsz, seq, d = x.shape
    t = bsz * seq
    ssm_w = w_glu.shape[1]
    heads = b_forget.shape[1]
    att_w = (w_in.shape[2] - ssm_w - heads) // 3
    groups = ssm_a_re.shape[1]
    assert meta_tokens.shape[0] == N_META == S5_CHUNK and att_w == heads * ATT_HEAD_DIM
    nc = seq // S5_CHUNK
    levels = int(math.log2(nc))
    assert 2 ** levels == nc and nc % 8 == 0
    row2 = lambda a: a.reshape(1, -1).astype(F32)

    n_uqk = ssm_w + 2 * att_w
    w_f = jnp.pad(w_in[0][:, n_uqk + att_w:], ((0, 0), (0, LANES - heads)))
    w_in_bf = jnp.concatenate([w_in[0][:, :n_uqk], w_f], axis=1).astype(BF16)
    wvt_bf = w_in[0][:, n_uqk:n_uqk + att_w].T.astype(BF16)
    bf_pad = jnp.pad(b_forget[0].astype(F32), (0, LANES - heads)).reshape(1, LANES)
    g_in, b_in = row2(ln_in_g), row2(ln_in_b)
    tm = min(256, t)
    h, u, q, k, vt, lf = _ln_inproj(x.reshape(t, d), g_in, b_in, w_in_bf, wvt_bf, bf_pad, tm, ssm_w, att_w)
    _, u_m, _, k_m, vt_m, lf_m = _ln_inproj(meta_tokens.astype(F32), g_in, b_in, w_in_bf, wvt_bf, bf_pad, N_META,
                                            ssm_w, att_w)

    m_t, bp_t, cp_t, a1, a2 = _s5_tables(ssm_a_re[0], ssm_a_im[0], ssm_log_dt[0], ssm_b_re[0], ssm_b_im[0],
                                         ssm_c_re[0], ssm_c_im[0], ssm_d[0], levels)
    cw = S5_CHUNK * SSM_GROUP_CH
    u_g = u.reshape(bsz * nc, S5_CHUNK, groups, SSM_GROUP_CH).transpose(2, 0, 1, 3).reshape(groups, bsz * nc, cw)
    um_g = u_m.reshape(1, S5_CHUNK, groups, SSM_GROUP_CH).transpose(2, 0, 1, 3).reshape(groups, 1, cw)
    um_g = jnp.broadcast_to(um_g, (groups, 16, cw))
    y_g = _s5(u_g, um_g, m_t, bp_t, cp_t, a1, a2, bsz, nc, levels)
    y_ssm = y_g.reshape(groups, bsz * nc, S5_CHUNK, SSM_GROUP_CH).transpose(1, 2, 0, 3).reshape(t, ssm_w)

    tq = min(512, seq)
    nq = seq // tq
    dh = ATT_HEAD_DIM
    ones_rows = 16
    lfm = lf_m[:, :heads] * LOG2E
    c_meta = jnp.cumsum(lfm, axis=0) - jnp.sum(lfm, axis=0, keepdims=True)
    ckm = jnp.full((heads, LANES), -NEG_BIG, F32).at[:, :N_META].set(c_meta.T)
    ckm = jnp.broadcast_to(ckm[:, :, None], (heads, LANES, LANES))
    c_main = jnp.cumsum(lf[:, :heads].reshape(bsz, seq, heads) * LOG2E, axis=1)
    ck = jnp.broadcast_to(c_main.transpose(0, 2, 1)[..., None], (bsz, heads, seq, LANES))
    km = jnp.pad(k_m, ((0, LANES - N_META), (0, 0)))
    va = vt.reshape(heads, dh, bsz, nq, tq).transpose(2, 0, 3, 1, 4)
    va = jnp.concatenate([va, jnp.ones((bsz, heads, nq, ones_rows, tq), BF16)], axis=3)
    vam = jnp.pad(vt_m.reshape(heads, dh, N_META), ((0, 0), (0, 0), (0, LANES - N_META)))
    vam = jnp.concatenate([vam, jnp.ones((heads, ones_rows, LANES), BF16)], axis=1)
    y_att = _fox(q, k, va, ck, km, vam, ckm, bsz, seq, heads, tq)

    h2, h2b, logits_t = _mix_out(
        y_ssm, y_att, h, w_glu[0].astype(BF16), row2(b_glu[0]), w_out[0].astype(BF16),
        row2(ln_mix_g[0]), row2(ln_mix_b[0]), w_router[0].T.astype(BF16), tm)

    tn = min(512, t)
    idx_t, gate_t, rank_t, counts = _route(logits_t, router_bias[0].astype(F32).reshape(N_EXPERTS, 1), tn)

    bm = 256
    counts = counts[:, 0].astype(jnp.int32)
    pcounts = (counts + bm - 1) // bm * bm
    pends = jnp.cumsum(pcounts)
    pstarts = pends - pcounts
    pos = pstarts[idx_t] + rank_t
    n_blocks = t * TOP_K // bm + N_EXPERTS
    tok = jnp.broadcast_to(jnp.arange(t, dtype=jnp.int32)[None, :], (TOP_K, t))
    buf_tok = jnp.zeros((n_blocks * bm,), jnp.int32).at[pos.reshape(-1)].set(tok.reshape(-1))
    bstart = jnp.arange(n_blocks, dtype=jnp.int32) * bm
    block_e = jnp.minimum(jnp.searchsorted(pends, bstart, side='right'), N_EXPERTS - 1).astype(jnp.int32)
    block_valid = (bstart < pends[-1]).astype(jnp.int32)
    block_first = jnp.concatenate([jnp.ones((1,), jnp.int32),
                                   (block_e[1:] != block_e[:-1]).astype(jnp.int32)])
    xs = h2b[buf_tok]
    ys = _moe_grouped(block_e, block_valid, block_first, xs, w_gate_exp[0], w_up_exp[0], w_down_exp[0], bm)
    routed = jnp.sum(ys[pos].astype(F32) * gate_t[:, :, None], axis=0)

    wgu_sh = jnp.concatenate([w_gate_sh[0], w_up_sh[0]], axis=1).astype(BF16)
    out = _final(h2, h2b, routed, wgu_sh, w_down_sh[0].astype(BF16), row2(ln_ffn_g[0]), row2(ln_ffn_b[0]), tm)
    return out.reshape(bsz, seq, d)
```

```python
import functools
import math

import jax
import jax.numpy as jnp
from jax import lax
from jax.experimental import pallas as pl
from jax.experimental.pallas import tpu as pltpu

N_META = 16
SSM_GROUP_CH = 16
SSM_STATE = 64
ATT_HEAD_DIM = 128
N_EXPERTS = 64
TOP_K = 8
N_EXPERT_GROUPS = 8
TOPK_GROUPS = 4
ROUTED_SCALE = 2.5
LN_EPS = 1e-5
DEPTH = 1
DEEPNORM_ALPHA = (2 * DEPTH) ** 0.25

S5_CHUNK = 16
LANES = 128
ROW_TILE = 8
NEG_BIG = -1e30
LOG2E = 1.4426950408889634

F32 = jnp.float32
BF16 = jnp.bfloat16
HIGHEST = lax.Precision.HIGHEST


def _dot(a, b):
    return jnp.dot(a, b, preferred_element_type=F32)


def _dot_nt(a, b):
    return lax.dot_general(a, b, (((1,), (1,)), ((), ())), preferred_element_type=F32)


def _layer_norm(x, g, b):
    mu = jnp.mean(x, axis=-1, keepdims=True)
    xc = x - mu
    var = jnp.mean(xc * xc, axis=-1, keepdims=True)
    return xc * lax.rsqrt(var + LN_EPS) * g + b


def _pack_rows(x, o_ref):
    m, w = x.shape
    half = w // 2
    assert half == ROW_TILE * LANES
    bits = lax.bitcast_convert_type(x.astype(BF16).astype(F32), jnp.uint32)
    packed = bits[:, half:] | (bits[:, :half] >> 16)
    for s in range(ROW_TILE):
        o_ref[pl.ds(s, m, stride=ROW_TILE), :] = packed[:, s * LANES:(s + 1) * LANES]


def _unpack_rows(x_ref, base, m):
    lo, hi = [], []
    for s in range(ROW_TILE):
        w = x_ref[pl.ds(base + s, m, stride=ROW_TILE), :]
        lo.append(lax.bitcast_convert_type(w << 16, F32))
        hi.append(lax.bitcast_convert_type(w & jnp.uint32(0xFFFF0000), F32))
    return lo, hi


def _params(sem, vmem_mb):
    return pltpu.CompilerParams(dimension_semantics=sem, vmem_limit_bytes=vmem_mb * 1024 * 1024)


def _const_spec(shape):
    nd = len(shape)
    return pl.BlockSpec(shape, lambda *_: (0,) * nd, pipeline_mode=pl.Buffered(1))


def _ln_inproj_kernel(x_ref, g_ref, b_ref, w_ref, wvt_ref, bf_ref, h_ref, u_ref, q_ref, k_ref, vt_ref, lf_ref,
                      *, ssm_w, att_w):
    h = _layer_norm(x_ref[...], g_ref[...], b_ref[...])
    h_ref[...] = h
    hb = h.astype(BF16)
    o = 0
    u_ref[...] = _dot(hb, w_ref[:, o:o + ssm_w])
    o += ssm_w
    q_ref[...] = (_dot(hb, w_ref[:, o:o + att_w]) * (LOG2E * ATT_HEAD_DIM ** -0.5)).astype(BF16)
    o += att_w
    k_ref[...] = _dot(hb, w_ref[:, o:o + att_w]).astype(BF16)
    o += att_w
    vt_ref[...] = _dot_nt(wvt_ref[...], hb).astype(BF16)
    f = _dot(hb, w_ref[:, o:o + LANES]) + bf_ref[...]
    lf_ref[...] = jnp.minimum(f, 0.0) - jnp.log(1.0 + jnp.exp(-jnp.abs(f)))


def _ln_inproj(x2d, g, b, w_bf, wvt_bf, bf_pad, tm, ssm_w, att_w):
    t, d = x2d.shape
    wcols = w_bf.shape[1]
    row = lambda i: (i, 0)
    return pl.pallas_call(
        functools.partial(_ln_inproj_kernel, ssm_w=ssm_w, att_w=att_w),
        grid=(t // tm,),
        in_specs=[
            pl.BlockSpec((tm, d), row),
            _const_spec((1, d)), _const_spec((1, d)),
            _const_spec((d, wcols)), _const_spec((att_w, d)), _const_spec((1, LANES)),
        ],
        out_specs=[
            pl.BlockSpec((tm, d), row),
            pl.BlockSpec((tm, ssm_w), row),
            pl.BlockSpec((tm, att_w), row),
            pl.BlockSpec((tm, att_w), row),
            pl.BlockSpec((att_w, tm), lambda i: (0, i)),
            pl.BlockSpec((tm, LANES), row),
        ],
        out_shape=[
            jax.ShapeDtypeStruct((t, d), F32),
            jax.ShapeDtypeStruct((t, ssm_w), F32),
            jax.ShapeDtypeStruct((t, att_w), BF16),
            jax.ShapeDtypeStruct((t, att_w), BF16),
            jax.ShapeDtypeStruct((att_w, t), BF16),
            jax.ShapeDtypeStruct((t, LANES), F32),
        ],
        compiler_params=_params(("parallel",), 56),
        name="ln_inproj",
    )(x2d, g, b, w_bf, wvt_bf, bf_pad)


def _s5_tables(a_re, a_im, log_dt, b_re, b_im, c_re, c_im, d_skip, levels):
    c = S5_CHUNK
    g, p = a_re.shape
    hc = SSM_GROUP_CH
    lam = lax.complex(a_re.astype(F32), a_im.astype(F32))
    dt = jnp.exp(log_dt.astype(F32))[:, None]
    ldt = lam * dt
    abar = jnp.exp(ldt)
    bbar = ((abar - 1.0) / lam)[..., None] * lax.complex(b_re.astype(F32), b_im.astype(F32))
    cc = lax.complex(c_re.astype(F32), c_im.astype(F32))
    tau = jnp.arange(c + 1, dtype=F32)
    apow = jnp.exp(ldt[:, None, :] * tau[None, :, None])

    ca = cc[:, None, :, :] * apow[:, :c, None, :]
    taps = (jnp.einsum('gthp,gpk->gthk', ca.real, bbar.real, precision=HIGHEST)
            - jnp.einsum('gthp,gpk->gthk', ca.imag, bbar.imag, precision=HIGHEST))
    tt = jnp.arange(c)[:, None]
    ss = jnp.arange(c)[None, :]
    lag = tt - ss
    m = taps[:, jnp.clip(lag, 0, c - 1)]
    m = jnp.where((lag >= 0)[None, :, :, None, None], m, 0.0)
    eye_t = (lag == 0).astype(F32)[None, :, :, None, None]
    eye_h = jnp.eye(hc, dtype=F32)[None, None, None, :, :]
    m = m + eye_t * eye_h * d_skip.astype(F32)[:, None, None, :, None]
    m_t = m.transpose(0, 2, 4, 1, 3).reshape(g, c * hc, c * hc)

    bp = apow[:, c - 1 - jnp.arange(c), :][:, :, :, None] * bbar[:, None, :, :]
    bp = bp.transpose(0, 1, 3, 2).reshape(g, c * hc, p)
    bp_t = jnp.concatenate([bp.real, bp.imag], axis=-1)

    cp = cc[:, None, :, :] * apow[:, 1:c + 1, None, :]
    cp = cp.reshape(g, c * hc, p).transpose(0, 2, 1)
    cp_t = jnp.concatenate([cp.real, -cp.imag], axis=1)

    steps = (c * (2.0 ** jnp.arange(levels, dtype=F32)))
    alev = jnp.exp(ldt[:, None, :] * steps[None, :, None])

    gl = LANES // hc
    nj = g // gl
    eye = jnp.eye(gl, dtype=BF16)
    m6 = m_t.astype(BF16).reshape(nj, gl, c, hc, c, hc)
    mj = jnp.einsum('jgskth,gG->jsgktGh', m6, eye).reshape(nj, c * LANES, c * LANES)
    bp6 = bp_t.astype(BF16).reshape(nj, gl, c, hc, 2, p)
    bpj = jnp.einsum('jgskrp,gG->jsgkrGp', bp6, eye).reshape(nj, c * LANES, 2 * gl * p)
    cp6 = cp_t.astype(BF16).reshape(nj, gl, 2, p, c, hc)
    cpj = jnp.einsum('jgrpth,gG->jrgptGh', cp6, eye).reshape(nj, 2 * gl * p, c * LANES)
    al = alev.reshape(nj, gl, levels, p).transpose(0, 2, 1, 3).reshape(nj, levels, gl * p)
    pad = ((0, 0), (0, 16 - levels), (0, 0))
    a1 = jnp.pad(jnp.concatenate([al.real, al.real], axis=-1), pad)
    a2 = jnp.pad(jnp.concatenate([-al.imag, al.imag], axis=-1), pad)
    return mj, bpj, cpj, a1, a2


def _shift_rows(x, sh):
    n = x.shape[0]
    if sh % 8 == 0:
        return jnp.concatenate([jnp.zeros((sh, x.shape[1]), x.dtype), x[:n - sh]], axis=0)
    rolled = pltpu.roll(x, sh, axis=0)
    rows = lax.broadcasted_iota(jnp.int32, x.shape, 0)
    return jnp.where(rows < sh, 0.0, rolled)


def _cmul(a1, a2, x):
    return a1 * x + a2 * pltpu.roll(x, x.shape[1] // 2, axis=1)


def _s5_kernel(u_ref, um_ref, mj_ref, bpj_ref, cpj_ref, a1_ref, a2_ref, y_ref, *, nc, levels):
    c = S5_CHUNK
    u = jnp.concatenate([u_ref[pl.ds(s, nc, stride=c), :].astype(BF16) for s in range(c)], axis=1)
    bpj = bpj_ref[0]
    w = _dot(u, bpj)
    um = jnp.concatenate([um_ref[s:s + 1, :] for s in range(c)], axis=1)
    x_meta = _dot(jnp.broadcast_to(um, (8, um.shape[1])).astype(BF16), bpj)
    first = lax.broadcasted_iota(jnp.int32, x_meta.shape, 0) == 0
    inject = jnp.where(first, _cmul(a1_ref[0, 0:1, :], a2_ref[0, 0:1, :], x_meta), 0.0)
    x = jnp.concatenate([w[:8] + inject, w[8:]], axis=0)
    for lvl in range(levels):
        x = x + _cmul(a1_ref[0, lvl:lvl + 1, :], a2_ref[0, lvl:lvl + 1, :], _shift_rows(x, 2 ** lvl))
    xp = _shift_rows(x, 1)
    x_in = jnp.concatenate([xp[:8] + jnp.where(first, x_meta, 0.0), xp[8:]], axis=0).astype(BF16)
    y = jax.nn.gelu(_dot(u, mj_ref[0]) + _dot(x_in, cpj_ref[0]))
    for s in range(c):
        y_ref[pl.ds(s, nc, stride=c), :] = y[:, s * LANES:(s + 1) * LANES]


def _s5(u, u_m, mj, bpj, cpj, a1, a2, bsz, nc, levels):
    t, ssm_w = u.shape
    nj, cl, sw = bpj.shape
    nl = a1.shape[1]
    rows = nc * S5_CHUNK
    tab = lambda j, b: (j, 0, 0)
    one = pl.Buffered(1)
    return pl.pallas_call(
        functools.partial(_s5_kernel, nc=nc, levels=levels),
        grid=(nj, bsz),
        in_specs=[
            pl.BlockSpec((rows, LANES), lambda j, b: (b, j)),
            pl.BlockSpec((S5_CHUNK, LANES), lambda j, b: (0, j)),
            pl.BlockSpec((1, cl, cl), tab, pipeline_mode=one),
            pl.BlockSpec((1, cl, sw), tab, pipeline_mode=one),
            pl.BlockSpec((1, sw, cl), tab, pipeline_mode=one),
            pl.BlockSpec((1, nl, sw), tab),
            pl.BlockSpec((1, nl, sw), tab),
        ],
        out_specs=pl.BlockSpec((rows, LANES), lambda j, b: (b, j)),
        out_shape=jax.ShapeDtypeStruct((t, ssm_w), F32),
        compiler_params=_params(("parallel", "parallel"), 56),
        name="s5_scan",
    )(u, u_m, mj, bpj, cpj, a1, a2)


def _fox_kernel(q_ref, k_ref, va_ref, ck_ref, km_ref, vam_ref, ckm_ref, o_ref,
                m_scr, acc_scr, s_scr, p_scr, al_scr, *, tq, dh):
    qi = pl.program_id(2)
    q = q_ref[...]
    reps = tq // LANES

    def scores(kb, ckb):
        return _dot_nt(kb, q) - jnp.concatenate([ckb] * reps, axis=1)

    def softmax_update(s):
        m_prev = m_scr[0:1, :]
        m_new = jnp.maximum(m_prev, jnp.max(s, axis=0, keepdims=True))
        m_scr[...] = jnp.broadcast_to(m_new, m_scr.shape)
        return jnp.exp2(m_prev - m_new), jnp.exp2(s - m_new).astype(BF16)

    s = scores(km_ref[...], ckm_ref[0])
    m0 = jnp.max(s, axis=0, keepdims=True)
    m_scr[...] = jnp.broadcast_to(m0, m_scr.shape)
    acc_scr[...] = _dot(vam_ref[0], jnp.exp2(s - m0).astype(BF16))

    start = pl.multiple_of(qi * tq, tq)
    s = scores(k_ref[pl.ds(start, tq), :], ck_ref[0, 0, pl.ds(start, tq), :])
    key = lax.broadcasted_iota(jnp.int32, (tq, tq), 0)
    qry = lax.broadcasted_iota(jnp.int32, (tq, tq), 1)
    alpha, p = softmax_update(jnp.where(key <= qry, s, NEG_BIG))
    acc_scr[...] = alpha * acc_scr[...] + _dot(va_ref[0, 0, qi], p)

    s_scr[1] = jnp.full((tq, tq), NEG_BIG, F32)
    p_scr[0] = jnp.zeros((tq, tq), BF16)
    al_scr[0] = jnp.ones((8, tq), F32)

    def tick(t, slot):
        other = 1 - slot
        jc = jnp.maximum(t - 2, 0)
        acc_scr[...] = al_scr[slot, 0:1, :] * acc_scr[...] + _dot(va_ref[0, 0, jc], p_scr[slot])
        alpha, p = softmax_update(s_scr[other])
        p_scr[other] = p
        al_scr[other] = jnp.broadcast_to(alpha, (8, tq))
        ja = jnp.minimum(t, qi - 1)
        start = pl.multiple_of(ja * tq, tq)
        off = jnp.where(t < qi, 0.0, -NEG_BIG)
        s_scr[slot] = scores(k_ref[pl.ds(start, tq), :], ck_ref[0, 0, pl.ds(start, tq), :] + off)

    def body(i, carry):
        tick(2 * i, 0)
        tick(2 * i + 1, 1)
        return carry

    lax.fori_loop(0, jnp.where(qi > 0, (qi + 3) // 2, 0), body, 0)
    acc = acc_scr[...]
    o_ref[...] = (acc[:dh] / acc[dh:dh + 1]).T.astype(BF16)


def _fox(q, k, va, ck, km, vam, ckm, bsz, seq, heads, tq):
    t, aw = q.shape
    nq = seq // tq
    dh = ATT_HEAD_DIM
    da = va.shape[3]
    return pl.pallas_call(
        functools.partial(_fox_kernel, tq=tq, dh=dh),
        grid=(bsz, heads, nq),
        in_specs=[
            pl.BlockSpec((tq, dh), lambda b, h, i: (b * nq + i, h)),
            pl.BlockSpec((seq, dh), lambda b, h, i: (b, h)),
            pl.BlockSpec((1, 1, nq, da, tq), lambda b, h, i: (b, h, 0, 0, 0)),
            pl.BlockSpec((1, 1, seq, LANES), lambda b, h, i: (b, h, 0, 0)),
            pl.BlockSpec((LANES, dh), lambda b, h, i: (0, h)),
            pl.BlockSpec((1, da, LANES), lambda b, h, i: (h, 0, 0)),
            pl.BlockSpec((1, LANES, LANES), lambda b, h, i: (h, 0, 0)),
        ],
        out_specs=pl.BlockSpec((tq, dh), lambda b, h, i: (b * nq + i, h)),
        out_shape=jax.ShapeDtypeStruct((t, aw), BF16),
        scratch_shapes=[
            pltpu.VMEM((8, tq), F32),
            pltpu.VMEM((da, tq), F32),
            pltpu.VMEM((2, tq, tq), F32),
            pltpu.VMEM((2, tq, tq), BF16),
            pltpu.VMEM((2, 8, tq), F32),
        ],
        compiler_params=_params(("parallel", "parallel", "arbitrary"), 48),
        name="fox_attention",
    )(q, k, va, ck, km, vam, ckm)


def _mix_out_kernel(ys_ref, ya_ref, h_ref, wglu_ref, bglu_ref, wout_ref, g_ref, b_ref, wr_ref,
                    h2_ref, h2p_ref, lt_ref, *, ssm_w):
    ys = ys_ref[...]
    z = _dot(ys.astype(BF16), wglu_ref[...]) + bglu_ref[...]
    yg = (ys * jax.nn.sigmoid(z)).astype(BF16)
    mix = _dot(yg, wout_ref[0:ssm_w, :]) + _dot(ya_ref[...], wout_ref[ssm_w:, :])
    h2 = _layer_norm(DEEPNORM_ALPHA * h_ref[...] + mix, g_ref[...], b_ref[...])
    h2_ref[...] = h2
    _pack_rows(h2, h2p_ref)
    lt_ref[...] = _dot_nt(wr_ref[...], h2.astype(BF16))


def _mix_out(ys, ya, h, wglu, bglu, wout, g, b, wr_t, tm):
    t, d = h.shape
    ssm_w = ys.shape[1]
    att_w = ya.shape[1]
    e = wr_t.shape[0]
    row = lambda i: (i, 0)
    return pl.pallas_call(
        functools.partial(_mix_out_kernel, ssm_w=ssm_w),
        grid=(t // tm,),
        in_specs=[
            pl.BlockSpec((tm, ssm_w), row),
            pl.BlockSpec((tm, att_w), row),
            pl.BlockSpec((tm, d), row),
            _const_spec((ssm_w, ssm_w)), _const_spec((1, ssm_w)),
            _const_spec((d, d)), _const_spec((1, d)), _const_spec((1, d)),
            _const_spec((e, d)),
        ],
        out_specs=[
            pl.BlockSpec((tm, d), row),
            pl.BlockSpec((tm * ROW_TILE, LANES), row),
            pl.BlockSpec((e, tm), lambda i: (0, i)),
        ],
        out_shape=[
            jax.ShapeDtypeStruct((t, d), F32),
            jax.ShapeDtypeStruct((t * ROW_TILE, LANES), jnp.uint32),
            jax.ShapeDtypeStruct((e, t), F32),
        ],
        compiler_params=_params(("parallel",), 48),
        name="mix_out",
    )(ys, ya, h, wglu, bglu, wout, g, b, wr_t)


def _route_kernel(lt_ref, bias_ref, idx_ref, gate_ref, rank_ref, cnt_ref, run_scr, *, tn):
    e = N_EXPERTS
    ng = N_EXPERT_GROUPS
    gs = e // ng

    @pl.when(pl.program_id(0) == 0)
    def _():
        run_scr[...] = jnp.zeros_like(run_scr)

    scores = jax.nn.sigmoid(lt_ref[...])
    sel = scores + bias_ref[...]
    sel3 = sel.reshape(ng, gs, tn)
    mem = lax.broadcasted_iota(jnp.int32, (ng, gs, tn), 1)
    m1 = jnp.max(sel3, axis=1, keepdims=True)
    i1 = jnp.min(jnp.where(sel3 == m1, mem, gs), axis=1, keepdims=True)
    m2 = jnp.max(jnp.where(mem == i1, -jnp.inf, sel3), axis=1, keepdims=True)
    gscore = (m1 + m2).reshape(ng, tn)

    gio = lax.broadcasted_iota(jnp.int32, (ng, tn), 0)
    gmask = jnp.zeros((ng, tn), F32)
    for _ in range(TOPK_GROUPS):
        mx = jnp.max(gscore, axis=0, keepdims=True)
        gi = jnp.min(jnp.where(gscore == mx, gio, ng), axis=0, keepdims=True)
        hit = gio == gi
        gmask = jnp.where(hit, 1.0, gmask)
        gscore = jnp.where(hit, -jnp.inf, gscore)

    cand = jnp.where(gmask.reshape(ng, 1, tn) > 0.5, sel3, -jnp.inf).reshape(e, tn)
    eio = lax.broadcasted_iota(jnp.int32, (e, tn), 0)
    hits = []
    chosen = jnp.zeros((e, tn), F32)
    for _ in range(TOP_K):
        mx = jnp.max(cand, axis=0, keepdims=True)
        ei = jnp.min(jnp.where(cand == mx, eio, e), axis=0, keepdims=True)
        hit = eio == ei
        hits.append((ei, hit))
        chosen = jnp.where(hit, 1.0, chosen)
        cand = jnp.where(hit, -jnp.inf, cand)

    onehot = chosen.astype(BF16)
    r = lax.broadcasted_iota(jnp.int32, (tn, tn), 0)
    c = lax.broadcasted_iota(jnp.int32, (tn, tn), 1)
    tri = (r < c).astype(BF16)
    rank = _dot(onehot, tri) + run_scr[:, 0:1]
    run_scr[...] = run_scr[...] + jnp.sum(chosen, axis=1, keepdims=True)
    cnt_ref[...] = run_scr[...]

    gates = [jnp.sum(jnp.where(hit, scores, 0.0), axis=0, keepdims=True) for _, hit in hits]
    total = gates[0]
    for gk in gates[1:]:
        total = total + gk
    for kk, (ei, hit) in enumerate(hits):
        idx_ref[kk:kk + 1, :] = ei
        gate_ref[kk:kk + 1, :] = gates[kk] / total * ROUTED_SCALE
        rank_ref[kk:kk + 1, :] = jnp.sum(jnp.where(hit, rank, 0.0), axis=0, keepdims=True).astype(jnp.int32)


def _route(logits_t, bias, tn):
    e, t = logits_t.shape
    col = lambda i: (0, i)
    return pl.pallas_call(
        functools.partial(_route_kernel, tn=tn),
        grid=(t // tn,),
        in_specs=[pl.BlockSpec((e, tn), col), _const_spec((e, 1))],
        out_specs=[
            pl.BlockSpec((TOP_K, tn), col),
            pl.BlockSpec((TOP_K, tn), col),
            pl.BlockSpec((TOP_K, tn), col),
            pl.BlockSpec((e, LANES), lambda i: (0, 0)),
        ],
        out_shape=[
            jax.ShapeDtypeStruct((TOP_K, t), jnp.int32),
            jax.ShapeDtypeStruct((TOP_K, t), F32),
            jax.ShapeDtypeStruct((TOP_K, t), jnp.int32),
            jax.ShapeDtypeStruct((e, LANES), F32),
        ],
        scratch_shapes=[pltpu.VMEM((e, LANES), F32)],
        compiler_params=_params(("arbitrary",), 32),
        name="route",
    )(logits_t, bias)


def _tile_rows(ref, row):
    return ref.at[pl.ds(pl.multiple_of(row * ROW_TILE, ROW_TILE), ROW_TILE), :]


def _dispatch_kernel(ps_ref, pe_ref, tail_ref, pos_ref, x_ref, xs_ref, zero_scr, sem, *, tm, bm, n_blocks):
    @pl.when(pl.program_id(0) == 0)
    def _():
        zero_scr[...] = jnp.zeros_like(zero_scr)

        def fill(start):
            def per_expert(e, carry):
                def per_slot(slot, c):
                    cp = pltpu.make_async_copy(zero_scr.at[0:ROW_TILE, :], _tile_rows(xs_ref, slot), sem)
                    cp.start() if start else cp.wait()
                    return c
                return lax.fori_loop(ps_ref[e], pe_ref[e], per_slot, carry)
            lax.fori_loop(0, N_EXPERTS, per_expert, 0)

            def per_block(b, c):
                rows = pl.ds(pl.multiple_of(b * bm * ROW_TILE, bm * ROW_TILE), bm * ROW_TILE)
                cp = pltpu.make_async_copy(zero_scr, xs_ref.at[rows, :], sem)
                cp.start() if start else cp.wait()
                return c
            lax.fori_loop(tail_ref[0], n_blocks, per_block, 0)

        fill(True)
        fill(False)

    def issue(t, carry):
        src = _tile_rows(x_ref, t)
        for kk in range(TOP_K):
            pltpu.make_async_copy(src, _tile_rows(xs_ref, pos_ref[kk, t]), sem).start()
        return carry

    lax.fori_loop(0, tm, issue, 0)
    for _ in range(TOP_K):
        pltpu.make_async_copy(x_ref, xs_ref.at[pl.ds(0, tm * ROW_TILE), :], sem).wait()


def _dispatch(pad_start, pad_end, tail_block, pos, h2p, n_blocks, bm, tm):
    t = pos.shape[1]
    grid_spec = pltpu.PrefetchScalarGridSpec(
        num_scalar_prefetch=3,
        grid=(t // tm,),
        in_specs=[
            pl.BlockSpec((TOP_K, tm), lambda i, ps, pe, tl: (0, i), memory_space=pltpu.SMEM),
            pl.BlockSpec((tm * ROW_TILE, LANES), lambda i, ps, pe, tl: (i, 0)),
        ],
        out_specs=pl.BlockSpec(memory_space=pl.ANY),
        scratch_shapes=[pltpu.VMEM((bm * ROW_TILE, LANES), jnp.uint32), pltpu.SemaphoreType.DMA(())],
    )
    return pl.pallas_call(
        functools.partial(_dispatch_kernel, tm=tm, bm=bm, n_blocks=n_blocks),
        grid_spec=grid_spec,
        out_shape=jax.ShapeDtypeStruct((n_blocks * bm * ROW_TILE, LANES), jnp.uint32),
        compiler_params=_params(("arbitrary",), 32),
        name="dispatch",
    )(pad_start, pad_end, tail_block, pos, h2p)


def _moe_kernel(be_ref, bv_ref, bf_ref, x_ref, wg_ref, wu_ref, wd_ref, y_ref, wg_s, wu_s, wd_s, *, bm):
    b = pl.program_id(0)

    @pl.when(bf_ref[b] == 1)
    def _():
        wg_s[...] = wg_ref[0].astype(BF16)
        wu_s[...] = wu_ref[0].astype(BF16)
        wd_s[...] = wd_ref[0].astype(BF16)

    @pl.when(bv_ref[b] == 1)
    def _():
        lo, hi = _unpack_rows(x_ref, 0, bm)
        x = jnp.concatenate([p.astype(BF16) for p in lo + hi], axis=1)
        hid = jax.nn.silu(_dot(x, wg_s[...])) * _dot(x, wu_s[...])
        _pack_rows(_dot(hid.astype(BF16), wd_s[...]), y_ref)

    @pl.when(bv_ref[b] == 0)
    def _():
        y_ref[...] = jnp.zeros_like(y_ref)


def _moe_grouped(block_e, block_valid, block_first, xs, wg, wu, wd, bm):
    rows = xs.shape[0] // ROW_TILE
    d, ff = wg.shape[1], wg.shape[2]
    n_blocks = rows // bm
    grid_spec = pltpu.PrefetchScalarGridSpec(
        num_scalar_prefetch=3,
        grid=(n_blocks,),
        in_specs=[
            pl.BlockSpec((bm * ROW_TILE, LANES), lambda b, be, bv, bf: (b, 0)),
            pl.BlockSpec((1, d, ff), lambda b, be, bv, bf: (be[b], 0, 0)),
            pl.BlockSpec((1, d, ff), lambda b, be, bv, bf: (be[b], 0, 0)),
            pl.BlockSpec((1, ff, d), lambda b, be, bv, bf: (be[b], 0, 0)),
        ],
        out_specs=pl.BlockSpec((bm * ROW_TILE, LANES), lambda b, be, bv, bf: (b, 0)),
        scratch_shapes=[
            pltpu.VMEM((d, ff), BF16),
            pltpu.VMEM((d, ff), BF16),
            pltpu.VMEM((ff, d), BF16),
        ],
    )
    return pl.pallas_call(
        functools.partial(_moe_kernel, bm=bm),
        grid_spec=grid_spec,
        out_shape=jax.ShapeDtypeStruct(xs.shape, jnp.uint32),
        compiler_params=_params(("arbitrary",), 48),
        name="moe_grouped",
    )(block_e, block_valid, block_first, xs, wg, wu, wd)


def _final_kernel(pos_ref, h2_ref, gate_ref, wgu_ref, wd_ref, g_ref, b_ref, ys_ref, o_ref, ybuf, sem, *, tm, ff):
    def issue(t, carry):
        for kk in range(TOP_K):
            pltpu.make_async_copy(_tile_rows(ys_ref, pos_ref[kk, t]), _tile_rows(ybuf, kk * tm + t), sem).start()
        return carry

    lax.fori_loop(0, tm, issue, 0)
    h2 = h2_ref[...]
    gu = _dot(h2.astype(BF16), wgu_ref[...])
    hid = jax.nn.silu(gu[:, :ff]) * gu[:, ff:]
    shared = _dot(hid.astype(BF16), wd_ref[...])
    for kk in range(TOP_K):
        pltpu.make_async_copy(ys_ref.at[pl.ds(0, tm * ROW_TILE), :],
                              ybuf.at[pl.ds(kk * tm * ROW_TILE, tm * ROW_TILE), :], sem).wait()
    gate = gate_ref[...]
    acc = None
    for kk in range(TOP_K):
        lo, hi = _unpack_rows(ybuf, kk * tm * ROW_TILE, tm)
        gk = gate[:, kk:kk + 1]
        terms = [gk * p for p in lo + hi]
        acc = terms if acc is None else [a + b for a, b in zip(acc, terms)]
    ffn = jnp.concatenate(acc, axis=1) + shared
    o_ref[...] = _layer_norm(DEEPNORM_ALPHA * h2 + ffn, g_ref[...], b_ref[...])


def _final(pos, h2, gate, ys, wgu, wd, g, b, tm):
    t, d = h2.shape
    ff = wd.shape[0]
    row = lambda i: (i, 0)
    return pl.pallas_call(
        functools.partial(_final_kernel, tm=tm, ff=ff),
        grid=(t // tm,),
        in_specs=[
            pl.BlockSpec((TOP_K, tm), lambda i: (0, i), memory_space=pltpu.SMEM),
            pl.BlockSpec((tm, d), row),
            pl.BlockSpec((tm, TOP_K), row),
            _const_spec((d, 2 * ff)), _const_spec((ff, d)), _const_spec((1, d)), _const_spec((1, d)),
            pl.BlockSpec(memory_space=pl.ANY),
        ],
        out_specs=pl.BlockSpec((tm, d), row),
        out_shape=jax.ShapeDtypeStruct((t, d), F32),
        scratch_shapes=[pltpu.VMEM((TOP_K * tm * ROW_TILE, LANES), jnp.uint32), pltpu.SemaphoreType.DMA(())],
        compiler_params=_params(("arbitrary",), 48),
        name="final",
    )(pos, h2, gate, wgu, wd, g, b, ys)


def kernel(x, meta_tokens, ln_in_g, ln_in_b, w_in, b_forget, ssm_a_re, ssm_a_im, ssm_log_dt, ssm_b_re, ssm_b_im, ssm_c_re, ssm_c_im, ssm_d, w_glu, b_glu, w_out, ln_mix_g, ln_mix_b, w_router, router_bias, w_gate_exp, w_up_exp, w_down_exp, w_gate_sh, w_up_sh, w_down_sh, ln_ffn_g, ln_ffn_b):
    bsz, seq, d = x.shape
    t = bsz * seq
    ssm_w = w_glu.shape[1]
    heads = b_forget.shape[1]
    att_w = (w_in.shape[2] - ssm_w - heads) // 3
    groups = ssm_a_re.shape[1]
    assert meta_tokens.shape[0] == N_META == S5_CHUNK and att_w == heads * ATT_HEAD_DIM
    nc = seq // S5_CHUNK
    levels = int(math.log2(nc))
    assert 2 ** levels == nc and nc % 8 == 0
    row2 = lambda a: a.reshape(1, -1).astype(F32)

    n_uqk = ssm_w + 2 * att_w
    w_f = jnp.pad(w_in[0][:, n_uqk + att_w:], ((0, 0), (0, LANES - heads)))
    w_in_bf = jnp.concatenate([w_in[0][:, :n_uqk], w_f], axis=1).astype(BF16)
    wvt_bf = w_in[0][:, n_uqk:n_uqk + att_w].T.astype(BF16)
    bf_pad = jnp.pad(b_forget[0].astype(F32), (0, LANES - heads)).reshape(1, LANES)
    g_in, b_in = row2(ln_in_g), row2(ln_in_b)
    tm = min(256, t)
    h, u, q, k, vt, lf = _ln_inproj(x.reshape(t, d), g_in, b_in, w_in_bf, wvt_bf, bf_pad, tm, ssm_w, att_w)
    _, u_m, _, k_m, vt_m, lf_m = _ln_inproj(meta_tokens.astype(F32), g_in, b_in, w_in_bf, wvt_bf, bf_pad, N_META,
                                            ssm_w, att_w)

    mj, bpj, cpj, a1, a2 = _s5_tables(ssm_a_re[0], ssm_a_im[0], ssm_log_dt[0], ssm_b_re[0], ssm_b_im[0],
                                      ssm_c_re[0], ssm_c_im[0], ssm_d[0], levels)
    y_ssm = _s5(u, u_m, mj, bpj, cpj, a1, a2, bsz, nc, levels)

    tq = min(512, seq)
    nq = seq // tq
    dh = ATT_HEAD_DIM
    ones_rows = 16
    lfm = lf_m[:, :heads] * LOG2E
    c_meta = jnp.cumsum(lfm, axis=0) - jnp.sum(lfm, axis=0, keepdims=True)
    ckm = jnp.full((heads, LANES), -NEG_BIG, F32).at[:, :N_META].set(c_meta.T)
    ckm = jnp.broadcast_to(ckm[:, :, None], (heads, LANES, LANES))
    c_main = jnp.cumsum(lf[:, :heads].reshape(bsz, seq, heads) * LOG2E, axis=1)
    ck = jnp.broadcast_to(c_main.transpose(0, 2, 1)[..., None], (bsz, heads, seq, LANES))
    km = jnp.pad(k_m, ((0, LANES - N_META), (0, 0)))
    va = vt.reshape(heads, dh, bsz, nq, tq).transpose(2, 0, 3, 1, 4)
    va = jnp.concatenate([va, jnp.ones((bsz, heads, nq, ones_rows, tq), BF16)], axis=3)
    vam = jnp.pad(vt_m.reshape(heads, dh, N_META), ((0, 0), (0, 0), (0, LANES - N_META)))
    vam = jnp.concatenate([vam, jnp.ones((heads, ones_rows, LANES), BF16)], axis=1)
    y_att = _fox(q, k, va, ck, km, vam, ckm, bsz, seq, heads, tq)

    h2, h2p, logits_t = _mix_out(
        y_ssm, y_att, h, w_glu[0].astype(BF16), row2(b_glu[0]), w_out[0].astype(BF16),
        row2(ln_mix_g[0]), row2(ln_mix_b[0]), w_router[0].T.astype(BF16), tm)

    tn = min(512, t)
    idx_t, gate_t, rank_t, counts = _route(logits_t, router_bias[0].astype(F32).reshape(N_EXPERTS, 1), tn)

    bm = 256
    counts = counts[:, 0].astype(jnp.int32)
    pcounts = (counts + bm - 1) // bm * bm
    pends = jnp.cumsum(pcounts)
    pstarts = pends - pcounts
    pos = (pstarts[idx_t] + rank_t).astype(jnp.int32)
    n_blocks = t * TOP_K // bm + N_EXPERTS
    bstart = jnp.arange(n_blocks, dtype=jnp.int32) * bm
    block_e = jnp.minimum(jnp.sum(pends[None, :] <= bstart[:, None], axis=1), N_EXPERTS - 1).astype(jnp.int32)
    block_valid = (bstart < pends[-1]).astype(jnp.int32)
    block_first = jnp.concatenate([jnp.ones((1,), jnp.int32),
                                   (block_e[1:] != block_e[:-1]).astype(jnp.int32)])
    xs = _dispatch((pstarts + counts).astype(jnp.int32), pends.astype(jnp.int32),
                   (pends[-1:] // bm).astype(jnp.int32), pos, h2p, n_blocks, bm, tm)
    ys = _moe_grouped(block_e, block_valid, block_first, xs, w_gate_exp[0], w_up_exp[0], w_down_exp[0], bm)

    wgu_sh = jnp.concatenate([w_gate_sh[0], w_up_sh[0]], axis=1).astype(BF16)
    out = _final(pos, h2, gate_t.T, ys, wgu_sh, w_down_sh[0].astype(BF16), row2(ln_ffn_g[0]), row2(ln_ffn_b[0]), tm)
    return out.reshape(bsz, seq, d)
```

```python
import functools
import math

import jax
import jax.numpy as jnp
from jax import lax
from jax.experimental import pallas as pl
from jax.experimental.pallas import tpu as pltpu

N_META = 16
SSM_GROUP_CH = 16
SSM_STATE = 64
ATT_HEAD_DIM = 128
N_EXPERTS = 64
TOP_K = 8
N_EXPERT_GROUPS = 8
TOPK_GROUPS = 4
ROUTED_SCALE = 2.5
LN_EPS = 1e-5
DEPTH = 1
DEEPNORM_ALPHA = (2 * DEPTH) ** 0.25

S5_CHUNK = 16
LANES = 128
ROW_TILE = 8
NEG_BIG = -1e30
LOG2E = 1.4426950408889634

F32 = jnp.float32
BF16 = jnp.bfloat16
HIGHEST = lax.Precision.HIGHEST


def _dot(a, b):
    return jnp.dot(a, b, preferred_element_type=F32)


def _dot_nt(a, b):
    return lax.dot_general(a, b, (((1,), (1,)), ((), ())), preferred_element_type=F32)


def _layer_norm(x, g, b):
    mu = jnp.mean(x, axis=-1, keepdims=True)
    xc = x - mu
    var = jnp.mean(xc * xc, axis=-1, keepdims=True)
    return xc * lax.rsqrt(var + LN_EPS) * g + b


def _pack_rows(x, o_ref):
    m, w = x.shape
    half = w // 2
    assert half == ROW_TILE * LANES
    bits = lax.bitcast_convert_type(x.astype(BF16).astype(F32), jnp.uint32)
    packed = bits[:, half:] | (bits[:, :half] >> 16)
    for s in range(ROW_TILE):
        o_ref[pl.ds(s, m, stride=ROW_TILE), :] = packed[:, s * LANES:(s + 1) * LANES]


def _unpack_rows(x_ref, base, m):
    lo, hi = [], []
    for s in range(ROW_TILE):
        w = x_ref[pl.ds(base + s, m, stride=ROW_TILE), :]
        lo.append(lax.bitcast_convert_type(w << 16, F32))
        hi.append(lax.bitcast_convert_type(w & jnp.uint32(0xFFFF0000), F32))
    return lo, hi


def _params(sem, vmem_mb):
    return pltpu.CompilerParams(dimension_semantics=sem, vmem_limit_bytes=vmem_mb * 1024 * 1024)


def _const_spec(shape):
    nd = len(shape)
    return pl.BlockSpec(shape, lambda *_: (0,) * nd, pipeline_mode=pl.Buffered(1))


def _ln_inproj_kernel(x_ref, g_ref, b_ref, w_ref, wvt_ref, bf_ref, h_ref, u_ref, q_ref, k_ref, vt_ref, lf_ref,
                      *, ssm_w, att_w):
    h = _layer_norm(x_ref[...], g_ref[...], b_ref[...])
    h_ref[...] = h
    hb = h.astype(BF16)
    o = 0
    u_ref[...] = _dot(hb, w_ref[:, o:o + ssm_w])
    o += ssm_w
    q_ref[...] = (_dot(hb, w_ref[:, o:o + att_w]) * (LOG2E * ATT_HEAD_DIM ** -0.5)).astype(BF16)
    o += att_w
    k_ref[...] = _dot(hb, w_ref[:, o:o + att_w]).astype(BF16)
    o += att_w
    vt_ref[...] = _dot_nt(wvt_ref[...], hb).astype(BF16)
    f = _dot(hb, w_ref[:, o:o + LANES]) + bf_ref[...]
    lf_ref[...] = jnp.minimum(f, 0.0) - jnp.log(1.0 + jnp.exp(-jnp.abs(f)))


def _ln_inproj(x2d, g, b, w_bf, wvt_bf, bf_pad, tm, ssm_w, att_w):
    t, d = x2d.shape
    wcols = w_bf.shape[1]
    row = lambda i: (i, 0)
    return pl.pallas_call(
        functools.partial(_ln_inproj_kernel, ssm_w=ssm_w, att_w=att_w),
        grid=(t // tm,),
        in_specs=[
            pl.BlockSpec((tm, d), row),
            _const_spec((1, d)), _const_spec((1, d)),
            _const_spec((d, wcols)), _const_spec((att_w, d)), _const_spec((1, LANES)),
        ],
        out_specs=[
            pl.BlockSpec((tm, d), row),
            pl.BlockSpec((tm, ssm_w), row),
            pl.BlockSpec((tm, att_w), row),
            pl.BlockSpec((tm, att_w), row),
            pl.BlockSpec((att_w, tm), lambda i: (0, i)),
            pl.BlockSpec((tm, LANES), row),
        ],
        out_shape=[
            jax.ShapeDtypeStruct((t, d), F32),
            jax.ShapeDtypeStruct((t, ssm_w), F32),
            jax.ShapeDtypeStruct((t, att_w), BF16),
            jax.ShapeDtypeStruct((t, att_w), BF16),
            jax.ShapeDtypeStruct((att_w, t), BF16),
            jax.ShapeDtypeStruct((t, LANES), F32),
        ],
        compiler_params=_params(("parallel",), 56),
        name="ln_inproj",
    )(x2d, g, b, w_bf, wvt_bf, bf_pad)


def _s5_tables(a_re, a_im, log_dt, b_re, b_im, c_re, c_im, d_skip, levels):
    c = S5_CHUNK
    g, p = a_re.shape
    hc = SSM_GROUP_CH
    lam = lax.complex(a_re.astype(F32), a_im.astype(F32))
    dt = jnp.exp(log_dt.astype(F32))[:, None]
    ldt = lam * dt
    abar = jnp.exp(ldt)
    bbar = ((abar - 1.0) / lam)[..., None] * lax.complex(b_re.astype(F32), b_im.astype(F32))
    cc = lax.complex(c_re.astype(F32), c_im.astype(F32))
    tau = jnp.arange(c + 1, dtype=F32)
    apow = jnp.exp(ldt[:, None, :] * tau[None, :, None])

    ca = cc[:, None, :, :] * apow[:, :c, None, :]
    taps = (jnp.einsum('gthp,gpk->gthk', ca.real, bbar.real, precision=HIGHEST)
            - jnp.einsum('gthp,gpk->gthk', ca.imag, bbar.imag, precision=HIGHEST))
    tt = jnp.arange(c)[:, None]
    ss = jnp.arange(c)[None, :]
    lag_is = ((tt - ss)[None, :, :] == jnp.arange(c)[:, None, None]).astype(F32)
    m = jnp.einsum('gvhk,vts->gskth', taps, lag_is, precision=HIGHEST)
    eye_t = jnp.eye(c, dtype=F32)[None, :, None, :, None]
    eye_h = jnp.eye(hc, dtype=F32)[None, None, :, None, :]
    m = m + eye_t * eye_h * d_skip.astype(F32)[:, None, None, None, :]
    m_t = m.reshape(g, c * hc, c * hc)

    bp = apow[:, c - 1 - jnp.arange(c), :][:, :, :, None] * bbar[:, None, :, :]
    bp = bp.transpose(0, 1, 3, 2).reshape(g, c * hc, p)
    bp_t = jnp.concatenate([bp.real, bp.imag], axis=-1)

    cp = cc[:, None, :, :] * apow[:, 1:c + 1, None, :]
    cp = cp.reshape(g, c * hc, p).transpose(0, 2, 1)
    cp_t = jnp.concatenate([cp.real, -cp.imag], axis=1)

    steps = (c * (2.0 ** jnp.arange(levels, dtype=F32)))
    alev = jnp.exp(ldt[:, None, :] * steps[None, :, None])

    gl = LANES // hc
    nj = g // gl

    def spread(compact, n_outer, inner):
        rows = compact.shape[2]
        cols = n_outer * inner
        sel = (jnp.arange(cols)[:, None] // inner == jnp.arange(n_outer * gl * inner)[None, :] // (gl * inner))
        sel = sel & (jnp.arange(cols)[:, None] % inner == jnp.arange(n_outer * gl * inner)[None, :] % inner)
        wide = jnp.einsum('jgrc,cw->jgrw', compact.astype(BF16), sel.astype(BF16),
                          preferred_element_type=F32).astype(BF16)
        own = (jnp.arange(n_outer * gl * inner)[None, :] // inner) % gl == jnp.arange(gl)[:, None]
        return jnp.where(own[None, :, None, :], wide, jnp.zeros((), BF16)), rows

    wide, _ = spread(m_t.reshape(nj, gl, c * hc, c * hc), c, hc)
    mj = wide.reshape(nj, gl, c, hc, c * LANES).transpose(0, 2, 1, 3, 4).reshape(nj, c * LANES, c * LANES)
    wide, _ = spread(bp_t.reshape(nj, gl, c * hc, 2 * p), 2, p)
    bpj = wide.reshape(nj, gl, c, hc, 2 * gl * p).transpose(0, 2, 1, 3, 4).reshape(nj, c * LANES, 2 * gl * p)
    wide, _ = spread(cp_t.reshape(nj, gl, 2 * p, c * hc), c, hc)
    cpj = wide.reshape(nj, gl, 2, p, c * LANES).transpose(0, 2, 1, 3, 4).reshape(nj, 2 * gl * p, c * LANES)
    al = alev.reshape(nj, gl, levels, p).transpose(0, 2, 1, 3).reshape(nj, levels, gl * p)
    pad = ((0, 0), (0, 16 - levels), (0, 0))
    a1 = jnp.pad(jnp.concatenate([al.real, al.real], axis=-1), pad)
    a2 = jnp.pad(jnp.concatenate([-al.imag, al.imag], axis=-1), pad)
    return mj, bpj, cpj, a1, a2


def _shift_rows(x, sh):
    n = x.shape[0]
    if sh % 8 == 0:
        return jnp.concatenate([jnp.zeros((sh, x.shape[1]), x.dtype), x[:n - sh]], axis=0)
    rolled = pltpu.roll(x, sh, axis=0)
    rows = lax.broadcasted_iota(jnp.int32, x.shape, 0)
    return jnp.where(rows < sh, 0.0, rolled)


def _cmul(a1, a2, x):
    return a1 * x + a2 * pltpu.roll(x, x.shape[1] // 2, axis=1)


def _s5_kernel(u_ref, um_ref, mj_ref, bpj_ref, cpj_ref, a1_ref, a2_ref, y_ref, *, nc, levels):
    c = S5_CHUNK
    u = jnp.concatenate([u_ref[pl.ds(s, nc, stride=c), :].astype(BF16) for s in range(c)], axis=1)
    bpj = bpj_ref[0]
    w = _dot(u, bpj)
    um = jnp.concatenate([um_ref[s:s + 1, :] for s in range(c)], axis=1)
    x_meta = _dot(jnp.broadcast_to(um, (8, um.shape[1])).astype(BF16), bpj)
    first = lax.broadcasted_iota(jnp.int32, x_meta.shape, 0) == 0
    inject = jnp.where(first, _cmul(a1_ref[0, 0:1, :], a2_ref[0, 0:1, :], x_meta), 0.0)
    x = jnp.concatenate([w[:8] + inject, w[8:]], axis=0)
    for lvl in range(levels):
        x = x + _cmul(a1_ref[0, lvl:lvl + 1, :], a2_ref[0, lvl:lvl + 1, :], _shift_rows(x, 2 ** lvl))
    xp = _shift_rows(x, 1)
    x_in = jnp.concatenate([xp[:8] + jnp.where(first, x_meta, 0.0), xp[8:]], axis=0).astype(BF16)
    y = jax.nn.gelu(_dot(u, mj_ref[0]) + _dot(x_in, cpj_ref[0]))
    for s in range(c):
        y_ref[pl.ds(s, nc, stride=c), :] = y[:, s * LANES:(s + 1) * LANES]


def _s5(u, u_m, mj, bpj, cpj, a1, a2, bsz, nc, levels):
    t, ssm_w = u.shape
    nj, cl, sw = bpj.shape
    nl = a1.shape[1]
    rows = nc * S5_CHUNK
    tab = lambda j, b: (j, 0, 0)
    one = pl.Buffered(1)
    return pl.pallas_call(
        functools.partial(_s5_kernel, nc=nc, levels=levels),
        grid=(nj, bsz),
        in_specs=[
            pl.BlockSpec((rows, LANES), lambda j, b: (b, j)),
            pl.BlockSpec((S5_CHUNK, LANES), lambda j, b: (0, j)),
            pl.BlockSpec((1, cl, cl), tab, pipeline_mode=one),
            pl.BlockSpec((1, cl, sw), tab, pipeline_mode=one),
            pl.BlockSpec((1, sw, cl), tab, pipeline_mode=one),
            pl.BlockSpec((1, nl, sw), tab),
            pl.BlockSpec((1, nl, sw), tab),
        ],
        out_specs=pl.BlockSpec((rows, LANES), lambda j, b: (b, j)),
        out_shape=jax.ShapeDtypeStruct((t, ssm_w), F32),
        compiler_params=_params(("parallel", "parallel"), 56),
        name="s5_scan",
    )(u, u_m, mj, bpj, cpj, a1, a2)


def _fox_kernel(q_ref, k_ref, va_ref, ck_ref, km_ref, vam_ref, ckm_ref, o_ref,
                m_scr, acc_scr, s_scr, p_scr, al_scr, *, tq, dh):
    qi = pl.program_id(2)
    q = q_ref[...]
    reps = tq // LANES

    def scores(kb, ckb):
        return _dot_nt(kb, q) - jnp.concatenate([ckb] * reps, axis=1)

    def softmax_update(s):
        m_prev = m_scr[0:1, :]
        m_new = jnp.maximum(m_prev, jnp.max(s, axis=0, keepdims=True))
        m_scr[...] = jnp.broadcast_to(m_new, m_scr.shape)
        return jnp.exp2(m_prev - m_new), jnp.exp2(s - m_new).astype(BF16)

    s = scores(km_ref[...], ckm_ref[0])
    m0 = jnp.max(s, axis=0, keepdims=True)
    m_scr[...] = jnp.broadcast_to(m0, m_scr.shape)
    acc_scr[...] = _dot(vam_ref[0], jnp.exp2(s - m0).astype(BF16))

    start = pl.multiple_of(qi * tq, tq)
    s = scores(k_ref[pl.ds(start, tq), :], ck_ref[0, 0, pl.ds(start, tq), :])
    key = lax.broadcasted_iota(jnp.int32, (tq, tq), 0)
    qry = lax.broadcasted_iota(jnp.int32, (tq, tq), 1)
    s_scr[1] = jnp.where(key <= qry, s, NEG_BIG)
    p_scr[0] = jnp.zeros((tq, tq), BF16)
    al_scr[0] = jnp.ones((8, tq), F32)

    def tick(t, slot):
        other = 1 - slot
        jc = jnp.where(t == 1, qi, jnp.maximum(t - 2, 0))
        acc_scr[...] = al_scr[slot, 0:1, :] * acc_scr[...] + _dot(va_ref[0, 0, jc], p_scr[slot])
        alpha, p = softmax_update(s_scr[other])
        p_scr[other] = p
        al_scr[other] = jnp.broadcast_to(alpha, (8, tq))
        ja = jnp.minimum(t, jnp.maximum(qi - 1, 0))
        start = pl.multiple_of(ja * tq, tq)
        off = jnp.where(t < qi, 0.0, -NEG_BIG)
        s_scr[slot] = scores(k_ref[pl.ds(start, tq), :], ck_ref[0, 0, pl.ds(start, tq), :] + off)

    def body(i, carry):
        tick(2 * i, 0)
        tick(2 * i + 1, 1)
        return carry

    lax.fori_loop(0, (qi + 3) // 2, body, 0)
    acc = acc_scr[...]
    o_ref[...] = (acc[:dh] / acc[dh:dh + 1]).T.astype(BF16)


def _fox(q, k, va, ck, km, vam, ckm, bsz, seq, heads, tq):
    t, aw = q.shape
    nq = seq // tq
    dh = ATT_HEAD_DIM
    da = va.shape[3]
    return pl.pallas_call(
        functools.partial(_fox_kernel, tq=tq, dh=dh),
        grid=(bsz, heads, nq),
        in_specs=[
            pl.BlockSpec((tq, dh), lambda b, h, i: (b * nq + i, h)),
            pl.BlockSpec((seq, dh), lambda b, h, i: (b, h)),
            pl.BlockSpec((1, 1, nq, da, tq), lambda b, h, i: (b, h, 0, 0, 0)),
            pl.BlockSpec((1, 1, seq, LANES), lambda b, h, i: (b, h, 0, 0)),
            pl.BlockSpec((LANES, dh), lambda b, h, i: (0, h)),
            pl.BlockSpec((1, da, LANES), lambda b, h, i: (h, 0, 0)),
            pl.BlockSpec((1, LANES, LANES), lambda b, h, i: (h, 0, 0)),
        ],
        out_specs=pl.BlockSpec((tq, dh), lambda b, h, i: (b * nq + i, h)),
        out_shape=jax.ShapeDtypeStruct((t, aw), BF16),
        scratch_shapes=[
            pltpu.VMEM((8, tq), F32),
            pltpu.VMEM((da, tq), F32),
            pltpu.VMEM((2, tq, tq), F32),
            pltpu.VMEM((2, tq, tq), BF16),
            pltpu.VMEM((2, 8, tq), F32),
        ],
        compiler_params=_params(("parallel", "parallel", "arbitrary"), 48),
        name="fox_attention",
    )(q, k, va, ck, km, vam, ckm)


def _mix_out_kernel(ys_ref, ya_ref, h_ref, wglu_ref, bglu_ref, wout_ref, g_ref, b_ref, wr_ref,
                    h2_ref, h2p_ref, lt_ref, *, ssm_w):
    ys = ys_ref[...]
    z = _dot(ys.astype(BF16), wglu_ref[...]) + bglu_ref[...]
    yg = (ys * jax.nn.sigmoid(z)).astype(BF16)
    mix = _dot(yg, wout_ref[0:ssm_w, :]) + _dot(ya_ref[...], wout_ref[ssm_w:, :])
    h2 = _layer_norm(DEEPNORM_ALPHA * h_ref[...] + mix, g_ref[...], b_ref[...])
    h2_ref[...] = h2
    _pack_rows(h2, h2p_ref)
    lt_ref[...] = _dot_nt(wr_ref[...], h2.astype(BF16))


def _mix_out(ys, ya, h, wglu, bglu, wout, g, b, wr_t, tm):
    t, d = h.shape
    ssm_w = ys.shape[1]
    att_w = ya.shape[1]
    e = wr_t.shape[0]
    row = lambda i: (i, 0)
    return pl.pallas_call(
        functools.partial(_mix_out_kernel, ssm_w=ssm_w),
        grid=(t // tm,),
        in_specs=[
            pl.BlockSpec((tm, ssm_w), row),
            pl.BlockSpec((tm, att_w), row),
            pl.BlockSpec((tm, d), row),
            _const_spec((ssm_w, ssm_w)), _const_spec((1, ssm_w)),
            _const_spec((d, d)), _const_spec((1, d)), _const_spec((1, d)),
            _const_spec((e, d)),
        ],
        out_specs=[
            pl.BlockSpec((tm, d), row),
            pl.BlockSpec((tm * ROW_TILE, LANES), row),
            pl.BlockSpec((e, tm), lambda i: (0, i)),
        ],
        out_shape=[
            jax.ShapeDtypeStruct((t, d), F32),
            jax.ShapeDtypeStruct((t * ROW_TILE, LANES), jnp.uint32),
            jax.ShapeDtypeStruct((e, t), F32),
        ],
        compiler_params=_params(("parallel",), 48),
        name="mix_out",
    )(ys, ya, h, wglu, bglu, wout, g, b, wr_t)


def _route_kernel(lt_ref, bias_ref, idx_ref, gate_ref, rank_ref, cnt_ref, run_scr, *, tn):
    e = N_EXPERTS
    ng = N_EXPERT_GROUPS
    gs = e // ng

    @pl.when(pl.program_id(0) == 0)
    def _():
        run_scr[...] = jnp.zeros_like(run_scr)

    scores = jax.nn.sigmoid(lt_ref[...])
    sel = scores + bias_ref[...]
    sel3 = sel.reshape(ng, gs, tn)
    mem = lax.broadcasted_iota(jnp.int32, (ng, gs, tn), 1)
    m1 = jnp.max(sel3, axis=1, keepdims=True)
    i1 = jnp.min(jnp.where(sel3 == m1, mem, gs), axis=1, keepdims=True)
    m2 = jnp.max(jnp.where(mem == i1, -jnp.inf, sel3), axis=1, keepdims=True)
    gscore = (m1 + m2).reshape(ng, tn)

    gio = lax.broadcasted_iota(jnp.int32, (ng, tn), 0)
    gmask = jnp.zeros((ng, tn), F32)
    for _ in range(TOPK_GROUPS):
        mx = jnp.max(gscore, axis=0, keepdims=True)
        gi = jnp.min(jnp.where(gscore == mx, gio, ng), axis=0, keepdims=True)
        hit = gio == gi
        gmask = jnp.where(hit, 1.0, gmask)
        gscore = jnp.where(hit, -jnp.inf, gscore)

    cand = jnp.where(gmask.reshape(ng, 1, tn) > 0.5, sel3, -jnp.inf).reshape(e, tn)
    eio = lax.broadcasted_iota(jnp.int32, (e, tn), 0)
    hits = []
    chosen = jnp.zeros((e, tn), F32)
    for _ in range(TOP_K):
        mx = jnp.max(cand, axis=0, keepdims=True)
        ei = jnp.min(jnp.where(cand == mx, eio, e), axis=0, keepdims=True)
        hit = eio == ei
        hits.append((ei, hit))
        chosen = jnp.where(hit, 1.0, chosen)
        cand = jnp.where(hit, -jnp.inf, cand)

    onehot = chosen.astype(BF16)
    r = lax.broadcasted_iota(jnp.int32, (tn, tn), 0)
    c = lax.broadcasted_iota(jnp.int32, (tn, tn), 1)
    tri = (r < c).astype(BF16)
    rank = _dot(onehot, tri) + run_scr[:, 0:1]
    run_scr[...] = run_scr[...] + jnp.sum(chosen, axis=1, keepdims=True)
    cnt_ref[...] = run_scr[...]

    gates = [jnp.sum(jnp.where(hit, scores, 0.0), axis=0, keepdims=True) for _, hit in hits]
    total = gates[0]
    for gk in gates[1:]:
        total = total + gk
    for kk, (ei, hit) in enumerate(hits):
        idx_ref[kk:kk + 1, :] = ei
        gate_ref[kk:kk + 1, :] = gates[kk] / total * ROUTED_SCALE
        rank_ref[kk:kk + 1, :] = jnp.sum(jnp.where(hit, rank, 0.0), axis=0, keepdims=True).astype(jnp.int32)


def _route(logits_t, bias, tn):
    e, t = logits_t.shape
    col = lambda i: (0, i)
    return pl.pallas_call(
        functools.partial(_route_kernel, tn=tn),
        grid=(t // tn,),
        in_specs=[pl.BlockSpec((e, tn), col), _const_spec((e, 1))],
        out_specs=[
            pl.BlockSpec((TOP_K, tn), col),
            pl.BlockSpec((TOP_K, tn), col),
            pl.BlockSpec((TOP_K, tn), col),
            pl.BlockSpec((e, LANES), lambda i: (0, 0)),
        ],
        out_shape=[
            jax.ShapeDtypeStruct((TOP_K, t), jnp.int32),
            jax.ShapeDtypeStruct((TOP_K, t), F32),
            jax.ShapeDtypeStruct((TOP_K, t), jnp.int32),
            jax.ShapeDtypeStruct((e, LANES), F32),
        ],
        scratch_shapes=[pltpu.VMEM((e, LANES), F32)],
        compiler_params=_params(("arbitrary",), 32),
        name="route",
    )(logits_t, bias)


def _tile_rows(ref, row):
    return ref.at[pl.ds(pl.multiple_of(row * ROW_TILE, ROW_TILE), ROW_TILE), :]


def _dispatch_kernel(ps_ref, pe_ref, tail_ref, pos_ref, x_ref, xs_ref, zero_scr, sem, *, tm, bm, n_blocks):
    @pl.when(pl.program_id(0) == 0)
    def _():
        zero_scr[...] = jnp.zeros_like(zero_scr)

        def fill(start):
            def per_expert(e, carry):
                def per_slot(slot, c):
                    cp = pltpu.make_async_copy(zero_scr.at[0:ROW_TILE, :], _tile_rows(xs_ref, slot), sem)
                    cp.start() if start else cp.wait()
                    return c
                return lax.fori_loop(ps_ref[e], pe_ref[e], per_slot, carry)
            lax.fori_loop(0, N_EXPERTS, per_expert, 0)

            def per_block(b, c):
                rows = pl.ds(pl.multiple_of(b * bm * ROW_TILE, bm * ROW_TILE), bm * ROW_TILE)
                cp = pltpu.make_async_copy(zero_scr, xs_ref.at[rows, :], sem)
                cp.start() if start else cp.wait()
                return c
            lax.fori_loop(tail_ref[0], n_blocks, per_block, 0)

        fill(True)
        fill(False)

    def issue(t, carry):
        src = _tile_rows(x_ref, t)
        for kk in range(TOP_K):
            pltpu.make_async_copy(src, _tile_rows(xs_ref, pos_ref[kk, t]), sem).start()
        return carry

    lax.fori_loop(0, tm, issue, 0)
    for _ in range(TOP_K):
        pltpu.make_async_copy(x_ref, xs_ref.at[pl.ds(0, tm * ROW_TILE), :], sem).wait()


def _dispatch(pad_start, pad_end, tail_block, pos, h2p, n_blocks, bm, tm):
    t = pos.shape[1]
    grid_spec = pltpu.PrefetchScalarGridSpec(
        num_scalar_prefetch=3,
        grid=(t // tm,),
        in_specs=[
            pl.BlockSpec((TOP_K, tm), lambda i, ps, pe, tl: (0, i), memory_space=pltpu.SMEM),
            pl.BlockSpec((tm * ROW_TILE, LANES), lambda i, ps, pe, tl: (i, 0)),
        ],
        out_specs=pl.BlockSpec(memory_space=pl.ANY),
        scratch_shapes=[pltpu.VMEM((bm * ROW_TILE, LANES), jnp.uint32), pltpu.SemaphoreType.DMA(())],
    )
    return pl.pallas_call(
        functools.partial(_dispatch_kernel, tm=tm, bm=bm, n_blocks=n_blocks),
        grid_spec=grid_spec,
        out_shape=jax.ShapeDtypeStruct((n_blocks * bm * ROW_TILE, LANES), jnp.uint32),
        compiler_params=_params(("arbitrary",), 32),
        name="dispatch",
    )(pad_start, pad_end, tail_block, pos, h2p)


def _moe_kernel(be_ref, bv_ref, bf_ref, x_ref, wg_ref, wu_ref, wd_ref, y_ref, wg_s, wu_s, wd_s, *, bm):
    b = pl.program_id(0)

    @pl.when(bf_ref[b] == 1)
    def _():
        wg_s[...] = wg_ref[0].astype(BF16)
        wu_s[...] = wu_ref[0].astype(BF16)
        wd_s[...] = wd_ref[0].astype(BF16)

    @pl.when(bv_ref[b] == 1)
    def _():
        lo, hi = _unpack_rows(x_ref, 0, bm)
        x = jnp.concatenate([p.astype(BF16) for p in lo + hi], axis=1)
        hid = jax.nn.silu(_dot(x, wg_s[...])) * _dot(x, wu_s[...])
        _pack_rows(_dot(hid.astype(BF16), wd_s[...]), y_ref)

    @pl.when(bv_ref[b] == 0)
    def _():
        y_ref[...] = jnp.zeros_like(y_ref)


def _moe_grouped(block_e, block_valid, block_first, xs, wg, wu, wd, bm):
    rows = xs.shape[0] // ROW_TILE
    d, ff = wg.shape[1], wg.shape[2]
    n_blocks = rows // bm
    grid_spec = pltpu.PrefetchScalarGridSpec(
        num_scalar_prefetch=3,
        grid=(n_blocks,),
        in_specs=[
            pl.BlockSpec((bm * ROW_TILE, LANES), lambda b, be, bv, bf: (b, 0)),
            pl.BlockSpec((1, d, ff), lambda b, be, bv, bf: (be[b], 0, 0)),
            pl.BlockSpec((1, d, ff), lambda b, be, bv, bf: (be[b], 0, 0)),
            pl.BlockSpec((1, ff, d), lambda b, be, bv, bf: (be[b], 0, 0)),
        ],
        out_specs=pl.BlockSpec((bm * ROW_TILE, LANES), lambda b, be, bv, bf: (b, 0)),
        scratch_shapes=[
            pltpu.VMEM((d, ff), BF16),
            pltpu.VMEM((d, ff), BF16),
            pltpu.VMEM((ff, d), BF16),
        ],
    )
    return pl.pallas_call(
        functools.partial(_moe_kernel, bm=bm),
        grid_spec=grid_spec,
        out_shape=jax.ShapeDtypeStruct(xs.shape, jnp.uint32),
        compiler_params=_params(("arbitrary",), 48),
        name="moe_grouped",
    )(block_e, block_valid, block_first, xs, wg, wu, wd)


def _final_kernel(pos_ref, h2_ref, gate_ref, wgu_ref, wd_ref, g_ref, b_ref, ys_ref, o_ref, ybuf, sem, *, tm, ff):
    def issue(t, carry):
        for kk in range(TOP_K):
            pltpu.make_async_copy(_tile_rows(ys_ref, pos_ref[kk, t]), _tile_rows(ybuf, kk * tm + t), sem).start()
        return carry

    lax.fori_loop(0, tm, issue, 0)
    h2 = h2_ref[...]
    gu = _dot(h2.astype(BF16), wgu_ref[...])
    hid = jax.nn.silu(gu[:, :ff]) * gu[:, ff:]
    shared = _dot(hid.astype(BF16), wd_ref[...])
    for kk in range(TOP_K):
        pltpu.make_async_copy(ys_ref.at[pl.ds(0, tm * ROW_TILE), :],
                              ybuf.at[pl.ds(kk * tm * ROW_TILE, tm * ROW_TILE), :], sem).wait()
    gate = gate_ref[...]
    acc = None
    for kk in range(TOP_K):
        lo, hi = _unpack_rows(ybuf, kk * tm * ROW_TILE, tm)
        gk = gate[:, kk:kk + 1]
        terms = [gk * p for p in lo + hi]
        acc = terms if acc is None else [a + b for a, b in zip(acc, terms)]
    ffn = jnp.concatenate(acc, axis=1) + shared
    o_ref[...] = _layer_norm(DEEPNORM_ALPHA * h2 + ffn, g_ref[...], b_ref[...])


def _final(pos, h2, gate, ys, wgu, wd, g, b, tm):
    t, d = h2.shape
    ff = wd.shape[0]
    row = lambda i: (i, 0)
    return pl.pallas_call(
        functools.partial(_final_kernel, tm=tm, ff=ff),
        grid=(t // tm,),
        in_specs=[
            pl.BlockSpec((TOP_K, tm), lambda i: (0, i), memory_space=pltpu.SMEM),
            pl.BlockSpec((tm, d), row),
            pl.BlockSpec((tm, TOP_K), row),
            _const_spec((d, 2 * ff)), _const_spec((ff, d)), _const_spec((1, d)), _const_spec((1, d)),
            pl.BlockSpec(memory_space=pl.ANY),
        ],
        out_specs=pl.BlockSpec((tm, d), row),
        out_shape=jax.ShapeDtypeStruct((t, d), F32),
        scratch_shapes=[pltpu.VMEM((TOP_K * tm * ROW_TILE, LANES), jnp.uint32), pltpu.SemaphoreType.DMA(())],
        compiler_params=_params(("arbitrary",), 48),
        name="final",
    )(pos, h2, gate, wgu, wd, g, b, ys)


def kernel(x, meta_tokens, ln_in_g, ln_in_b, w_in, b_forget, ssm_a_re, ssm_a_im, ssm_log_dt, ssm_b_re, ssm_b_im, ssm_c_re, ssm_c_im, ssm_d, w_glu, b_glu, w_out, ln_mix_g, ln_mix_b, w_router, router_bias, w_gate_exp, w_up_exp, w_down_exp, w_gate_sh, w_up_sh, w_down_sh, ln_ffn_g, ln_ffn_b):
    bsz, seq, d = x.shape
    t = bsz * seq
    ssm_w = w_glu.shape[1]
    heads = b_forget.shape[1]
    att_w = (w_in.shape[2] - ssm_w - heads) // 3
    groups = ssm_a_re.shape[1]
    assert meta_tokens.shape[0] == N_META == S5_CHUNK and att_w == heads * ATT_HEAD_DIM
    nc = seq // S5_CHUNK
    levels = int(math.log2(nc))
    assert 2 ** levels == nc and nc % 8 == 0
    row2 = lambda a: a.reshape(1, -1).astype(F32)

    n_uqk = ssm_w + 2 * att_w
    w_f = jnp.pad(w_in[0][:, n_uqk + att_w:], ((0, 0), (0, LANES - heads)))
    w_in_bf = jnp.concatenate([w_in[0][:, :n_uqk], w_f], axis=1).astype(BF16)
    wvt_bf = w_in[0][:, n_uqk:n_uqk + att_w].T.astype(BF16)
    bf_pad = jnp.pad(b_forget[0].astype(F32), (0, LANES - heads)).reshape(1, LANES)
    g_in, b_in = row2(ln_in_g), row2(ln_in_b)
    tm = min(256, t)
    h, u, q, k, vt, lf = _ln_inproj(x.reshape(t, d), g_in, b_in, w_in_bf, wvt_bf, bf_pad, tm, ssm_w, att_w)
    _, u_m, _, k_m, vt_m, lf_m = _ln_inproj(meta_tokens.astype(F32), g_in, b_in, w_in_bf, wvt_bf, bf_pad, N_META,
                                            ssm_w, att_w)

    mj, bpj, cpj, a1, a2 = _s5_tables(ssm_a_re[0], ssm_a_im[0], ssm_log_dt[0], ssm_b_re[0], ssm_b_im[0],
                                      ssm_c_re[0], ssm_c_im[0], ssm_d[0], levels)
    y_ssm = _s5(u, u_m, mj, bpj, cpj, a1, a2, bsz, nc, levels)

    tq = min(512, seq)
    nq = seq // tq
    dh = ATT_HEAD_DIM
    ones_rows = 16
    lfm = lf_m[:, :heads] * LOG2E
    c_meta = jnp.cumsum(lfm, axis=0) - jnp.sum(lfm, axis=0, keepdims=True)
    ckm = jnp.full((heads, LANES), -NEG_BIG, F32).at[:, :N_META].set(c_meta.T)
    ckm = jnp.broadcast_to(ckm[:, :, None], (heads, LANES, LANES))
    c_main = jnp.cumsum(lf[:, :heads].reshape(bsz, seq, heads) * LOG2E, axis=1)
    ck = jnp.broadcast_to(c_main.transpose(0, 2, 1)[..., None], (bsz, heads, seq, LANES))
    km = jnp.pad(k_m, ((0, LANES - N_META), (0, 0)))
    va = vt.reshape(heads, dh, bsz, nq, tq).transpose(2, 0, 3, 1, 4)
    va = jnp.concatenate([va, jnp.ones((bsz, heads, nq, ones_rows, tq), BF16)], axis=3)
    vam = jnp.pad(vt_m.reshape(heads, dh, N_META), ((0, 0), (0, 0), (0, LANES - N_META)))
    vam = jnp.concatenate([vam, jnp.ones((heads, ones_rows, LANES), BF16)], axis=1)
    y_att = _fox(q, k, va, ck, km, vam, ckm, bsz, seq, heads, tq)

    h2, h2p, logits_t = _mix_out(
        y_ssm, y_att, h, w_glu[0].astype(BF16), row2(b_glu[0]), w_out[0].astype(BF16),
        row2(ln_mix_g[0]), row2(ln_mix_b[0]), w_router[0].T.astype(BF16), tm)

    tn = min(512, t)
    idx_t, gate_t, rank_t, counts = _route(logits_t, router_bias[0].astype(F32).reshape(N_EXPERTS, 1), tn)

    bm = 256
    counts = counts[:, 0].astype(jnp.int32)
    pcounts = (counts + bm - 1) // bm * bm
    pends = jnp.cumsum(pcounts)
    pstarts = pends - pcounts
    pos = (pstarts[idx_t] + rank_t).astype(jnp.int32)
    n_blocks = t * TOP_K // bm + N_EXPERTS
    bstart = jnp.arange(n_blocks, dtype=jnp.int32) * bm
    block_e = jnp.minimum(jnp.sum(pends[None, :] <= bstart[:, None], axis=1), N_EXPERTS - 1).astype(jnp.int32)
    block_valid = (bstart < pends[-1]).astype(jnp.int32)
    block_first = jnp.concatenate([jnp.ones((1,), jnp.int32),
                                   (block_e[1:] != block_e[:-1]).astype(jnp.int32)])
    xs = _dispatch((pstarts + counts).astype(jnp.int32), pends.astype(jnp.int32),
                   (pends[-1:] // bm).astype(jnp.int32), pos, h2p, n_blocks, bm, tm)
    ys = _moe_grouped(block_e, block_valid, block_first, xs, w_gate_exp[0], w_up_exp[0], w_down_exp[0], bm)

    wgu_sh = jnp.concatenate([w_gate_sh[0], w_up_sh[0]], axis=1).astype(BF16)
    out = _final(pos, h2, gate_t.T, ys, wgu_sh, w_down_sh[0].astype(BF16), row2(ln_ffn_g[0]), row2(ln_ffn_b[0]), tm)
    return out.reshape(bsz, seq, d)
```

```python
import functools
import math

import jax
import jax.numpy as jnp
from jax import lax
from jax.experimental import pallas as pl
from jax.experimental.pallas import tpu as pltpu

N_META = 16
SSM_GROUP_CH = 16
SSM_STATE = 64
ATT_HEAD_DIM = 128
N_EXPERTS = 64
TOP_K = 8
N_EXPERT_GROUPS = 8
TOPK_GROUPS = 4
ROUTED_SCALE = 2.5
LN_EPS = 1e-5
DEPTH = 1
DEEPNORM_ALPHA = (2 * DEPTH) ** 0.25

S5_CHUNK = 16
FOX_SLOTS = 2
LANES = 128
ROW_TILE = 8
NEG_BIG = -1e30
LOG2E = 1.4426950408889634

F32 = jnp.float32
BF16 = jnp.bfloat16
HIGHEST = lax.Precision.HIGHEST


def _dot(a, b):
    return jnp.dot(a, b, preferred_element_type=F32)


def _dot_nt(a, b):
    return lax.dot_general(a, b, (((1,), (1,)), ((), ())), preferred_element_type=F32)


def _layer_norm(x, g, b):
    mu = jnp.mean(x, axis=-1, keepdims=True)
    xc = x - mu
    var = jnp.mean(xc * xc, axis=-1, keepdims=True)
    return xc * lax.rsqrt(var + LN_EPS) * g + b


def _pack_rows(x, o_ref):
    m, w = x.shape
    half = w // 2
    assert half == ROW_TILE * LANES
    bits = lax.bitcast_convert_type(x.astype(BF16).astype(F32), jnp.uint32)
    packed = bits[:, half:] | (bits[:, :half] >> 16)
    for s in range(ROW_TILE):
        o_ref[pl.ds(s, m, stride=ROW_TILE), :] = packed[:, s * LANES:(s + 1) * LANES]


def _unpack_rows(x_ref, base, m):
    lo, hi = [], []
    for s in range(ROW_TILE):
        w = x_ref[pl.ds(base + s, m, stride=ROW_TILE), :]
        lo.append(lax.bitcast_convert_type(w << 16, F32))
        hi.append(lax.bitcast_convert_type(w & jnp.uint32(0xFFFF0000), F32))
    return lo, hi


def _params(sem, vmem_mb):
    return pltpu.CompilerParams(dimension_semantics=sem, vmem_limit_bytes=vmem_mb * 1024 * 1024)


def _const_spec(shape):
    nd = len(shape)
    return pl.BlockSpec(shape, lambda *_: (0,) * nd, pipeline_mode=pl.Buffered(1))


def _ln_inproj_kernel(x_ref, g_ref, b_ref, w_ref, wvt_ref, bf_ref, h_ref, u_ref, q_ref, k_ref, vt_ref, lf_ref,
                      *, ssm_w, att_w):
    h = _layer_norm(x_ref[...], g_ref[...], b_ref[...])
    h_ref[...] = h
    hb = h.astype(BF16)
    o = 0
    u_ref[...] = _dot(hb, w_ref[:, o:o + ssm_w])
    o += ssm_w
    q_ref[...] = (_dot(hb, w_ref[:, o:o + att_w]) * (LOG2E * ATT_HEAD_DIM ** -0.5)).astype(BF16)
    o += att_w
    k_ref[...] = _dot(hb, w_ref[:, o:o + att_w]).astype(BF16)
    o += att_w
    vt_ref[...] = _dot_nt(wvt_ref[...], hb).astype(BF16)
    f = _dot(hb, w_ref[:, o:o + LANES]) + bf_ref[...]
    lf_ref[...] = jnp.minimum(f, 0.0) - jnp.log(1.0 + jnp.exp(-jnp.abs(f)))


def _ln_inproj(x2d, g, b, w_bf, wvt_bf, bf_pad, tm, ssm_w, att_w):
    t, d = x2d.shape
    wcols = w_bf.shape[1]
    row = lambda i: (i, 0)
    return pl.pallas_call(
        functools.partial(_ln_inproj_kernel, ssm_w=ssm_w, att_w=att_w),
        grid=(t // tm,),
        in_specs=[
            pl.BlockSpec((tm, d), row),
            _const_spec((1, d)), _const_spec((1, d)),
            _const_spec((d, wcols)), _const_spec((att_w, d)), _const_spec((1, LANES)),
        ],
        out_specs=[
            pl.BlockSpec((tm, d), row),
            pl.BlockSpec((tm, ssm_w), row),
            pl.BlockSpec((tm, att_w), row),
            pl.BlockSpec((tm, att_w), row),
            pl.BlockSpec((att_w, tm), lambda i: (0, i)),
            pl.BlockSpec((tm, LANES), row),
        ],
        out_shape=[
            jax.ShapeDtypeStruct((t, d), F32),
            jax.ShapeDtypeStruct((t, ssm_w), F32),
            jax.ShapeDtypeStruct((t, att_w), BF16),
            jax.ShapeDtypeStruct((t, att_w), BF16),
            jax.ShapeDtypeStruct((att_w, t), BF16),
            jax.ShapeDtypeStruct((t, LANES), F32),
        ],
        compiler_params=_params(("parallel",), 56),
        name="ln_inproj",
    )(x2d, g, b, w_bf, wvt_bf, bf_pad)


def _s5_tables(a_re, a_im, log_dt, b_re, b_im, c_re, c_im, d_skip, levels):
    c = S5_CHUNK
    g, p = a_re.shape
    hc = SSM_GROUP_CH
    lam = lax.complex(a_re.astype(F32), a_im.astype(F32))
    dt = jnp.exp(log_dt.astype(F32))[:, None]
    ldt = lam * dt
    abar = jnp.exp(ldt)
    bbar = ((abar - 1.0) / lam)[..., None] * lax.complex(b_re.astype(F32), b_im.astype(F32))
    cc = lax.complex(c_re.astype(F32), c_im.astype(F32))
    tau = jnp.arange(c + 1, dtype=F32)
    apow = jnp.exp(ldt[:, None, :] * tau[None, :, None])

    ca = cc[:, None, :, :] * apow[:, :c, None, :]
    taps = (jnp.einsum('gthp,gpk->gthk', ca.real, bbar.real, precision=HIGHEST)
            - jnp.einsum('gthp,gpk->gthk', ca.imag, bbar.imag, precision=HIGHEST))
    tt = jnp.arange(c)[:, None]
    ss = jnp.arange(c)[None, :]
    lag_is = ((tt - ss)[None, :, :] == jnp.arange(c)[:, None, None]).astype(F32)
    m = jnp.einsum('gvhk,vts->gskth', taps, lag_is, precision=HIGHEST)
    eye_t = jnp.eye(c, dtype=F32)[None, :, None, :, None]
    eye_h = jnp.eye(hc, dtype=F32)[None, None, :, None, :]
    m = m + eye_t * eye_h * d_skip.astype(F32)[:, None, None, None, :]
    m_t = m.reshape(g, c * hc, c * hc)

    bp = apow[:, c - 1 - jnp.arange(c), :][:, :, :, None] * bbar[:, None, :, :]
    bp = bp.transpose(0, 1, 3, 2).reshape(g, c * hc, p)
    bp_t = jnp.concatenate([bp.real, bp.imag], axis=-1)

    cp = cc[:, None, :, :] * apow[:, 1:c + 1, None, :]
    cp = cp.reshape(g, c * hc, p).transpose(0, 2, 1)
    cp_t = jnp.concatenate([cp.real, -cp.imag], axis=1)

    steps = (c * (2.0 ** jnp.arange(levels, dtype=F32)))
    alev = jnp.exp(ldt[:, None, :] * steps[None, :, None])

    gl = LANES // hc
    nj = g // gl

    def spread(compact, n_outer, inner):
        rows = compact.shape[2]
        cols = n_outer * inner
        sel = (jnp.arange(cols)[:, None] // inner == jnp.arange(n_outer * gl * inner)[None, :] // (gl * inner))
        sel = sel & (jnp.arange(cols)[:, None] % inner == jnp.arange(n_outer * gl * inner)[None, :] % inner)
        wide = jnp.einsum('jgrc,cw->jgrw', compact.astype(BF16), sel.astype(BF16),
                          preferred_element_type=F32).astype(BF16)
        own = (jnp.arange(n_outer * gl * inner)[None, :] // inner) % gl == jnp.arange(gl)[:, None]
        return jnp.where(own[None, :, None, :], wide, jnp.zeros((), BF16)), rows

    wide, _ = spread(m_t.reshape(nj, gl, c * hc, c * hc), c, hc)
    mj = wide.reshape(nj, gl, c, hc, c * LANES).transpose(0, 2, 1, 3, 4).reshape(nj, c * LANES, c * LANES)
    wide, _ = spread(bp_t.reshape(nj, gl, c * hc, 2 * p), 2, p)
    bpj = wide.reshape(nj, gl, c, hc, 2 * gl * p).transpose(0, 2, 1, 3, 4).reshape(nj, c * LANES, 2 * gl * p)
    wide, _ = spread(cp_t.reshape(nj, gl, 2 * p, c * hc), c, hc)
    cpj = wide.reshape(nj, gl, 2, p, c * LANES).transpose(0, 2, 1, 3, 4).reshape(nj, 2 * gl * p, c * LANES)
    al = alev.reshape(nj, gl, levels, p).transpose(0, 2, 1, 3).reshape(nj, levels, gl * p)
    pad = ((0, 0), (0, 16 - levels), (0, 0))
    a1 = jnp.pad(jnp.concatenate([al.real, al.real], axis=-1), pad)
    a2 = jnp.pad(jnp.concatenate([-al.imag, al.imag], axis=-1), pad)
    return mj, bpj, cpj, a1, a2


def _shift_rows(x, sh):
    n = x.shape[0]
    if sh % 8 == 0:
        return jnp.concatenate([jnp.zeros((sh, x.shape[1]), x.dtype), x[:n - sh]], axis=0)
    rolled = pltpu.roll(x, sh, axis=0)
    rows = lax.broadcasted_iota(jnp.int32, x.shape, 0)
    return jnp.where(rows < sh, 0.0, rolled)


def _cmul(a1, a2, x):
    return a1 * x + a2 * pltpu.roll(x, x.shape[1] // 2, axis=1)


def _s5_kernel(u_ref, um_ref, mj_ref, bpj_ref, cpj_ref, a1_ref, a2_ref, y_ref, *, nc, levels):
    c = S5_CHUNK
    u = jnp.concatenate([u_ref[pl.ds(s, nc, stride=c), :].astype(BF16) for s in range(c)], axis=1)
    bpj = bpj_ref[0]
    w = _dot(u, bpj)
    um = jnp.concatenate([um_ref[s:s + 1, :] for s in range(c)], axis=1)
    x_meta = _dot(jnp.broadcast_to(um, (8, um.shape[1])).astype(BF16), bpj)
    first = lax.broadcasted_iota(jnp.int32, x_meta.shape, 0) == 0
    inject = jnp.where(first, _cmul(a1_ref[0, 0:1, :], a2_ref[0, 0:1, :], x_meta), 0.0)
    x = jnp.concatenate([w[:8] + inject, w[8:]], axis=0)
    for lvl in range(levels):
        x = x + _cmul(a1_ref[0, lvl:lvl + 1, :], a2_ref[0, lvl:lvl + 1, :], _shift_rows(x, 2 ** lvl))
    xp = _shift_rows(x, 1)
    x_in = jnp.concatenate([xp[:8] + jnp.where(first, x_meta, 0.0), xp[8:]], axis=0).astype(BF16)
    y = jax.nn.gelu(_dot(u, mj_ref[0]) + _dot(x_in, cpj_ref[0]))
    for s in range(c):
        y_ref[pl.ds(s, nc, stride=c), :] = y[:, s * LANES:(s + 1) * LANES]


def _s5(u, u_m, mj, bpj, cpj, a1, a2, bsz, nc, levels):
    t, ssm_w = u.shape
    nj, cl, sw = bpj.shape
    nl = a1.shape[1]
    rows = nc * S5_CHUNK
    tab = lambda j, b: (j, 0, 0)
    one = pl.Buffered(1)
    return pl.pallas_call(
        functools.partial(_s5_kernel, nc=nc, levels=levels),
        grid=(nj, bsz),
        in_specs=[
            pl.BlockSpec((rows, LANES), lambda j, b: (b, j)),
            pl.BlockSpec((S5_CHUNK, LANES), lambda j, b: (0, j)),
            pl.BlockSpec((1, cl, cl), tab, pipeline_mode=one),
            pl.BlockSpec((1, cl, sw), tab, pipeline_mode=one),
            pl.BlockSpec((1, sw, cl), tab, pipeline_mode=one),
            pl.BlockSpec((1, nl, sw), tab),
            pl.BlockSpec((1, nl, sw), tab),
        ],
        out_specs=pl.BlockSpec((rows, LANES), lambda j, b: (b, j)),
        out_shape=jax.ShapeDtypeStruct((t, ssm_w), F32),
        compiler_params=_params(("parallel", "parallel"), 56),
        name="s5_scan",
    )(u, u_m, mj, bpj, cpj, a1, a2)


def _fox_kernel(q_ref, k_ref, va_ref, ck_ref, km_ref, vam_ref, ckm_ref, o_ref,
                m_scr, acc_scr, s_scr, p_scr, al_scr, *, tq, dh):
    qi = pl.program_id(2)
    q = q_ref[...]
    reps = tq // LANES

    def scores(kb, ckb):
        return _dot_nt(kb, q) - jnp.concatenate([ckb] * reps, axis=1)

    def softmax_update(s):
        m_prev = m_scr[0:1, :]
        m_new = jnp.maximum(m_prev, jnp.max(s, axis=0, keepdims=True))
        m_scr[...] = jnp.broadcast_to(m_new, m_scr.shape)
        return jnp.exp2(m_prev - m_new), jnp.exp2((s - m_new).astype(BF16))

    s = scores(km_ref[...], ckm_ref[0])
    m0 = jnp.max(s, axis=0, keepdims=True)
    m_scr[...] = jnp.broadcast_to(m0, m_scr.shape)
    acc_scr[...] = _dot(vam_ref[0], jnp.exp2(s - m0).astype(BF16))

    start = pl.multiple_of(qi * tq, tq)
    s = scores(k_ref[pl.ds(start, tq), :], ck_ref[0, 0, pl.ds(start, tq), :])
    key = lax.broadcasted_iota(jnp.int32, (tq, tq), 0)
    qry = lax.broadcasted_iota(jnp.int32, (tq, tq), 1)
    ns = FOX_SLOTS
    s_scr[ns - 1] = jnp.where(key <= qry, s, NEG_BIG)
    p_scr[ns - 2] = jnp.zeros((tq, tq), BF16)
    al_scr[ns - 2] = jnp.ones((8, tq), F32)

    def tick(t, u):
        sb, sc = (u - 1) % ns, (u - 2) % ns
        jc = jnp.where(t == 1, qi, jnp.clip(t - 2, 0, qi))
        acc_scr[...] = al_scr[sc, 0:1, :] * acc_scr[...] + _dot(va_ref[0, 0, jc], p_scr[sc])
        alpha, p = softmax_update(s_scr[sb])
        p_scr[sb] = p
        al_scr[sb] = jnp.broadcast_to(alpha, (8, tq))
        ja = jnp.minimum(t, jnp.maximum(qi - 1, 0))
        start = pl.multiple_of(ja * tq, tq)
        off = jnp.where(t < qi, 0.0, -NEG_BIG)
        s_scr[u] = scores(k_ref[pl.ds(start, tq), :], ck_ref[0, 0, pl.ds(start, tq), :] + off)

    def body(i, carry):
        for u in range(ns):
            tick(ns * i + u, u)
        return carry

    lax.fori_loop(0, (qi + 2 + ns - 1) // ns, body, 0)
    acc = acc_scr[...]
    o_ref[...] = (acc[:dh] / acc[dh:dh + 1]).T.astype(BF16)


def _fox(q, k, va, ck, km, vam, ckm, bsz, seq, heads, tq):
    t, aw = q.shape
    nq = seq // tq
    dh = ATT_HEAD_DIM
    da = va.shape[3]
    return pl.pallas_call(
        functools.partial(_fox_kernel, tq=tq, dh=dh),
        grid=(bsz, heads, nq),
        in_specs=[
            pl.BlockSpec((tq, dh), lambda b, h, i: (b * nq + i, h)),
            pl.BlockSpec((seq, dh), lambda b, h, i: (b, h)),
            pl.BlockSpec((1, 1, nq, da, tq), lambda b, h, i: (b, h, 0, 0, 0)),
            pl.BlockSpec((1, 1, seq, LANES), lambda b, h, i: (b, h, 0, 0)),
            pl.BlockSpec((LANES, dh), lambda b, h, i: (0, h)),
            pl.BlockSpec((1, da, LANES), lambda b, h, i: (h, 0, 0)),
            pl.BlockSpec((1, LANES, LANES), lambda b, h, i: (h, 0, 0)),
        ],
        out_specs=pl.BlockSpec((tq, dh), lambda b, h, i: (b * nq + i, h)),
        out_shape=jax.ShapeDtypeStruct((t, aw), BF16),
        scratch_shapes=[
            pltpu.VMEM((8, tq), F32),
            pltpu.VMEM((da, tq), F32),
            pltpu.VMEM((FOX_SLOTS, tq, tq), F32),
            pltpu.VMEM((FOX_SLOTS, tq, tq), BF16),
            pltpu.VMEM((FOX_SLOTS, 8, tq), F32),
        ],
        compiler_params=_params(("parallel", "parallel", "arbitrary"), 48),
        name="fox_attention",
    )(q, k, va, ck, km, vam, ckm)


def _mix_out_kernel(ys_ref, ya_ref, h_ref, wglu_ref, bglu_ref, wout_ref, g_ref, b_ref, wr_ref,
                    h2_ref, h2p_ref, lt_ref, *, ssm_w):
    ys = ys_ref[...]
    z = _dot(ys.astype(BF16), wglu_ref[...]) + bglu_ref[...]
    yg = (ys * jax.nn.sigmoid(z)).astype(BF16)
    mix = _dot(yg, wout_ref[0:ssm_w, :]) + _dot(ya_ref[...], wout_ref[ssm_w:, :])
    h2 = _layer_norm(DEEPNORM_ALPHA * h_ref[...] + mix, g_ref[...], b_ref[...])
    h2_ref[...] = h2
    _pack_rows(h2, h2p_ref)
    lt_ref[...] = _dot_nt(wr_ref[...], h2.astype(BF16))


def _mix_out(ys, ya, h, wglu, bglu, wout, g, b, wr_t, tm):
    t, d = h.shape
    ssm_w = ys.shape[1]
    att_w = ya.shape[1]
    e = wr_t.shape[0]
    row = lambda i: (i, 0)
    return pl.pallas_call(
        functools.partial(_mix_out_kernel, ssm_w=ssm_w),
        grid=(t // tm,),
        in_specs=[
            pl.BlockSpec((tm, ssm_w), row),
            pl.BlockSpec((tm, att_w), row),
            pl.BlockSpec((tm, d), row),
            _const_spec((ssm_w, ssm_w)), _const_spec((1, ssm_w)),
            _const_spec((d, d)), _const_spec((1, d)), _const_spec((1, d)),
            _const_spec((e, d)),
        ],
        out_specs=[
            pl.BlockSpec((tm, d), row),
            pl.BlockSpec((tm * ROW_TILE, LANES), row),
            pl.BlockSpec((e, tm), lambda i: (0, i)),
        ],
        out_shape=[
            jax.ShapeDtypeStruct((t, d), F32),
            jax.ShapeDtypeStruct((t * ROW_TILE, LANES), jnp.uint32),
            jax.ShapeDtypeStruct((e, t), F32),
        ],
        compiler_params=_params(("parallel",), 48),
        name="mix_out",
    )(ys, ya, h, wglu, bglu, wout, g, b, wr_t)


def _route_kernel(lt_ref, bias_ref, idx_ref, gate_ref, rank_ref, cnt_ref, run_scr, *, tn):
    e = N_EXPERTS
    ng = N_EXPERT_GROUPS
    gs = e // ng

    @pl.when(pl.program_id(0) == 0)
    def _():
        run_scr[...] = jnp.zeros_like(run_scr)

    scores = jax.nn.sigmoid(lt_ref[...])
    sel = scores + bias_ref[...]
    sel3 = sel.reshape(ng, gs, tn)
    mem = lax.broadcasted_iota(jnp.int32, (ng, gs, tn), 1)
    m1 = jnp.max(sel3, axis=1, keepdims=True)
    i1 = jnp.min(jnp.where(sel3 == m1, mem, gs), axis=1, keepdims=True)
    m2 = jnp.max(jnp.where(mem == i1, -jnp.inf, sel3), axis=1, keepdims=True)
    gscore = (m1 + m2).reshape(ng, tn)

    gio = lax.broadcasted_iota(jnp.int32, (ng, tn), 0)
    gmask = jnp.zeros((ng, tn), F32)
    for _ in range(TOPK_GROUPS):
        mx = jnp.max(gscore, axis=0, keepdims=True)
        gi = jnp.min(jnp.where(gscore == mx, gio, ng), axis=0, keepdims=True)
        hit = gio == gi
        gmask = jnp.where(hit, 1.0, gmask)
        gscore = jnp.where(hit, -jnp.inf, gscore)

    cand = jnp.where(gmask.reshape(ng, 1, tn) > 0.5, sel3, -jnp.inf).reshape(e, tn)
    eio = lax.broadcasted_iota(jnp.int32, (e, tn), 0)
    hits = []
    chosen = jnp.zeros((e, tn), F32)
    for _ in range(TOP_K):
        mx = jnp.max(cand, axis=0, keepdims=True)
        ei = jnp.min(jnp.where(cand == mx, eio, e), axis=0, keepdims=True)
        hit = eio == ei
        hits.append((ei, hit))
        chosen = jnp.where(hit, 1.0, chosen)
        cand = jnp.where(hit, -jnp.inf, cand)

    onehot = chosen.astype(BF16)
    r = lax.broadcasted_iota(jnp.int32, (tn, tn), 0)
    c = lax.broadcasted_iota(jnp.int32, (tn, tn), 1)
    tri = (r < c).astype(BF16)
    rank = _dot(onehot, tri) + run_scr[:, 0:1]
    run_scr[...] = run_scr[...] + jnp.sum(chosen, axis=1, keepdims=True)
    cnt_ref[...] = run_scr[...]

    gates = [jnp.sum(jnp.where(hit, scores, 0.0), axis=0, keepdims=True) for _, hit in hits]
    total = gates[0]
    for gk in gates[1:]:
        total = total + gk
    for kk, (ei, hit) in enumerate(hits):
        idx_ref[kk:kk + 1, :] = ei
        gate_ref[kk:kk + 1, :] = gates[kk] / total * ROUTED_SCALE
        rank_ref[kk:kk + 1, :] = jnp.sum(jnp.where(hit, rank, 0.0), axis=0, keepdims=True).astype(jnp.int32)


def _route(logits_t, bias, tn):
    e, t = logits_t.shape
    col = lambda i: (0, i)
    return pl.pallas_call(
        functools.partial(_route_kernel, tn=tn),
        grid=(t // tn,),
        in_specs=[pl.BlockSpec((e, tn), col), _const_spec((e, 1))],
        out_specs=[
            pl.BlockSpec((TOP_K, tn), col),
            pl.BlockSpec((TOP_K, tn), col),
            pl.BlockSpec((TOP_K, tn), col),
            pl.BlockSpec((e, LANES), lambda i: (0, 0)),
        ],
        out_shape=[
            jax.ShapeDtypeStruct((TOP_K, t), jnp.int32),
            jax.ShapeDtypeStruct((TOP_K, t), F32),
            jax.ShapeDtypeStruct((TOP_K, t), jnp.int32),
            jax.ShapeDtypeStruct((e, LANES), F32),
        ],
        scratch_shapes=[pltpu.VMEM((e, LANES), F32)],
        compiler_params=_params(("arbitrary",), 32),
        name="route",
    )(logits_t, bias)


def _tile_rows(ref, row):
    return ref.at[pl.ds(pl.multiple_of(row * ROW_TILE, ROW_TILE), ROW_TILE), :]


def _dispatch_kernel(ps_ref, pe_ref, tail_ref, pos_ref, x_ref, xs_ref, zero_scr, sem, *, tm, bm, n_blocks):
    @pl.when(pl.program_id(0) == 0)
    def _():
        zero_scr[...] = jnp.zeros_like(zero_scr)

        def fill(start):
            def per_expert(e, carry):
                def per_slot(slot, c):
                    cp = pltpu.make_async_copy(zero_scr.at[0:ROW_TILE, :], _tile_rows(xs_ref, slot), sem)
                    cp.start() if start else cp.wait()
                    return c
                return lax.fori_loop(ps_ref[e], pe_ref[e], per_slot, carry)
            lax.fori_loop(0, N_EXPERTS, per_expert, 0)

            def per_block(b, c):
                rows = pl.ds(pl.multiple_of(b * bm * ROW_TILE, bm * ROW_TILE), bm * ROW_TILE)
                cp = pltpu.make_async_copy(zero_scr, xs_ref.at[rows, :], sem)
                cp.start() if start else cp.wait()
                return c
            lax.fori_loop(tail_ref[0], n_blocks, per_block, 0)

        fill(True)
        fill(False)

    def issue(t, carry):
        src = _tile_rows(x_ref, t)
        for kk in range(TOP_K):
            pltpu.make_async_copy(src, _tile_rows(xs_ref, pos_ref[kk, t]), sem).start()
        return carry

    lax.fori_loop(0, tm, issue, 0)
    for _ in range(TOP_K):
        pltpu.make_async_copy(x_ref, xs_ref.at[pl.ds(0, tm * ROW_TILE), :], sem).wait()


def _dispatch(pad_start, pad_end, tail_block, pos, h2p, n_blocks, bm, tm):
    t = pos.shape[1]
    grid_spec = pltpu.PrefetchScalarGridSpec(
        num_scalar_prefetch=3,
        grid=(t // tm,),
        in_specs=[
            pl.BlockSpec((TOP_K, tm), lambda i, ps, pe, tl: (0, i), memory_space=pltpu.SMEM),
            pl.BlockSpec((tm * ROW_TILE, LANES), lambda i, ps, pe, tl: (i, 0)),
        ],
        out_specs=pl.BlockSpec(memory_space=pl.ANY),
        scratch_shapes=[pltpu.VMEM((bm * ROW_TILE, LANES), jnp.uint32), pltpu.SemaphoreType.DMA(())],
    )
    return pl.pallas_call(
        functools.partial(_dispatch_kernel, tm=tm, bm=bm, n_blocks=n_blocks),
        grid_spec=grid_spec,
        out_shape=jax.ShapeDtypeStruct((n_blocks * bm * ROW_TILE, LANES), jnp.uint32),
        compiler_params=_params(("arbitrary",), 32),
        name="dispatch",
    )(pad_start, pad_end, tail_block, pos, h2p)


def _moe_kernel(be_ref, bv_ref, bf_ref, x_ref, wg_ref, wu_ref, wd_ref, y_ref, wg_s, wu_s, wd_s, *, bm):
    b = pl.program_id(0)

    @pl.when(bf_ref[b] == 1)
    def _():
        wg_s[...] = wg_ref[0].astype(BF16)
        wu_s[...] = wu_ref[0].astype(BF16)
        wd_s[...] = wd_ref[0].astype(BF16)

    @pl.when(bv_ref[b] == 1)
    def _():
        lo, hi = _unpack_rows(x_ref, 0, bm)
        x = jnp.concatenate([p.astype(BF16) for p in lo + hi], axis=1)
        hid = jax.nn.silu(_dot(x, wg_s[...])) * _dot(x, wu_s[...])
        _pack_rows(_dot(hid.astype(BF16), wd_s[...]), y_ref)

    @pl.when(bv_ref[b] == 0)
    def _():
        y_ref[...] = jnp.zeros_like(y_ref)


def _moe_grouped(block_e, block_valid, block_first, xs, wg, wu, wd, bm):
    rows = xs.shape[0] // ROW_TILE
    d, ff = wg.shape[1], wg.shape[2]
    n_blocks = rows // bm
    grid_spec = pltpu.PrefetchScalarGridSpec(
        num_scalar_prefetch=3,
        grid=(n_blocks,),
        in_specs=[
            pl.BlockSpec((bm * ROW_TILE, LANES), lambda b, be, bv, bf: (b, 0)),
            pl.BlockSpec((1, d, ff), lambda b, be, bv, bf: (be[b], 0, 0)),
            pl.BlockSpec((1, d, ff), lambda b, be, bv, bf: (be[b], 0, 0)),
            pl.BlockSpec((1, ff, d), lambda b, be, bv, bf: (be[b], 0, 0)),
        ],
        out_specs=pl.BlockSpec((bm * ROW_TILE, LANES), lambda b, be, bv, bf: (b, 0)),
        scratch_shapes=[
            pltpu.VMEM((d, ff), BF16),
            pltpu.VMEM((d, ff), BF16),
            pltpu.VMEM((ff, d), BF16),
        ],
    )
    return pl.pallas_call(
        functools.partial(_moe_kernel, bm=bm),
        grid_spec=grid_spec,
        out_shape=jax.ShapeDtypeStruct(xs.shape, jnp.uint32),
        compiler_params=_params(("arbitrary",), 48),
        name="moe_grouped",
    )(block_e, block_valid, block_first, xs, wg, wu, wd)


def _final_kernel(pos_ref, posn_ref, h2_ref, gate_ref, wgu_ref, wd_ref, g_ref, b_ref, ys_ref, o_ref, ybuf, sem,
                  *, tm, ff):
    i = pl.program_id(0)
    slot = i % 2
    tile = TOP_K * tm * ROW_TILE

    def gather(p_ref, dst_slot):
        def issue(t, carry):
            for kk in range(TOP_K):
                dst = ybuf.at[pl.ds(pl.multiple_of(dst_slot * tile + (kk * tm + t) * ROW_TILE, ROW_TILE), ROW_TILE), :]
                pltpu.make_async_copy(_tile_rows(ys_ref, p_ref[kk, t]), dst, sem.at[dst_slot]).start()
            return carry
        lax.fori_loop(0, tm, issue, 0)

    @pl.when(i == 0)
    def _():
        gather(pos_ref, slot)

    @pl.when(i + 1 < pl.num_programs(0))
    def _():
        gather(posn_ref, 1 - slot)

    h2 = h2_ref[...]
    gu = _dot(h2.astype(BF16), wgu_ref[...])
    hid = jax.nn.silu(gu[:, :ff]) * gu[:, ff:]
    shared = _dot(hid.astype(BF16), wd_ref[...])
    base = pl.multiple_of(slot * tile, tile)
    for kk in range(TOP_K):
        pltpu.make_async_copy(ys_ref.at[pl.ds(0, tm * ROW_TILE), :],
                              ybuf.at[pl.ds(base + kk * tm * ROW_TILE, tm * ROW_TILE), :], sem.at[slot]).wait()
    gate = gate_ref[...]
    acc = None
    for kk in range(TOP_K):
        lo, hi = _unpack_rows(ybuf, base + kk * tm * ROW_TILE, tm)
        gk = gate[:, kk:kk + 1]
        terms = [gk * p for p in lo + hi]
        acc = terms if acc is None else [a + b for a, b in zip(acc, terms)]
    ffn = jnp.concatenate(acc, axis=1) + shared
    o_ref[...] = _layer_norm(DEEPNORM_ALPHA * h2 + ffn, g_ref[...], b_ref[...])


def _final(pos, h2, gate, ys, wgu, wd, g, b, tm):
    t, d = h2.shape
    ff = wd.shape[0]
    row = lambda i: (i, 0)
    return pl.pallas_call(
        functools.partial(_final_kernel, tm=tm, ff=ff),
        grid=(t // tm,),
        in_specs=[
            pl.BlockSpec((TOP_K, tm), lambda i: (0, i), memory_space=pltpu.SMEM),
            pl.BlockSpec((TOP_K, tm), lambda i: (0, jnp.minimum(i + 1, t // tm - 1)), memory_space=pltpu.SMEM),
            pl.BlockSpec((tm, d), row),
            pl.BlockSpec((tm, TOP_K), row),
            _const_spec((d, 2 * ff)), _const_spec((ff, d)), _const_spec((1, d)), _const_spec((1, d)),
            pl.BlockSpec(memory_space=pl.ANY),
        ],
        out_specs=pl.BlockSpec((tm, d), row),
        out_shape=jax.ShapeDtypeStruct((t, d), F32),
        scratch_shapes=[pltpu.VMEM((2 * TOP_K * tm * ROW_TILE, LANES), jnp.uint32), pltpu.SemaphoreType.DMA((2,))],
        compiler_params=_params(("arbitrary",), 56),
        name="final",
    )(pos, pos, h2, gate, wgu, wd, g, b, ys)


def kernel(x, meta_tokens, ln_in_g, ln_in_b, w_in, b_forget, ssm_a_re, ssm_a_im, ssm_log_dt, ssm_b_re, ssm_b_im, ssm_c_re, ssm_c_im, ssm_d, w_glu, b_glu, w_out, ln_mix_g, ln_mix_b, w_router, router_bias, w_gate_exp, w_up_exp, w_down_exp, w_gate_sh, w_up_sh, w_down_sh, ln_ffn_g, ln_ffn_b):
    bsz, seq, d = x.shape
    t = bsz * seq
    ssm_w = w_glu.shape[1]
    heads = b_forget.shape[1]
    att_w = (w_in.shape[2] - ssm_w - heads) // 3
    groups = ssm_a_re.shape[1]
    assert meta_tokens.shape[0] == N_META == S5_CHUNK and att_w == heads * ATT_HEAD_DIM
    nc = seq // S5_CHUNK
    levels = int(math.log2(nc))
    assert 2 ** levels == nc and nc % 8 == 0
    row2 = lambda a: a.reshape(1, -1).astype(F32)

    n_uqk = ssm_w + 2 * att_w
    w_f = jnp.pad(w_in[0][:, n_uqk + att_w:], ((0, 0), (0, LANES - heads)))
    w_in_bf = jnp.concatenate([w_in[0][:, :n_uqk], w_f], axis=1).astype(BF16)
    wvt_bf = w_in[0][:, n_uqk:n_uqk + att_w].T.astype(BF16)
    bf_pad = jnp.pad(b_forget[0].astype(F32), (0, LANES - heads)).reshape(1, LANES)
    g_in, b_in = row2(ln_in_g), row2(ln_in_b)
    tm = min(256, t)
    h, u, q, k, vt, lf = _ln_inproj(x.reshape(t, d), g_in, b_in, w_in_bf, wvt_bf, bf_pad, tm, ssm_w, att_w)
    _, u_m, _, k_m, vt_m, lf_m = _ln_inproj(meta_tokens.astype(F32), g_in, b_in, w_in_bf, wvt_bf, bf_pad, N_META,
                                            ssm_w, att_w)

    mj, bpj, cpj, a1, a2 = _s5_tables(ssm_a_re[0], ssm_a_im[0], ssm_log_dt[0], ssm_b_re[0], ssm_b_im[0],
                                      ssm_c_re[0], ssm_c_im[0], ssm_d[0], levels)
    y_ssm = _s5(u, u_m, mj, bpj, cpj, a1, a2, bsz, nc, levels)

    tq = min(512, seq)
    nq = seq // tq
    dh = ATT_HEAD_DIM
    ones_rows = 16
    lfm = lf_m[:, :heads] * LOG2E
    c_meta = jnp.cumsum(lfm, axis=0) - jnp.sum(lfm, axis=0, keepdims=True)
    ckm = jnp.full((heads, LANES), -NEG_BIG, F32).at[:, :N_META].set(c_meta.T)
    ckm = jnp.broadcast_to(ckm[:, :, None], (heads, LANES, LANES))
    c_main = jnp.cumsum(lf[:, :heads].reshape(bsz, seq, heads) * LOG2E, axis=1)
    ck = jnp.broadcast_to(c_main.transpose(0, 2, 1)[..., None], (bsz, heads, seq, LANES))
    km = jnp.pad(k_m, ((0, LANES - N_META), (0, 0)))
    va = vt.reshape(heads, dh, bsz, nq, tq).transpose(2, 0, 3, 1, 4)
    va = jnp.concatenate([va, jnp.ones((bsz, heads, nq, ones_rows, tq), BF16)], axis=3)
    vam = jnp.pad(vt_m.reshape(heads, dh, N_META), ((0, 0), (0, 0), (0, LANES - N_META)))
    vam = jnp.concatenate([vam, jnp.ones((heads, ones_rows, LANES), BF16)], axis=1)
    y_att = _fox(q, k, va, ck, km, vam, ckm, bsz, seq, heads, tq)

    h2, h2p, logits_t = _mix_out(
        y_ssm, y_att, h, w_glu[0].astype(BF16), row2(b_glu[0]), w_out[0].astype(BF16),
        row2(ln_mix_g[0]), row2(ln_mix_b[0]), w_router[0].T.astype(BF16), tm)

    tn = min(512, t)
    idx_t, gate_t, rank_t, counts = _route(logits_t, router_bias[0].astype(F32).reshape(N_EXPERTS, 1), tn)

    bm = 256
    counts = counts[:, 0].astype(jnp.int32)
    pcounts = (counts + bm - 1) // bm * bm
    pends = jnp.cumsum(pcounts)
    pstarts = pends - pcounts
    start_of = jnp.sum(jnp.where(idx_t[:, :, None] == jnp.arange(N_EXPERTS, dtype=jnp.int32), pstarts, 0), axis=-1)
    pos = (start_of + rank_t).astype(jnp.int32)
    n_blocks = t * TOP_K // bm + N_EXPERTS
    bstart = jnp.arange(n_blocks, dtype=jnp.int32) * bm
    block_e = jnp.minimum(jnp.sum(pends[None, :] <= bstart[:, None], axis=1), N_EXPERTS - 1).astype(jnp.int32)
    block_valid = (bstart < pends[-1]).astype(jnp.int32)
    block_first = jnp.concatenate([jnp.ones((1,), jnp.int32),
                                   (block_e[1:] != block_e[:-1]).astype(jnp.int32)])
    xs = _dispatch((pstarts + counts).astype(jnp.int32), pends.astype(jnp.int32),
                   (pends[-1:] // bm).astype(jnp.int32), pos, h2p, n_blocks, bm, tm)
    ys = _moe_grouped(block_e, block_valid, block_first, xs, w_gate_exp[0], w_up_exp[0], w_down_exp[0], bm)

    wgu_sh = jnp.concatenate([w_gate_sh[0], w_up_sh[0]], axis=1).astype(BF16)
    out = _final(pos, h2, gate_t.T, ys, wgu_sh, w_down_sh[0].astype(BF16), row2(ln_ffn_g[0]), row2(ln_ffn_b[0]), tm)
    return out.reshape(bsz, seq, d)
```

```python
import functools
import math

import jax
import jax.numpy as jnp
from jax import lax
from jax.experimental import pallas as pl
from jax.experimental.pallas import tpu as pltpu

N_META = 16
SSM_GROUP_CH = 16
SSM_STATE = 64
ATT_HEAD_DIM = 128
N_EXPERTS = 64
TOP_K = 8
N_EXPERT_GROUPS = 8
TOPK_GROUPS = 4
ROUTED_SCALE = 2.5
LN_EPS = 1e-5
DEPTH = 1
DEEPNORM_ALPHA = (2 * DEPTH) ** 0.25

S5_CHUNK = 16
FOX_SLOTS = 2
LANES = 128
ROW_TILE = 8
NEG_BIG = -1e30
LOG2E = 1.4426950408889634

F32 = jnp.float32
BF16 = jnp.bfloat16
HIGHEST = lax.Precision.HIGHEST


def _dot(a, b):
    return jnp.dot(a, b, preferred_element_type=F32)


def _dot_nt(a, b):
    return lax.dot_general(a, b, (((1,), (1,)), ((), ())), preferred_element_type=F32)


def _layer_norm(x, g, b):
    mu = jnp.mean(x, axis=-1, keepdims=True)
    xc = x - mu
    var = jnp.mean(xc * xc, axis=-1, keepdims=True)
    return xc * lax.rsqrt(var + LN_EPS) * g + b


def _pack_rows(x, o_ref):
    m, w = x.shape
    half = w // 2
    assert half == ROW_TILE * LANES
    bits = lax.bitcast_convert_type(x.astype(BF16).astype(F32), jnp.uint32)
    packed = bits[:, half:] | (bits[:, :half] >> 16)
    for s in range(ROW_TILE):
        o_ref[pl.ds(s, m, stride=ROW_TILE), :] = packed[:, s * LANES:(s + 1) * LANES]


def _unpack_rows(x_ref, base, m):
    lo, hi = [], []
    for s in range(ROW_TILE):
        w = x_ref[pl.ds(base + s, m, stride=ROW_TILE), :]
        lo.append(lax.bitcast_convert_type(w << 16, F32))
        hi.append(lax.bitcast_convert_type(w & jnp.uint32(0xFFFF0000), F32))
    return lo, hi


def _params(sem, vmem_mb):
    return pltpu.CompilerParams(dimension_semantics=sem, vmem_limit_bytes=vmem_mb * 1024 * 1024)


def _const_spec(shape):
    nd = len(shape)
    return pl.BlockSpec(shape, lambda *_: (0,) * nd, pipeline_mode=pl.Buffered(1))


def _ln_inproj_kernel(x_ref, g_ref, b_ref, w_ref, wqvt_ref, bf_ref, h_ref, u_ref, qt_ref, k_ref, vt_ref, lf_ref,
                      *, ssm_w, att_w):
    h = _layer_norm(x_ref[...], g_ref[...], b_ref[...])
    h_ref[...] = h
    hb = h.astype(BF16)
    o = 0
    u_ref[...] = _dot(hb, w_ref[:, o:o + ssm_w])
    o += ssm_w
    k_ref[...] = _dot(hb, w_ref[:, o:o + att_w]).astype(BF16)
    o += att_w
    qvt = _dot_nt(wqvt_ref[...], hb)
    qt_ref[...] = (qvt[:att_w] * (LOG2E * ATT_HEAD_DIM ** -0.5)).astype(BF16)
    vt_ref[...] = qvt[att_w:].astype(BF16)
    f = _dot(hb, w_ref[:, o:o + LANES]) + bf_ref[...]
    lf_ref[...] = jnp.minimum(f, 0.0) - jnp.log(1.0 + jnp.exp(-jnp.abs(f)))


def _ln_inproj(x2d, g, b, w_bf, wqvt_bf, bf_pad, tm, ssm_w, att_w):
    t, d = x2d.shape
    wcols = w_bf.shape[1]
    row = lambda i: (i, 0)
    col = lambda i: (0, i)
    return pl.pallas_call(
        functools.partial(_ln_inproj_kernel, ssm_w=ssm_w, att_w=att_w),
        grid=(t // tm,),
        in_specs=[
            pl.BlockSpec((tm, d), row),
            _const_spec((1, d)), _const_spec((1, d)),
            _const_spec((d, wcols)), _const_spec((2 * att_w, d)), _const_spec((1, LANES)),
        ],
        out_specs=[
            pl.BlockSpec((tm, d), row),
            pl.BlockSpec((tm, ssm_w), row),
            pl.BlockSpec((att_w, tm), col),
            pl.BlockSpec((tm, att_w), row),
            pl.BlockSpec((att_w, tm), col),
            pl.BlockSpec((tm, LANES), row),
        ],
        out_shape=[
            jax.ShapeDtypeStruct((t, d), F32),
            jax.ShapeDtypeStruct((t, ssm_w), F32),
            jax.ShapeDtypeStruct((att_w, t), BF16),
            jax.ShapeDtypeStruct((t, att_w), BF16),
            jax.ShapeDtypeStruct((att_w, t), BF16),
            jax.ShapeDtypeStruct((t, LANES), F32),
        ],
        compiler_params=_params(("parallel",), 56),
        name="ln_inproj",
    )(x2d, g, b, w_bf, wqvt_bf, bf_pad)


def _s5_tables(a_re, a_im, log_dt, b_re, b_im, c_re, c_im, d_skip, levels):
    c = S5_CHUNK
    g, p = a_re.shape
    hc = SSM_GROUP_CH
    lam = lax.complex(a_re.astype(F32), a_im.astype(F32))
    dt = jnp.exp(log_dt.astype(F32))[:, None]
    ldt = lam * dt
    abar = jnp.exp(ldt)
    bbar = ((abar - 1.0) / lam)[..., None] * lax.complex(b_re.astype(F32), b_im.astype(F32))
    cc = lax.complex(c_re.astype(F32), c_im.astype(F32))
    tau = jnp.arange(c + 1, dtype=F32)
    apow = jnp.exp(ldt[:, None, :] * tau[None, :, None])

    ca = cc[:, None, :, :] * apow[:, :c, None, :]
    taps = (jnp.einsum('gthp,gpk->gthk', ca.real, bbar.real, precision=HIGHEST)
            - jnp.einsum('gthp,gpk->gthk', ca.imag, bbar.imag, precision=HIGHEST))
    tt = jnp.arange(c)[:, None]
    ss = jnp.arange(c)[None, :]
    lag_is = ((tt - ss)[None, :, :] == jnp.arange(c)[:, None, None]).astype(F32)
    m = jnp.einsum('gvhk,vts->gskth', taps, lag_is, precision=HIGHEST)
    eye_t = jnp.eye(c, dtype=F32)[None, :, None, :, None]
    eye_h = jnp.eye(hc, dtype=F32)[None, None, :, None, :]
    m = m + eye_t * eye_h * d_skip.astype(F32)[:, None, None, None, :]
    m_t = m.reshape(g, c * hc, c * hc)

    bp = apow[:, c - 1 - jnp.arange(c), :][:, :, :, None] * bbar[:, None, :, :]
    bp = bp.transpose(0, 1, 3, 2).reshape(g, c * hc, p)
    bp_t = jnp.concatenate([bp.real, bp.imag], axis=-1)

    cp = cc[:, None, :, :] * apow[:, 1:c + 1, None, :]
    cp = cp.reshape(g, c * hc, p).transpose(0, 2, 1)
    cp_t = jnp.concatenate([cp.real, -cp.imag], axis=1)

    steps = (c * (2.0 ** jnp.arange(levels, dtype=F32)))
    alev = jnp.exp(ldt[:, None, :] * steps[None, :, None])

    gl = LANES // hc
    nj = g // gl

    def spread(compact, n_outer, inner):
        rows = compact.shape[2]
        cols = n_outer * inner
        sel = (jnp.arange(cols)[:, None] // inner == jnp.arange(n_outer * gl * inner)[None, :] // (gl * inner))
        sel = sel & (jnp.arange(cols)[:, None] % inner == jnp.arange(n_outer * gl * inner)[None, :] % inner)
        wide = jnp.einsum('jgrc,cw->jgrw', compact.astype(BF16), sel.astype(BF16),
                          preferred_element_type=F32).astype(BF16)
        own = (jnp.arange(n_outer * gl * inner)[None, :] // inner) % gl == jnp.arange(gl)[:, None]
        return jnp.where(own[None, :, None, :], wide, jnp.zeros((), BF16)), rows

    wide, _ = spread(m_t.reshape(nj, gl, c * hc, c * hc), c, hc)
    mj = wide.reshape(nj, gl, c, hc, c * LANES).transpose(0, 2, 1, 3, 4).reshape(nj, c * LANES, c * LANES)
    wide, _ = spread(bp_t.reshape(nj, gl, c * hc, 2 * p), 2, p)
    bpj = wide.reshape(nj, gl, c, hc, 2 * gl * p).transpose(0, 2, 1, 3, 4).reshape(nj, c * LANES, 2 * gl * p)
    wide, _ = spread(cp_t.reshape(nj, gl, 2 * p, c * hc), c, hc)
    cpj = wide.reshape(nj, gl, 2, p, c * LANES).transpose(0, 2, 1, 3, 4).reshape(nj, 2 * gl * p, c * LANES)
    al = alev.reshape(nj, gl, levels, p).transpose(0, 2, 1, 3).reshape(nj, levels, gl * p)
    pad = ((0, 0), (0, 16 - levels), (0, 0))
    a1 = jnp.pad(jnp.concatenate([al.real, al.real], axis=-1), pad)
    a2 = jnp.pad(jnp.concatenate([-al.imag, al.imag], axis=-1), pad)
    return mj, bpj, cpj, a1, a2


def _shift_rows(x, sh):
    n = x.shape[0]
    if sh % 8 == 0:
        return jnp.concatenate([jnp.zeros((sh, x.shape[1]), x.dtype), x[:n - sh]], axis=0)
    rolled = pltpu.roll(x, sh, axis=0)
    rows = lax.broadcasted_iota(jnp.int32, x.shape, 0)
    return jnp.where(rows < sh, 0.0, rolled)


def _cmul(a1, a2, x):
    return a1 * x + a2 * pltpu.roll(x, x.shape[1] // 2, axis=1)


def _s5_kernel(u_ref, um_ref, mj_ref, bpj_ref, cpj_ref, a1_ref, a2_ref, y_ref, *, nc, levels):
    c = S5_CHUNK
    u = jnp.concatenate([u_ref[pl.ds(s, nc, stride=c), :].astype(BF16) for s in range(c)], axis=1)
    bpj = bpj_ref[0]
    w = _dot(u, bpj)
    um = jnp.concatenate([um_ref[s:s + 1, :] for s in range(c)], axis=1)
    x_meta = _dot(jnp.broadcast_to(um, (8, um.shape[1])).astype(BF16), bpj)
    first = lax.broadcasted_iota(jnp.int32, x_meta.shape, 0) == 0
    inject = jnp.where(first, _cmul(a1_ref[0, 0:1, :], a2_ref[0, 0:1, :], x_meta), 0.0)
    x = jnp.concatenate([w[:8] + inject, w[8:]], axis=0)
    for lvl in range(levels):
        x = x + _cmul(a1_ref[0, lvl:lvl + 1, :], a2_ref[0, lvl:lvl + 1, :], _shift_rows(x, 2 ** lvl))
    xp = _shift_rows(x, 1)
    x_in = jnp.concatenate([xp[:8] + jnp.where(first, x_meta, 0.0), xp[8:]], axis=0).astype(BF16)
    y = jax.nn.gelu(_dot(u, mj_ref[0]) + _dot(x_in, cpj_ref[0]))
    for s in range(c):
        y_ref[pl.ds(s, nc, stride=c), :] = y[:, s * LANES:(s + 1) * LANES]


def _s5(u, u_m, mj, bpj, cpj, a1, a2, bsz, nc, levels):
    t, ssm_w = u.shape
    nj, cl, sw = bpj.shape
    nl = a1.shape[1]
    rows = nc * S5_CHUNK
    tab = lambda j, b: (j, 0, 0)
    one = pl.Buffered(1)
    return pl.pallas_call(
        functools.partial(_s5_kernel, nc=nc, levels=levels),
        grid=(nj, bsz),
        in_specs=[
            pl.BlockSpec((rows, LANES), lambda j, b: (b, j)),
            pl.BlockSpec((S5_CHUNK, LANES), lambda j, b: (0, j)),
            pl.BlockSpec((1, cl, cl), tab, pipeline_mode=one),
            pl.BlockSpec((1, cl, sw), tab, pipeline_mode=one),
            pl.BlockSpec((1, sw, cl), tab, pipeline_mode=one),
            pl.BlockSpec((1, nl, sw), tab),
            pl.BlockSpec((1, nl, sw), tab),
        ],
        out_specs=pl.BlockSpec((rows, LANES), lambda j, b: (b, j)),
        out_shape=jax.ShapeDtypeStruct((t, ssm_w), F32),
        compiler_params=_params(("parallel", "parallel"), 56),
        name="s5_scan",
    )(u, u_m, mj, bpj, cpj, a1, a2)


def _fox_kernel(qt_ref, k_ref, va_ref, ck_ref, km_ref, vam_ref, ckm_ref, o_ref,
                m_scr, acc_scr, s_scr, p_scr, al_scr, *, tq, dh):
    qi = pl.program_id(2)
    qt = qt_ref[...]
    reps = tq // LANES

    def scores(kb, ckb):
        return _dot(kb, qt) - jnp.concatenate([ckb] * reps, axis=1)

    def softmax_update(s):
        m_prev = m_scr[0:1, :]
        m_new = jnp.maximum(m_prev, jnp.max(s, axis=0, keepdims=True))
        m_scr[...] = jnp.broadcast_to(m_new, m_scr.shape)
        return jnp.exp2(m_prev - m_new), jnp.exp2((s - m_new).astype(BF16))

    s = scores(km_ref[...], ckm_ref[0])
    m0 = jnp.max(s, axis=0, keepdims=True)
    m_scr[...] = jnp.broadcast_to(m0, m_scr.shape)
    acc_scr[...] = _dot(vam_ref[0], jnp.exp2(s - m0).astype(BF16))

    start = pl.multiple_of(qi * tq, tq)
    s = scores(k_ref[pl.ds(start, tq), :], ck_ref[0, 0, pl.ds(start, tq), :])
    key = lax.broadcasted_iota(jnp.int32, (tq, tq), 0)
    qry = lax.broadcasted_iota(jnp.int32, (tq, tq), 1)
    ns = FOX_SLOTS
    s_scr[ns - 1] = jnp.where(key <= qry, s, NEG_BIG)
    p_scr[ns - 2] = jnp.zeros((tq, tq), BF16)
    al_scr[ns - 2] = jnp.ones((8, tq), F32)

    def tick(t, u):
        sb, sc = (u - 1) % ns, (u - 2) % ns
        jc = jnp.where(t == 1, qi, jnp.clip(t - 2, 0, qi))
        acc_scr[...] = al_scr[sc, 0:1, :] * acc_scr[...] + _dot(va_ref[0, 0, jc], p_scr[sc])
        alpha, p = softmax_update(s_scr[sb])
        p_scr[sb] = p
        al_scr[sb] = jnp.broadcast_to(alpha, (8, tq))
        ja = jnp.minimum(t, jnp.maximum(qi - 1, 0))
        start = pl.multiple_of(ja * tq, tq)
        off = jnp.where(t < qi, 0.0, -NEG_BIG)
        s_scr[u] = scores(k_ref[pl.ds(start, tq), :], ck_ref[0, 0, pl.ds(start, tq), :] + off)

    def body(i, carry):
        for u in range(ns):
            tick(ns * i + u, u)
        return carry

    lax.fori_loop(0, (qi + 2 + ns - 1) // ns, body, 0)
    acc = acc_scr[...]
    o_ref[...] = (acc[:dh] / acc[dh:dh + 1]).T.astype(BF16)


def _fox(qt, k, va, ck, km, vam, ckm, bsz, seq, heads, tq):
    aw, t = qt.shape
    nq = seq // tq
    dh = ATT_HEAD_DIM
    da = va.shape[3]
    return pl.pallas_call(
        functools.partial(_fox_kernel, tq=tq, dh=dh),
        grid=(bsz, heads, nq),
        in_specs=[
            pl.BlockSpec((dh, tq), lambda b, h, i: (h, b * nq + i)),
            pl.BlockSpec((seq, dh), lambda b, h, i: (b, h)),
            pl.BlockSpec((1, 1, nq, da, tq), lambda b, h, i: (b, h, 0, 0, 0)),
            pl.BlockSpec((1, 1, seq, LANES), lambda b, h, i: (b, h, 0, 0)),
            pl.BlockSpec((LANES, dh), lambda b, h, i: (0, h)),
            pl.BlockSpec((1, da, LANES), lambda b, h, i: (h, 0, 0)),
            pl.BlockSpec((1, LANES, LANES), lambda b, h, i: (h, 0, 0)),
        ],
        out_specs=pl.BlockSpec((tq, dh), lambda b, h, i: (b * nq + i, h)),
        out_shape=jax.ShapeDtypeStruct((t, aw), BF16),
        scratch_shapes=[
            pltpu.VMEM((8, tq), F32),
            pltpu.VMEM((da, tq), F32),
            pltpu.VMEM((FOX_SLOTS, tq, tq), F32),
            pltpu.VMEM((FOX_SLOTS, tq, tq), BF16),
            pltpu.VMEM((FOX_SLOTS, 8, tq), F32),
        ],
        compiler_params=_params(("parallel", "parallel", "arbitrary"), 48),
        name="fox_attention",
    )(qt, k, va, ck, km, vam, ckm)


def _mix_out_kernel(ys_ref, ya_ref, h_ref, wglu_ref, bglu_ref, wout_ref, g_ref, b_ref, wr_ref,
                    h2_ref, h2p_ref, lt_ref, *, ssm_w):
    ys = ys_ref[...]
    z = _dot(ys.astype(BF16), wglu_ref[...]) + bglu_ref[...]
    yg = (ys * jax.nn.sigmoid(z)).astype(BF16)
    mix = _dot(yg, wout_ref[0:ssm_w, :]) + _dot(ya_ref[...], wout_ref[ssm_w:, :])
    h2 = _layer_norm(DEEPNORM_ALPHA * h_ref[...] + mix, g_ref[...], b_ref[...])
    h2_ref[...] = h2
    _pack_rows(h2, h2p_ref)
    lt_ref[...] = _dot_nt(wr_ref[...], h2.astype(BF16))


def _mix_out(ys, ya, h, wglu, bglu, wout, g, b, wr_t, tm):
    t, d = h.shape
    ssm_w = ys.shape[1]
    att_w = ya.shape[1]
    e = wr_t.shape[0]
    row = lambda i: (i, 0)
    return pl.pallas_call(
        functools.partial(_mix_out_kernel, ssm_w=ssm_w),
        grid=(t // tm,),
        in_specs=[
            pl.BlockSpec((tm, ssm_w), row),
            pl.BlockSpec((tm, att_w), row),
            pl.BlockSpec((tm, d), row),
            _const_spec((ssm_w, ssm_w)), _const_spec((1, ssm_w)),
            _const_spec((d, d)), _const_spec((1, d)), _const_spec((1, d)),
            _const_spec((e, d)),
        ],
        out_specs=[
            pl.BlockSpec((tm, d), row),
            pl.BlockSpec((tm * ROW_TILE, LANES), row),
            pl.BlockSpec((e, tm), lambda i: (0, i)),
        ],
        out_shape=[
            jax.ShapeDtypeStruct((t, d), F32),
            jax.ShapeDtypeStruct((t * ROW_TILE, LANES), jnp.uint32),
            jax.ShapeDtypeStruct((e, t), F32),
        ],
        compiler_params=_params(("parallel",), 48),
        name="mix_out",
    )(ys, ya, h, wglu, bglu, wout, g, b, wr_t)


def _route_kernel(lt_ref, bias_ref, idx_ref, gate_ref, rank_ref, cnt_ref, run_scr, *, tn):
    e = N_EXPERTS
    ng = N_EXPERT_GROUPS
    gs = e // ng

    @pl.when(pl.program_id(0) == 0)
    def _():
        run_scr[...] = jnp.zeros_like(run_scr)

    scores = jax.nn.sigmoid(lt_ref[...])
    sel = scores + bias_ref[...]
    sel3 = sel.reshape(ng, gs, tn)
    mem = lax.broadcasted_iota(jnp.int32, (ng, gs, tn), 1)
    m1 = jnp.max(sel3, axis=1, keepdims=True)
    i1 = jnp.min(jnp.where(sel3 == m1, mem, gs), axis=1, keepdims=True)
    m2 = jnp.max(jnp.where(mem == i1, -jnp.inf, sel3), axis=1, keepdims=True)
    gscore = (m1 + m2).reshape(ng, tn)

    gio = lax.broadcasted_iota(jnp.int32, (ng, tn), 0)
    gmask = jnp.zeros((ng, tn), F32)
    for _ in range(TOPK_GROUPS):
        mx = jnp.max(gscore, axis=0, keepdims=True)
        gi = jnp.min(jnp.where(gscore == mx, gio, ng), axis=0, keepdims=True)
        hit = gio == gi
        gmask = jnp.where(hit, 1.0, gmask)
        gscore = jnp.where(hit, -jnp.inf, gscore)

    cand = jnp.where(gmask.reshape(ng, 1, tn) > 0.5, sel3, -jnp.inf).reshape(e, tn)
    eio = lax.broadcasted_iota(jnp.int32, (e, tn), 0)
    hits = []
    chosen = jnp.zeros((e, tn), F32)
    for _ in range(TOP_K):
        mx = jnp.max(cand, axis=0, keepdims=True)
        ei = jnp.min(jnp.where(cand == mx, eio, e), axis=0, keepdims=True)
        hit = eio == ei
        hits.append((ei, hit))
        chosen = jnp.where(hit, 1.0, chosen)
        cand = jnp.where(hit, -jnp.inf, cand)

    onehot = chosen.astype(BF16)
    r = lax.broadcasted_iota(jnp.int32, (tn, tn), 0)
    c = lax.broadcasted_iota(jnp.int32, (tn, tn), 1)
    tri = (r < c).astype(BF16)
    rank = _dot(onehot, tri) + run_scr[:, 0:1]
    run_scr[...] = run_scr[...] + jnp.sum(chosen, axis=1, keepdims=True)
    cnt_ref[...] = run_scr[...]

    gates = [jnp.sum(jnp.where(hit, scores, 0.0), axis=0, keepdims=True) for _, hit in hits]
    total = gates[0]
    for gk in gates[1:]:
        total = total + gk
    for kk, (ei, hit) in enumerate(hits):
        idx_ref[kk:kk + 1, :] = ei
        gate_ref[kk:kk + 1, :] = gates[kk] / total * ROUTED_SCALE
        rank_ref[kk:kk + 1, :] = jnp.sum(jnp.where(hit, rank, 0.0), axis=0, keepdims=True).astype(jnp.int32)


def _route(logits_t, bias, tn):
    e, t = logits_t.shape
    col = lambda i: (0, i)
    return pl.pallas_call(
        functools.partial(_route_kernel, tn=tn),
        grid=(t // tn,),
        in_specs=[pl.BlockSpec((e, tn), col), _const_spec((e, 1))],
        out_specs=[
            pl.BlockSpec((TOP_K, tn), col),
            pl.BlockSpec((TOP_K, tn), col),
            pl.BlockSpec((TOP_K, tn), col),
            pl.BlockSpec((e, LANES), lambda i: (0, 0)),
        ],
        out_shape=[
            jax.ShapeDtypeStruct((TOP_K, t), jnp.int32),
            jax.ShapeDtypeStruct((TOP_K, t), F32),
            jax.ShapeDtypeStruct((TOP_K, t), jnp.int32),
            jax.ShapeDtypeStruct((e, LANES), F32),
        ],
        scratch_shapes=[pltpu.VMEM((e, LANES), F32)],
        compiler_params=_params(("arbitrary",), 32),
        name="route",
    )(logits_t, bias)


def _tile_rows(ref, row):
    return ref.at[pl.ds(pl.multiple_of(row * ROW_TILE, ROW_TILE), ROW_TILE), :]


def _dispatch_kernel(ps_ref, pe_ref, tail_ref, pos_ref, x_ref, wgu_ref, wd_ref, xs_ref, sh_ref, zero_scr, sem,
                     *, tm, bm, n_blocks, ff):
    @pl.when(pl.program_id(0) == 0)
    def _():
        zero_scr[...] = jnp.zeros_like(zero_scr)

        def fill(start):
            def per_expert(e, carry):
                def per_slot(slot, c):
                    cp = pltpu.make_async_copy(zero_scr.at[0:ROW_TILE, :], _tile_rows(xs_ref, slot), sem)
                    cp.start() if start else cp.wait()
                    return c
                return lax.fori_loop(ps_ref[e], pe_ref[e], per_slot, carry)
            lax.fori_loop(0, N_EXPERTS, per_expert, 0)

            def per_block(b, c):
                rows = pl.ds(pl.multiple_of(b * bm * ROW_TILE, bm * ROW_TILE), bm * ROW_TILE)
                cp = pltpu.make_async_copy(zero_scr, xs_ref.at[rows, :], sem)
                cp.start() if start else cp.wait()
                return c
            lax.fori_loop(tail_ref[0], n_blocks, per_block, 0)

        fill(True)
        fill(False)

    def issue(t, carry):
        src = _tile_rows(x_ref, t)
        for kk in range(TOP_K):
            pltpu.make_async_copy(src, _tile_rows(xs_ref, pos_ref[kk, t]), sem).start()
        return carry

    lax.fori_loop(0, tm, issue, 0)
    lo, hi = _unpack_rows(x_ref, 0, tm)
    x = jnp.concatenate([p.astype(BF16) for p in lo + hi], axis=1)
    gu = _dot(x, wgu_ref[...])
    hid = jax.nn.silu(gu[:, :ff]) * gu[:, ff:]
    sh_ref[...] = _dot(hid.astype(BF16), wd_ref[...])
    for _ in range(TOP_K):
        pltpu.make_async_copy(x_ref, xs_ref.at[pl.ds(0, tm * ROW_TILE), :], sem).wait()


def _dispatch(pad_start, pad_end, tail_block, pos, h2p, wgu, wd, n_blocks, bm, tm):
    t = pos.shape[1]
    ff, d = wd.shape
    grid_spec = pltpu.PrefetchScalarGridSpec(
        num_scalar_prefetch=3,
        grid=(t // tm,),
        in_specs=[
            pl.BlockSpec((TOP_K, tm), lambda i, ps, pe, tl: (0, i), memory_space=pltpu.SMEM),
            pl.BlockSpec((tm * ROW_TILE, LANES), lambda i, ps, pe, tl: (i, 0)),
            _const_spec((d, 2 * ff)), _const_spec((ff, d)),
        ],
        out_specs=[pl.BlockSpec(memory_space=pl.ANY), pl.BlockSpec((tm, d), lambda i, ps, pe, tl: (i, 0))],
        scratch_shapes=[pltpu.VMEM((bm * ROW_TILE, LANES), jnp.uint32), pltpu.SemaphoreType.DMA(())],
    )
    return pl.pallas_call(
        functools.partial(_dispatch_kernel, tm=tm, bm=bm, n_blocks=n_blocks, ff=ff),
        grid_spec=grid_spec,
        out_shape=[jax.ShapeDtypeStruct((n_blocks * bm * ROW_TILE, LANES), jnp.uint32),
                   jax.ShapeDtypeStruct((t, d), F32)],
        compiler_params=_params(("arbitrary",), 40),
        name="dispatch",
    )(pad_start, pad_end, tail_block, pos, h2p, wgu, wd)


def _moe_kernel(be_ref, bv_ref, bf_ref, nx_ref, sl_ref, x_ref, wg_hbm, wu_hbm, wd_hbm, y_ref,
                wg_f, wu_f, wd_f, wg_s, wu_s, wd_s, sem, *, bm):
    b = pl.program_id(0)

    def weights(e, slot, start):
        for i, (src, dst) in enumerate(((wg_hbm, wg_f), (wu_hbm, wu_f), (wd_hbm, wd_f))):
            cp = pltpu.make_async_copy(src.at[e], dst.at[slot], sem.at[slot, i])
            cp.start() if start else cp.wait()

    @pl.when(bf_ref[b] == 1)
    def _():
        slot = sl_ref[b]

        @pl.when(b == 0)
        def _():
            weights(be_ref[0], slot, True)

        weights(be_ref[b], slot, False)

        @pl.when(nx_ref[b] >= 0)
        def _():
            weights(nx_ref[b], 1 - slot, True)

        wg_s[...] = wg_f[slot].astype(BF16)
        wu_s[...] = wu_f[slot].astype(BF16)
        wd_s[...] = wd_f[slot].astype(BF16)

    @pl.when(bv_ref[b] == 1)
    def _():
        lo, hi = _unpack_rows(x_ref, 0, bm)
        x = jnp.concatenate([p.astype(BF16) for p in lo + hi], axis=1)
        hid = jax.nn.silu(_dot(x, wg_s[...])) * _dot(x, wu_s[...])
        _pack_rows(_dot(hid.astype(BF16), wd_s[...]), y_ref)

    @pl.when(bv_ref[b] == 0)
    def _():
        y_ref[...] = jnp.zeros_like(y_ref)


def _moe_grouped(block_e, block_valid, block_first, next_e, slot_of, xs, wg, wu, wd, bm):
    rows = xs.shape[0] // ROW_TILE
    d, ff = wg.shape[1], wg.shape[2]
    n_blocks = rows // bm
    blk = lambda b, *_: (b, 0)
    grid_spec = pltpu.PrefetchScalarGridSpec(
        num_scalar_prefetch=5,
        grid=(n_blocks,),
        in_specs=[
            pl.BlockSpec((bm * ROW_TILE, LANES), blk),
            pl.BlockSpec(memory_space=pl.ANY), pl.BlockSpec(memory_space=pl.ANY), pl.BlockSpec(memory_space=pl.ANY),
        ],
        out_specs=pl.BlockSpec((bm * ROW_TILE, LANES), blk),
        scratch_shapes=[
            pltpu.VMEM((2, d, ff), F32), pltpu.VMEM((2, d, ff), F32), pltpu.VMEM((2, ff, d), F32),
            pltpu.VMEM((d, ff), BF16), pltpu.VMEM((d, ff), BF16), pltpu.VMEM((ff, d), BF16),
            pltpu.SemaphoreType.DMA((2, 3)),
        ],
    )
    return pl.pallas_call(
        functools.partial(_moe_kernel, bm=bm),
        grid_spec=grid_spec,
        out_shape=jax.ShapeDtypeStruct(xs.shape, jnp.uint32),
        compiler_params=_params(("arbitrary",), 56),
        name="moe_grouped",
    )(block_e, block_valid, block_first, next_e, slot_of, xs, wg, wu, wd)


def _final_kernel(pos_ref, posn_ref, h2_ref, sh_ref, gate_ref, g_ref, b_ref, ys_ref, o_ref, ybuf, sem, *, tm):
    i = pl.program_id(0)
    slot = i % 2
    tile = TOP_K * tm * ROW_TILE

    def gather(p_ref, dst_slot):
        def issue(t, carry):
            for kk in range(TOP_K):
                dst = ybuf.at[pl.ds(pl.multiple_of(dst_slot * tile + (kk * tm + t) * ROW_TILE, ROW_TILE), ROW_TILE), :]
                pltpu.make_async_copy(_tile_rows(ys_ref, p_ref[kk, t]), dst, sem.at[dst_slot]).start()
            return carry
        lax.fori_loop(0, tm, issue, 0)

    @pl.when(i == 0)
    def _():
        gather(pos_ref, slot)

    @pl.when(i + 1 < pl.num_programs(0))
    def _():
        gather(posn_ref, 1 - slot)

    base = pl.multiple_of(slot * tile, tile)
    for kk in range(TOP_K):
        pltpu.make_async_copy(ys_ref.at[pl.ds(0, tm * ROW_TILE), :],
                              ybuf.at[pl.ds(base + kk * tm * ROW_TILE, tm * ROW_TILE), :], sem.at[slot]).wait()
    gate = gate_ref[...]
    acc = None
    for kk in range(TOP_K):
        lo, hi = _unpack_rows(ybuf, base + kk * tm * ROW_TILE, tm)
        gk = gate[:, kk:kk + 1]
        terms = [gk * p for p in lo + hi]
        acc = terms if acc is None else [a + b for a, b in zip(acc, terms)]
    ffn = jnp.concatenate(acc, axis=1) + sh_ref[...]
    o_ref[...] = _layer_norm(DEEPNORM_ALPHA * h2_ref[...] + ffn, g_ref[...], b_ref[...])


def _final(pos, h2, shared, gate, ys, g, b, tm):
    t, d = h2.shape
    row = lambda i: (i, 0)
    return pl.pallas_call(
        functools.partial(_final_kernel, tm=tm),
        grid=(t // tm,),
        in_specs=[
            pl.BlockSpec((TOP_K, tm), lambda i: (0, i), memory_space=pltpu.SMEM),
            pl.BlockSpec((TOP_K, tm), lambda i: (0, jnp.minimum(i + 1, t // tm - 1)), memory_space=pltpu.SMEM),
            pl.BlockSpec((tm, d), row),
            pl.BlockSpec((tm, d), row),
            pl.BlockSpec((tm, TOP_K), row),
            _const_spec((1, d)), _const_spec((1, d)),
            pl.BlockSpec(memory_space=pl.ANY),
        ],
        out_specs=pl.BlockSpec((tm, d), row),
        out_shape=jax.ShapeDtypeStruct((t, d), F32),
        scratch_shapes=[pltpu.VMEM((2 * TOP_K * tm * ROW_TILE, LANES), jnp.uint32), pltpu.SemaphoreType.DMA((2,))],
        compiler_params=_params(("arbitrary",), 56),
        name="final",
    )(pos, pos, h2, shared, gate, g, b, ys)


def kernel(x, meta_tokens, ln_in_g, ln_in_b, w_in, b_forget, ssm_a_re, ssm_a_im, ssm_log_dt, ssm_b_re, ssm_b_im, ssm_c_re, ssm_c_im, ssm_d, w_glu, b_glu, w_out, ln_mix_g, ln_mix_b, w_router, router_bias, w_gate_exp, w_up_exp, w_down_exp, w_gate_sh, w_up_sh, w_down_sh, ln_ffn_g, ln_ffn_b):
    bsz, seq, d = x.shape
    t = bsz * seq
    ssm_w = w_glu.shape[1]
    heads = b_forget.shape[1]
    att_w = (w_in.shape[2] - ssm_w - heads) // 3
    groups = ssm_a_re.shape[1]
    assert meta_tokens.shape[0] == N_META == S5_CHUNK and att_w == heads * ATT_HEAD_DIM
    nc = seq // S5_CHUNK
    levels = int(math.log2(nc))
    assert 2 ** levels == nc and nc % 8 == 0
    row2 = lambda a: a.reshape(1, -1).astype(F32)

    w_u, w_q, w_k, w_v, w_f = jnp.split(w_in[0], [ssm_w, ssm_w + att_w, ssm_w + 2 * att_w, ssm_w + 3 * att_w], axis=1)
    w_f = jnp.pad(w_f, ((0, 0), (0, LANES - heads)))
    w_in_bf = jnp.concatenate([w_u, w_k, w_f], axis=1).astype(BF16)
    wqvt_bf = jnp.concatenate([w_q, w_v], axis=1).T.astype(BF16)
    bf_pad = jnp.pad(b_forget[0].astype(F32), (0, LANES - heads)).reshape(1, LANES)
    g_in, b_in = row2(ln_in_g), row2(ln_in_b)
    tm = min(256, t)
    h, u, qt, k, vt, lf = _ln_inproj(x.reshape(t, d), g_in, b_in, w_in_bf, wqvt_bf, bf_pad, tm, ssm_w, att_w)
    _, u_m, _, k_m, vt_m, lf_m = _ln_inproj(meta_tokens.astype(F32), g_in, b_in, w_in_bf, wqvt_bf, bf_pad, N_META,
                                            ssm_w, att_w)

    mj, bpj, cpj, a1, a2 = _s5_tables(ssm_a_re[0], ssm_a_im[0], ssm_log_dt[0], ssm_b_re[0], ssm_b_im[0],
                                      ssm_c_re[0], ssm_c_im[0], ssm_d[0], levels)
    y_ssm = _s5(u, u_m, mj, bpj, cpj, a1, a2, bsz, nc, levels)

    tq = min(512, seq)
    nq = seq // tq
    dh = ATT_HEAD_DIM
    ones_rows = 16
    lfm = lf_m[:, :heads] * LOG2E
    c_meta = jnp.cumsum(lfm, axis=0) - jnp.sum(lfm, axis=0, keepdims=True)
    ckm = jnp.full((heads, LANES), -NEG_BIG, F32).at[:, :N_META].set(c_meta.T)
    ckm = jnp.broadcast_to(ckm[:, :, None], (heads, LANES, LANES))
    c_main = jnp.cumsum(lf[:, :heads].reshape(bsz, seq, heads) * LOG2E, axis=1)
    ck = jnp.broadcast_to(c_main.transpose(0, 2, 1)[..., None], (bsz, heads, seq, LANES))
    km = jnp.pad(k_m, ((0, LANES - N_META), (0, 0)))
    va = vt.reshape(heads, dh, bsz, nq, tq).transpose(2, 0, 3, 1, 4)
    va = jnp.concatenate([va, jnp.ones((bsz, heads, nq, ones_rows, tq), BF16)], axis=3)
    vam = jnp.pad(vt_m.reshape(heads, dh, N_META), ((0, 0), (0, 0), (0, LANES - N_META)))
    vam = jnp.concatenate([vam, jnp.ones((heads, ones_rows, LANES), BF16)], axis=1)
    y_att = _fox(qt, k, va, ck, km, vam, ckm, bsz, seq, heads, tq)

    h2, h2p, logits_t = _mix_out(
        y_ssm, y_att, h, w_glu[0].astype(BF16), row2(b_glu[0]), w_out[0].astype(BF16),
        row2(ln_mix_g[0]), row2(ln_mix_b[0]), w_router[0].T.astype(BF16), tm)

    tn = min(512, t)
    idx_t, gate_t, rank_t, counts = _route(logits_t, router_bias[0].astype(F32).reshape(N_EXPERTS, 1), tn)

    bm = 256
    counts = counts[:, 0].astype(jnp.int32)
    pcounts = (counts + bm - 1) // bm * bm
    pends = jnp.cumsum(pcounts)
    pstarts = pends - pcounts
    start_of = jnp.sum(jnp.where(idx_t[:, :, None] == jnp.arange(N_EXPERTS, dtype=jnp.int32), pstarts, 0), axis=-1)
    pos = (start_of + rank_t).astype(jnp.int32)
    n_blocks = t * TOP_K // bm + N_EXPERTS
    bstart = jnp.arange(n_blocks, dtype=jnp.int32) * bm
    block_e = jnp.minimum(jnp.sum(pends[None, :] <= bstart[:, None], axis=1), N_EXPERTS - 1).astype(jnp.int32)
    block_valid = (bstart < pends[-1]).astype(jnp.int32)
    block_first = block_valid * jnp.concatenate([jnp.ones((1,), jnp.int32),
                                                 (block_e[1:] != block_e[:-1]).astype(jnp.int32)])
    slot_of = ((jnp.cumsum(block_first) - 1) % 2).astype(jnp.int32)
    n_valid = pends[-1] // bm
    nxt = jnp.arange(n_blocks, dtype=jnp.int32) + pcounts[block_e] // bm
    next_e = jnp.where(nxt < n_valid, block_e[jnp.minimum(nxt, n_blocks - 1)], -1).astype(jnp.int32)
    wgu_sh = jnp.concatenate([w_gate_sh[0], w_up_sh[0]], axis=1).astype(BF16)
    xs, shared = _dispatch((pstarts + counts).astype(jnp.int32), pends.astype(jnp.int32),
                           (pends[-1:] // bm).astype(jnp.int32), pos, h2p, wgu_sh, w_down_sh[0].astype(BF16),
                           n_blocks, bm, tm)
    ys = _moe_grouped(block_e, block_valid, block_first, next_e, slot_of, xs,
                      w_gate_exp[0], w_up_exp[0], w_down_exp[0], bm)

    out = _final(pos, h2, shared, gate_t.T, ys, row2(ln_ffn_g[0]), row2(ln_ffn_b[0]), tm)
    return out.reshape(bsz, seq, d)
```

```python
import functools
import math

import jax
import jax.numpy as jnp
from jax import lax
from jax.experimental import pallas as pl
from jax.experimental.pallas import tpu as pltpu

N_META = 16
SSM_GROUP_CH = 16
SSM_STATE = 64
ATT_HEAD_DIM = 128
N_EXPERTS = 64
TOP_K = 8
N_EXPERT_GROUPS = 8
TOPK_GROUPS = 4
ROUTED_SCALE = 2.5
LN_EPS = 1e-5
DEPTH = 1
DEEPNORM_ALPHA = (2 * DEPTH) ** 0.25

S5_CHUNK = 16
FOX_SLOTS = 4
LANES = 128
ROW_TILE = 8
NEG_BIG = -1e30
LOG2E = 1.4426950408889634

F32 = jnp.float32
BF16 = jnp.bfloat16
HIGHEST = lax.Precision.HIGHEST


def _dot(a, b):
    return jnp.dot(a, b, preferred_element_type=F32)


def _dot_nt(a, b):
    return lax.dot_general(a, b, (((1,), (1,)), ((), ())), preferred_element_type=F32)


def _layer_norm(x, g, b):
    mu = jnp.mean(x, axis=-1, keepdims=True)
    xc = x - mu
    var = jnp.mean(xc * xc, axis=-1, keepdims=True)
    return xc * lax.rsqrt(var + LN_EPS) * g + b


def _pack_rows(x, o_ref):
    m, w = x.shape
    half = w // 2
    assert half == ROW_TILE * LANES
    bits = lax.bitcast_convert_type(x.astype(BF16).astype(F32), jnp.uint32)
    packed = bits[:, half:] | (bits[:, :half] >> 16)
    for s in range(ROW_TILE):
        o_ref[pl.ds(s, m, stride=ROW_TILE), :] = packed[:, s * LANES:(s + 1) * LANES]


def _unpack_rows(x_ref, base, m):
    lo, hi = [], []
    for s in range(ROW_TILE):
        w = x_ref[pl.ds(base + s, m, stride=ROW_TILE), :]
        lo.append(lax.bitcast_convert_type(w << 16, F32))
        hi.append(lax.bitcast_convert_type(w & jnp.uint32(0xFFFF0000), F32))
    return lo, hi


def _params(sem, vmem_mb):
    return pltpu.CompilerParams(dimension_semantics=sem, vmem_limit_bytes=vmem_mb * 1024 * 1024)


def _const_spec(shape):
    nd = len(shape)
    return pl.BlockSpec(shape, lambda *_: (0,) * nd, pipeline_mode=pl.Buffered(1))


def _ln_inproj_kernel(x_ref, g_ref, b_ref, w_ref, wqvt_ref, bf_ref, h_ref, u_ref, qt_ref, k_ref, vt_ref, lf_ref,
                      *, ssm_w, att_w):
    h = _layer_norm(x_ref[...], g_ref[...], b_ref[...])
    h_ref[...] = h
    hb = h.astype(BF16)
    o = 0
    u_ref[...] = _dot(hb, w_ref[:, o:o + ssm_w])
    o += ssm_w
    k_ref[...] = _dot(hb, w_ref[:, o:o + att_w]).astype(BF16)
    o += att_w
    qvt = _dot_nt(wqvt_ref[...], hb)
    qt_ref[...] = (qvt[:att_w] * (LOG2E * ATT_HEAD_DIM ** -0.5)).astype(BF16)
    vt_ref[...] = qvt[att_w:].astype(BF16)
    f = _dot(hb, w_ref[:, o:o + LANES]) + bf_ref[...]
    lf_ref[...] = jnp.minimum(f, 0.0) - jnp.log(1.0 + jnp.exp(-jnp.abs(f)))


def _ln_inproj(x2d, g, b, w_bf, wqvt_bf, bf_pad, tm, ssm_w, att_w):
    t, d = x2d.shape
    wcols = w_bf.shape[1]
    row = lambda i: (i, 0)
    col = lambda i: (0, i)
    return pl.pallas_call(
        functools.partial(_ln_inproj_kernel, ssm_w=ssm_w, att_w=att_w),
        grid=(t // tm,),
        in_specs=[
            pl.BlockSpec((tm, d), row),
            _const_spec((1, d)), _const_spec((1, d)),
            _const_spec((d, wcols)), _const_spec((2 * att_w, d)), _const_spec((1, LANES)),
        ],
        out_specs=[
            pl.BlockSpec((tm, d), row),
            pl.BlockSpec((tm, ssm_w), row),
            pl.BlockSpec((att_w, tm), col),
            pl.BlockSpec((tm, att_w), row),
            pl.BlockSpec((att_w, tm), col),
            pl.BlockSpec((tm, LANES), row),
        ],
        out_shape=[
            jax.ShapeDtypeStruct((t, d), F32),
            jax.ShapeDtypeStruct((t, ssm_w), F32),
            jax.ShapeDtypeStruct((att_w, t), BF16),
            jax.ShapeDtypeStruct((t, att_w), BF16),
            jax.ShapeDtypeStruct((att_w, t), BF16),
            jax.ShapeDtypeStruct((t, LANES), F32),
        ],
        compiler_params=_params(("parallel",), 56),
        name="ln_inproj",
    )(x2d, g, b, w_bf, wqvt_bf, bf_pad)


def _s5_tables(a_re, a_im, log_dt, b_re, b_im, c_re, c_im, d_skip, levels):
    c = S5_CHUNK
    g, p = a_re.shape
    hc = SSM_GROUP_CH
    lam = lax.complex(a_re.astype(F32), a_im.astype(F32))
    dt = jnp.exp(log_dt.astype(F32))[:, None]
    ldt = lam * dt
    abar = jnp.exp(ldt)
    bbar = ((abar - 1.0) / lam)[..., None] * lax.complex(b_re.astype(F32), b_im.astype(F32))
    cc = lax.complex(c_re.astype(F32), c_im.astype(F32))
    tau = jnp.arange(c + 1, dtype=F32)
    apow = jnp.exp(ldt[:, None, :] * tau[None, :, None])

    ca = cc[:, None, :, :] * apow[:, :c, None, :]
    taps = (jnp.einsum('gthp,gpk->gthk', ca.real, bbar.real, precision=HIGHEST)
            - jnp.einsum('gthp,gpk->gthk', ca.imag, bbar.imag, precision=HIGHEST))
    tt = jnp.arange(c)[:, None]
    ss = jnp.arange(c)[None, :]
    lag_is = ((tt - ss)[None, :, :] == jnp.arange(c)[:, None, None]).astype(F32)
    m = jnp.einsum('gvhk,vts->gskth', taps, lag_is, precision=HIGHEST)
    eye_t = jnp.eye(c, dtype=F32)[None, :, None, :, None]
    eye_h = jnp.eye(hc, dtype=F32)[None, None, :, None, :]
    m = m + eye_t * eye_h * d_skip.astype(F32)[:, None, None, None, :]
    m_t = m.reshape(g, c * hc, c * hc)

    bp = apow[:, c - 1 - jnp.arange(c), :][:, :, :, None] * bbar[:, None, :, :]
    bp = bp.transpose(0, 1, 3, 2).reshape(g, c * hc, p)
    bp_t = jnp.concatenate([bp.real, bp.imag], axis=-1)

    cp = cc[:, None, :, :] * apow[:, 1:c + 1, None, :]
    cp = cp.reshape(g, c * hc, p).transpose(0, 2, 1)
    cp_t = jnp.concatenate([cp.real, -cp.imag], axis=1)

    steps = (c * (2.0 ** jnp.arange(levels, dtype=F32)))
    alev = jnp.exp(ldt[:, None, :] * steps[None, :, None])

    gl = LANES // hc
    nj = g // gl

    def spread(compact, n_outer, inner):
        cols = n_outer * inner
        wcols = n_outer * gl * inner
        sel = (jnp.arange(cols)[:, None] // inner == jnp.arange(wcols)[None, :] // (gl * inner))
        sel = sel & (jnp.arange(cols)[:, None] % inner == jnp.arange(wcols)[None, :] % inner)
        wide = jnp.einsum('jagbc,cw->jagbw', compact.astype(BF16), sel.astype(BF16),
                          preferred_element_type=BF16)
        own = (jnp.arange(wcols)[None, :] // inner) % gl == jnp.arange(gl)[:, None]
        wide = jnp.where(own[None, None, :, None, :], wide, jnp.zeros((), BF16))
        return wide.reshape(nj, -1, wcols)

    mj = spread(m_t.reshape(nj, gl, c, hc, c * hc).transpose(0, 2, 1, 3, 4), c, hc)
    bpj = spread(bp_t.reshape(nj, gl, c, hc, 2 * p).transpose(0, 2, 1, 3, 4), 2, p)
    cpj = spread(cp_t.reshape(nj, gl, 2, p, c * hc).transpose(0, 2, 1, 3, 4), c, hc)
    al = alev.reshape(nj, gl, levels, p).transpose(0, 2, 1, 3).reshape(nj, levels, gl * p)
    pad = ((0, 0), (0, 16 - levels), (0, 0))
    a1 = jnp.pad(jnp.concatenate([al.real, al.real], axis=-1), pad)
    a2 = jnp.pad(jnp.concatenate([-al.imag, al.imag], axis=-1), pad)
    return mj, bpj, cpj, a1, a2


def _shift_rows(x, sh):
    n = x.shape[0]
    if sh % 8 == 0:
        return jnp.concatenate([jnp.zeros((sh, x.shape[1]), x.dtype), x[:n - sh]], axis=0)
    rolled = pltpu.roll(x, sh, axis=0)
    rows = lax.broadcasted_iota(jnp.int32, x.shape, 0)
    return jnp.where(rows < sh, 0.0, rolled)


def _cmul(a1, a2, x):
    return a1 * x + a2 * pltpu.roll(x, x.shape[1] // 2, axis=1)


def _s5_kernel(u_ref, um_ref, mj_ref, bpj_ref, cpj_ref, a1_ref, a2_ref, y_ref, *, nc, levels):
    c = S5_CHUNK
    u = jnp.concatenate([u_ref[pl.ds(s, nc, stride=c), :].astype(BF16) for s in range(c)], axis=1)
    bpj = bpj_ref[0]
    w = _dot(u, bpj)
    um = jnp.concatenate([um_ref[s:s + 1, :] for s in range(c)], axis=1)
    x_meta = _dot(jnp.broadcast_to(um, (8, um.shape[1])).astype(BF16), bpj)
    first = lax.broadcasted_iota(jnp.int32, x_meta.shape, 0) == 0
    inject = jnp.where(first, _cmul(a1_ref[0, 0:1, :], a2_ref[0, 0:1, :], x_meta), 0.0)
    x = jnp.concatenate([w[:8] + inject, w[8:]], axis=0)
    for lvl in range(levels):
        x = x + _cmul(a1_ref[0, lvl:lvl + 1, :], a2_ref[0, lvl:lvl + 1, :], _shift_rows(x, 2 ** lvl))
    xp = _shift_rows(x, 1)
    x_in = jnp.concatenate([xp[:8] + jnp.where(first, x_meta, 0.0), xp[8:]], axis=0).astype(BF16)
    y = jax.nn.gelu(_dot(u, mj_ref[0]) + _dot(x_in, cpj_ref[0]))
    for s in range(c):
        y_ref[pl.ds(s, nc, stride=c), :] = y[:, s * LANES:(s + 1) * LANES]


def _s5(u, u_m, mj, bpj, cpj, a1, a2, bsz, nc, levels):
    t, ssm_w = u.shape
    nj, cl, sw = bpj.shape
    nl = a1.shape[1]
    rows = nc * S5_CHUNK
    tab = lambda j, b: (j, 0, 0)
    one = pl.Buffered(1)
    return pl.pallas_call(
        functools.partial(_s5_kernel, nc=nc, levels=levels),
        grid=(nj, bsz),
        in_specs=[
            pl.BlockSpec((rows, LANES), lambda j, b: (b, j)),
            pl.BlockSpec((S5_CHUNK, LANES), lambda j, b: (0, j)),
            pl.BlockSpec((1, cl, cl), tab, pipeline_mode=one),
            pl.BlockSpec((1, cl, sw), tab, pipeline_mode=one),
            pl.BlockSpec((1, sw, cl), tab, pipeline_mode=one),
            pl.BlockSpec((1, nl, sw), tab),
            pl.BlockSpec((1, nl, sw), tab),
        ],
        out_specs=pl.BlockSpec((rows, LANES), lambda j, b: (b, j)),
        out_shape=jax.ShapeDtypeStruct((t, ssm_w), F32),
        compiler_params=_params(("parallel", "parallel"), 56),
        name="s5_scan",
    )(u, u_m, mj, bpj, cpj, a1, a2)


def _fox_kernel(qt_ref, k_ref, va_ref, ck_ref, km_ref, vam_ref, ckm_ref, o_ref,
                m_scr, acc_scr, s_scr, p_scr, al_scr, *, tq, dh):
    qi = pl.program_id(2)
    qt = qt_ref[...]
    reps = tq // LANES

    def scores(kb, ckb):
        return _dot(kb, qt) - jnp.concatenate([ckb] * reps, axis=1)

    def softmax_update(s):
        m_prev = m_scr[0:1, :]
        m_new = jnp.maximum(m_prev, jnp.max(s, axis=0, keepdims=True))
        m_scr[...] = jnp.broadcast_to(m_new, m_scr.shape)
        return jnp.exp2(m_prev - m_new), jnp.exp2((s - m_new).astype(BF16))

    s = scores(km_ref[...], ckm_ref[0])
    m0 = jnp.max(s, axis=0, keepdims=True)
    m_scr[...] = jnp.broadcast_to(m0, m_scr.shape)
    acc_scr[...] = _dot(vam_ref[0], jnp.exp2(s - m0).astype(BF16))

    start = pl.multiple_of(qi * tq, tq)
    s = scores(k_ref[pl.ds(start, tq), :], ck_ref[0, 0, pl.ds(start, tq), :])
    key = lax.broadcasted_iota(jnp.int32, (tq, tq), 0)
    qry = lax.broadcasted_iota(jnp.int32, (tq, tq), 1)
    ns = FOX_SLOTS
    s_scr[ns - 1] = jnp.where(key <= qry, s, NEG_BIG)
    p_scr[ns - 2] = jnp.zeros((tq, tq), BF16)
    al_scr[ns - 2] = jnp.ones((8, tq), F32)

    def tick(t, u):
        sb, sc = (u - 1) % ns, (u - 2) % ns
        jc = jnp.where(t == 1, qi, jnp.clip(t - 2, 0, qi))
        acc_scr[...] = al_scr[sc, 0:1, :] * acc_scr[...] + _dot(va_ref[0, 0, jc], p_scr[sc])
        alpha, p = softmax_update(s_scr[sb])
        p_scr[sb] = p
        al_scr[sb] = jnp.broadcast_to(alpha, (8, tq))
        ja = jnp.minimum(t, jnp.maximum(qi - 1, 0))
        start = pl.multiple_of(ja * tq, tq)
        off = jnp.where(t < qi, 0.0, -NEG_BIG)
        s_scr[u] = scores(k_ref[pl.ds(start, tq), :], ck_ref[0, 0, pl.ds(start, tq), :] + off)

    def body(i, carry):
        for u in range(ns):
            tick(ns * i + u, u)
        return carry

    n_ticks = (qi + 3) // 2 * 2
    n_full = n_ticks // ns
    lax.fori_loop(0, n_full, body, 0)

    @pl.when(n_ticks % ns != 0)
    def _():
        tick(ns * n_full, 0)
        tick(ns * n_full + 1, 1)
    acc = acc_scr[...]
    o_ref[...] = (acc[:dh] / acc[dh:dh + 1]).T.astype(BF16)


def _fox(qt, k, va, ck, km, vam, ckm, bsz, seq, heads, tq):
    aw, t = qt.shape
    nq = seq // tq
    dh = ATT_HEAD_DIM
    da = va.shape[3]
    return pl.pallas_call(
        functools.partial(_fox_kernel, tq=tq, dh=dh),
        grid=(bsz, heads, nq),
        in_specs=[
            pl.BlockSpec((dh, tq), lambda b, h, i: (h, b * nq + i)),
            pl.BlockSpec((seq, dh), lambda b, h, i: (b, h)),
            pl.BlockSpec((1, 1, nq, da, tq), lambda b, h, i: (b, h, 0, 0, 0)),
            pl.BlockSpec((1, 1, seq, LANES), lambda b, h, i: (b, h, 0, 0)),
            pl.BlockSpec((LANES, dh), lambda b, h, i: (0, h)),
            pl.BlockSpec((1, da, LANES), lambda b, h, i: (h, 0, 0)),
            pl.BlockSpec((1, LANES, LANES), lambda b, h, i: (h, 0, 0)),
        ],
        out_specs=pl.BlockSpec((tq, dh), lambda b, h, i: (b * nq + i, h)),
        out_shape=jax.ShapeDtypeStruct((t, aw), BF16),
        scratch_shapes=[
            pltpu.VMEM((8, tq), F32),
            pltpu.VMEM((da, tq), F32),
            pltpu.VMEM((FOX_SLOTS, tq, tq), F32),
            pltpu.VMEM((FOX_SLOTS, tq, tq), BF16),
            pltpu.VMEM((FOX_SLOTS, 8, tq), F32),
        ],
        compiler_params=_params(("parallel", "parallel", "arbitrary"), 48),
        name="fox_attention",
    )(qt, k, va, ck, km, vam, ckm)


def _mix_out_kernel(ys_ref, ya_ref, h_ref, wglu_ref, bglu_ref, wout_ref, g_ref, b_ref, wr_ref,
                    h2_ref, h2p_ref, lt_ref, *, ssm_w):
    ys = ys_ref[...]
    z = _dot(ys.astype(BF16), wglu_ref[...]) + bglu_ref[...]
    yg = (ys * jax.nn.sigmoid(z)).astype(BF16)
    mix = _dot(yg, wout_ref[0:ssm_w, :]) + _dot(ya_ref[...], wout_ref[ssm_w:, :])
    h2 = _layer_norm(DEEPNORM_ALPHA * h_ref[...] + mix, g_ref[...], b_ref[...])
    h2_ref[...] = h2
    _pack_rows(h2, h2p_ref)
    lt_ref[...] = _dot_nt(wr_ref[...], h2.astype(BF16))


def _mix_out(ys, ya, h, wglu, bglu, wout, g, b, wr_t, tm):
    t, d = h.shape
    ssm_w = ys.shape[1]
    att_w = ya.shape[1]
    e = wr_t.shape[0]
    row = lambda i: (i, 0)
    return pl.pallas_call(
        functools.partial(_mix_out_kernel, ssm_w=ssm_w),
        grid=(t // tm,),
        in_specs=[
            pl.BlockSpec((tm, ssm_w), row),
            pl.BlockSpec((tm, att_w), row),
            pl.BlockSpec((tm, d), row),
            _const_spec((ssm_w, ssm_w)), _const_spec((1, ssm_w)),
            _const_spec((d, d)), _const_spec((1, d)), _const_spec((1, d)),
            _const_spec((e, d)),
        ],
        out_specs=[
            pl.BlockSpec((tm, d), row),
            pl.BlockSpec((tm * ROW_TILE, LANES), row),
            pl.BlockSpec((e, tm), lambda i: (0, i)),
        ],
        out_shape=[
            jax.ShapeDtypeStruct((t, d), F32),
            jax.ShapeDtypeStruct((t * ROW_TILE, LANES), jnp.uint32),
            jax.ShapeDtypeStruct((e, t), F32),
        ],
        compiler_params=_params(("parallel",), 48),
        name="mix_out",
    )(ys, ya, h, wglu, bglu, wout, g, b, wr_t)


def _route_kernel(lt_ref, bias_ref, idx_ref, gate_ref, rank_ref, cnt_ref, run_scr, *, tn):
    e = N_EXPERTS
    ng = N_EXPERT_GROUPS
    gs = e // ng

    @pl.when(pl.program_id(0) == 0)
    def _():
        run_scr[...] = jnp.zeros_like(run_scr)

    scores = jax.nn.sigmoid(lt_ref[...])
    sel = scores + bias_ref[...]
    sel3 = sel.reshape(ng, gs, tn)
    mem = lax.broadcasted_iota(jnp.int32, (ng, gs, tn), 1)
    m1 = jnp.max(sel3, axis=1, keepdims=True)
    i1 = jnp.min(jnp.where(sel3 == m1, mem, gs), axis=1, keepdims=True)
    m2 = jnp.max(jnp.where(mem == i1, -jnp.inf, sel3), axis=1, keepdims=True)
    gscore = (m1 + m2).reshape(ng, tn)

    gio = lax.broadcasted_iota(jnp.int32, (ng, tn), 0)
    gmask = jnp.zeros((ng, tn), F32)
    for _ in range(TOPK_GROUPS):
        mx = jnp.max(gscore, axis=0, keepdims=True)
        gi = jnp.min(jnp.where(gscore == mx, gio, ng), axis=0, keepdims=True)
        hit = gio == gi
        gmask = jnp.where(hit, 1.0, gmask)
        gscore = jnp.where(hit, -jnp.inf, gscore)

    cand = jnp.where(gmask.reshape(ng, 1, tn) > 0.5, sel3, -jnp.inf).reshape(e, tn)
    eio = lax.broadcasted_iota(jnp.int32, (e, tn), 0)
    hits = []
    chosen = jnp.zeros((e, tn), F32)
    for _ in range(TOP_K):
        mx = jnp.max(cand, axis=0, keepdims=True)
        ei = jnp.min(jnp.where(cand == mx, eio, e), axis=0, keepdims=True)
        hit = eio == ei
        hits.append((ei, hit))
        chosen = jnp.where(hit, 1.0, chosen)
        cand = jnp.where(hit, -jnp.inf, cand)

    onehot = chosen.astype(BF16)
    r = lax.broadcasted_iota(jnp.int32, (tn, tn), 0)
    c = lax.broadcasted_iota(jnp.int32, (tn, tn), 1)
    tri = (r < c).astype(BF16)
    rank = _dot(onehot, tri) + run_scr[:, 0:1]
    run_scr[...] = run_scr[...] + jnp.sum(chosen, axis=1, keepdims=True)
    cnt_ref[...] = run_scr[...]

    gates = [jnp.sum(jnp.where(hit, scores, 0.0), axis=0, keepdims=True) for _, hit in hits]
    total = gates[0]
    for gk in gates[1:]:
        total = total + gk
    for kk, (ei, hit) in enumerate(hits):
        idx_ref[kk:kk + 1, :] = ei
        gate_ref[kk:kk + 1, :] = gates[kk] / total * ROUTED_SCALE
        rank_ref[kk:kk + 1, :] = jnp.sum(jnp.where(hit, rank, 0.0), axis=0, keepdims=True).astype(jnp.int32)


def _route(logits_t, bias, tn):
    e, t = logits_t.shape
    col = lambda i: (0, i)
    return pl.pallas_call(
        functools.partial(_route_kernel, tn=tn),
        grid=(t // tn,),
        in_specs=[pl.BlockSpec((e, tn), col), _const_spec((e, 1))],
        out_specs=[
            pl.BlockSpec((TOP_K, tn), col),
            pl.BlockSpec((TOP_K, tn), col),
            pl.BlockSpec((TOP_K, tn), col),
            pl.BlockSpec((e, LANES), lambda i: (0, 0)),
        ],
        out_shape=[
            jax.ShapeDtypeStruct((TOP_K, t), jnp.int32),
            jax.ShapeDtypeStruct((TOP_K, t), F32),
            jax.ShapeDtypeStruct((TOP_K, t), jnp.int32),
            jax.ShapeDtypeStruct((e, LANES), F32),
        ],
        scratch_shapes=[pltpu.VMEM((e, LANES), F32)],
        compiler_params=_params(("arbitrary",), 32),
        name="route",
    )(logits_t, bias)


def _tile_rows(ref, row):
    return ref.at[pl.ds(pl.multiple_of(row * ROW_TILE, ROW_TILE), ROW_TILE), :]


def _dispatch_kernel(ps_ref, pe_ref, tail_ref, pos_ref, x_ref, wgu_ref, wd_ref, xs_ref, sh_ref, zero_scr, sem,
                     *, tm, bm, n_blocks, ff):
    @pl.when(pl.program_id(0) == 0)
    def _():
        zero_scr[...] = jnp.zeros_like(zero_scr)

        def fill(start):
            def per_expert(e, carry):
                first = ps_ref[e]
                n_pad = pe_ref[e] - first
                size = bm // 2
                while size >= 1:
                    done = n_pad & ~(2 * size - 1)

                    @pl.when((n_pad & size) != 0)
                    def _(size=size, done=done):
                        rows = pl.ds(pl.multiple_of((first + done) * ROW_TILE, ROW_TILE), size * ROW_TILE)
                        cp = pltpu.make_async_copy(zero_scr.at[0:size * ROW_TILE, :], xs_ref.at[rows, :], sem)
                        cp.start() if start else cp.wait()

                    size //= 2
                return carry
            lax.fori_loop(0, N_EXPERTS, per_expert, 0)

            def per_block(b, c):
                rows = pl.ds(pl.multiple_of(b * bm * ROW_TILE, bm * ROW_TILE), bm * ROW_TILE)
                cp = pltpu.make_async_copy(zero_scr, xs_ref.at[rows, :], sem)
                cp.start() if start else cp.wait()
                return c
            lax.fori_loop(tail_ref[0], n_blocks, per_block, 0)

        fill(True)
        fill(False)

    def issue(t, carry):
        src = _tile_rows(x_ref, t)
        for kk in range(TOP_K):
            pltpu.make_async_copy(src, _tile_rows(xs_ref, pos_ref[kk, t]), sem).start()
        return carry

    lax.fori_loop(0, tm, issue, 0)
    lo, hi = _unpack_rows(x_ref, 0, tm)
    x = jnp.concatenate([p.astype(BF16) for p in lo + hi], axis=1)
    gu = _dot(x, wgu_ref[...])
    hid = jax.nn.silu(gu[:, :ff]) * gu[:, ff:]
    sh_ref[...] = _dot(hid.astype(BF16), wd_ref[...])
    for _ in range(TOP_K):
        pltpu.make_async_copy(x_ref, xs_ref.at[pl.ds(0, tm * ROW_TILE), :], sem).wait()


def _dispatch(pad_start, pad_end, tail_block, pos, h2p, wgu, wd, n_blocks, bm, tm):
    t = pos.shape[1]
    ff, d = wd.shape
    grid_spec = pltpu.PrefetchScalarGridSpec(
        num_scalar_prefetch=3,
        grid=(t // tm,),
        in_specs=[
            pl.BlockSpec((TOP_K, tm), lambda i, ps, pe, tl: (0, i), memory_space=pltpu.SMEM),
            pl.BlockSpec((tm * ROW_TILE, LANES), lambda i, ps, pe, tl: (i, 0)),
            _const_spec((d, 2 * ff)), _const_spec((ff, d)),
        ],
        out_specs=[pl.BlockSpec(memory_space=pl.ANY), pl.BlockSpec((tm, d), lambda i, ps, pe, tl: (i, 0))],
        scratch_shapes=[pltpu.VMEM((bm * ROW_TILE, LANES), jnp.uint32), pltpu.SemaphoreType.DMA(())],
    )
    return pl.pallas_call(
        functools.partial(_dispatch_kernel, tm=tm, bm=bm, n_blocks=n_blocks, ff=ff),
        grid_spec=grid_spec,
        out_shape=[jax.ShapeDtypeStruct((n_blocks * bm * ROW_TILE, LANES), jnp.uint32),
                   jax.ShapeDtypeStruct((t, d), F32)],
        compiler_params=_params(("arbitrary",), 40),
        name="dispatch",
    )(pad_start, pad_end, tail_block, pos, h2p, wgu, wd)


def _moe_kernel(be_ref, bv_ref, bf_ref, nx_ref, sl_ref, x_ref, wg_hbm, wu_hbm, wd_hbm, y_ref,
                wg_f, wu_f, wd_f, wg_s, wu_s, wd_s, sem, *, bm):
    b = pl.program_id(0)

    def weights(e, slot, start):
        for i, (src, dst) in enumerate(((wg_hbm, wg_f), (wu_hbm, wu_f), (wd_hbm, wd_f))):
            cp = pltpu.make_async_copy(src.at[e], dst.at[slot], sem.at[slot, i])
            cp.start() if start else cp.wait()

    @pl.when(bf_ref[b] == 1)
    def _():
        slot = sl_ref[b]

        @pl.when(b == 0)
        def _():
            weights(be_ref[0], slot, True)

        weights(be_ref[b], slot, False)

        @pl.when(nx_ref[b] >= 0)
        def _():
            weights(nx_ref[b], 1 - slot, True)

        wg_s[...] = wg_f[slot].astype(BF16)
        wu_s[...] = wu_f[slot].astype(BF16)
        wd_s[...] = wd_f[slot].astype(BF16)

    @pl.when(bv_ref[b] == 1)
    def _():
        lo, hi = _unpack_rows(x_ref, 0, bm)
        x = jnp.concatenate([p.astype(BF16) for p in lo + hi], axis=1)
        hid = jax.nn.silu(_dot(x, wg_s[...])) * _dot(x, wu_s[...])
        _pack_rows(_dot(hid.astype(BF16), wd_s[...]), y_ref)

    @pl.when(bv_ref[b] == 0)
    def _():
        y_ref[...] = jnp.zeros_like(y_ref)


def _moe_grouped(block_e, block_valid, block_first, next_e, slot_of, xs, wg, wu, wd, bm):
    rows = xs.shape[0] // ROW_TILE
    d, ff = wg.shape[1], wg.shape[2]
    n_blocks = rows // bm
    blk = lambda b, *_: (b, 0)
    grid_spec = pltpu.PrefetchScalarGridSpec(
        num_scalar_prefetch=5,
        grid=(n_blocks,),
        in_specs=[
            pl.BlockSpec((bm * ROW_TILE, LANES), blk),
            pl.BlockSpec(memory_space=pl.ANY), pl.BlockSpec(memory_space=pl.ANY), pl.BlockSpec(memory_space=pl.ANY),
        ],
        out_specs=pl.BlockSpec((bm * ROW_TILE, LANES), blk),
        scratch_shapes=[
            pltpu.VMEM((2, d, ff), F32), pltpu.VMEM((2, d, ff), F32), pltpu.VMEM((2, ff, d), F32),
            pltpu.VMEM((d, ff), BF16), pltpu.VMEM((d, ff), BF16), pltpu.VMEM((ff, d), BF16),
            pltpu.SemaphoreType.DMA((2, 3)),
        ],
    )
    return pl.pallas_call(
        functools.partial(_moe_kernel, bm=bm),
        grid_spec=grid_spec,
        out_shape=jax.ShapeDtypeStruct(xs.shape, jnp.uint32),
        compiler_params=_params(("arbitrary",), 56),
        name="moe_grouped",
    )(block_e, block_valid, block_first, next_e, slot_of, xs, wg, wu, wd)


def _final_kernel(pos_ref, posn_ref, h2_ref, sh_ref, gate_ref, g_ref, b_ref, ys_ref, o_ref, ybuf, sem, *, tm):
    i = pl.program_id(0)
    slot = i % 2
    tile = TOP_K * tm * ROW_TILE

    def gather(p_ref, dst_slot):
        def issue(t, carry):
            for kk in range(TOP_K):
                dst = ybuf.at[pl.ds(pl.multiple_of(dst_slot * tile + (kk * tm + t) * ROW_TILE, ROW_TILE), ROW_TILE), :]
                pltpu.make_async_copy(_tile_rows(ys_ref, p_ref[kk, t]), dst, sem.at[dst_slot]).start()
            return carry
        lax.fori_loop(0, tm, issue, 0)

    @pl.when(i == 0)
    def _():
        gather(pos_ref, slot)

    @pl.when(i + 1 < pl.num_programs(0))
    def _():
        gather(posn_ref, 1 - slot)

    base = pl.multiple_of(slot * tile, tile)
    for kk in range(TOP_K):
        pltpu.make_async_copy(ys_ref.at[pl.ds(0, tm * ROW_TILE), :],
                              ybuf.at[pl.ds(base + kk * tm * ROW_TILE, tm * ROW_TILE), :], sem.at[slot]).wait()
    gate = gate_ref[...]
    acc = None
    for kk in range(TOP_K):
        lo, hi = _unpack_rows(ybuf, base + kk * tm * ROW_TILE, tm)
        gk = gate[:, kk:kk + 1]
        terms = [gk * p for p in lo + hi]
        acc = terms if acc is None else [a + b for a, b in zip(acc, terms)]
    ffn = jnp.concatenate(acc, axis=1) + sh_ref[...]
    o_ref[...] = _layer_norm(DEEPNORM_ALPHA * h2_ref[...] + ffn, g_ref[...], b_ref[...])


def _final(pos, h2, shared, gate, ys, g, b, tm):
    t, d = h2.shape
    row = lambda i: (i, 0)
    return pl.pallas_call(
        functools.partial(_final_kernel, tm=tm),
        grid=(t // tm,),
        in_specs=[
            pl.BlockSpec((TOP_K, tm), lambda i: (0, i), memory_space=pltpu.SMEM),
            pl.BlockSpec((TOP_K, tm), lambda i: (0, jnp.minimum(i + 1, t // tm - 1)), memory_space=pltpu.SMEM),
            pl.BlockSpec((tm, d), row),
            pl.BlockSpec((tm, d), row),
            pl.BlockSpec((tm, TOP_K), row),
            _const_spec((1, d)), _const_spec((1, d)),
            pl.BlockSpec(memory_space=pl.ANY),
        ],
        out_specs=pl.BlockSpec((tm, d), row),
        out_shape=jax.ShapeDtypeStruct((t, d), F32),
        scratch_shapes=[pltpu.VMEM((2 * TOP_K * tm * ROW_TILE, LANES), jnp.uint32), pltpu.SemaphoreType.DMA((2,))],
        compiler_params=_params(("arbitrary",), 56),
        name="final",
    )(pos, pos, h2, shared, gate, g, b, ys)


def kernel(x, meta_tokens, ln_in_g, ln_in_b, w_in, b_forget, ssm_a_re, ssm_a_im, ssm_log_dt, ssm_b_re, ssm_b_im, ssm_c_re, ssm_c_im, ssm_d, w_glu, b_glu, w_out, ln_mix_g, ln_mix_b, w_router, router_bias, w_gate_exp, w_up_exp, w_down_exp, w_gate_sh, w_up_sh, w_down_sh, ln_ffn_g, ln_ffn_b):
    bsz, seq, d = x.shape
    t = bsz * seq
    ssm_w = w_glu.shape[1]
    heads = b_forget.shape[1]
    att_w = (w_in.shape[2] - ssm_w - heads) // 3
    groups = ssm_a_re.shape[1]
    assert meta_tokens.shape[0] == N_META == S5_CHUNK and att_w == heads * ATT_HEAD_DIM
    nc = seq // S5_CHUNK
    levels = int(math.log2(nc))
    assert 2 ** levels == nc and nc % 8 == 0
    row2 = lambda a: a.reshape(1, -1).astype(F32)

    w_u, w_q, w_k, w_v, w_f = jnp.split(w_in[0], [ssm_w, ssm_w + att_w, ssm_w + 2 * att_w, ssm_w + 3 * att_w], axis=1)
    w_f = jnp.pad(w_f, ((0, 0), (0, LANES - heads)))
    w_in_bf = jnp.concatenate([w_u, w_k, w_f], axis=1).astype(BF16)
    wqvt_bf = jnp.concatenate([w_q, w_v], axis=1).T.astype(BF16)
    bf_pad = jnp.pad(b_forget[0].astype(F32), (0, LANES - heads)).reshape(1, LANES)
    g_in, b_in = row2(ln_in_g), row2(ln_in_b)
    tm = min(256, t)
    h, u, qt, k, vt, lf = _ln_inproj(x.reshape(t, d), g_in, b_in, w_in_bf, wqvt_bf, bf_pad, tm, ssm_w, att_w)
    _, u_m, _, k_m, vt_m, lf_m = _ln_inproj(meta_tokens.astype(F32), g_in, b_in, w_in_bf, wqvt_bf, bf_pad, N_META,
                                            ssm_w, att_w)

    mj, bpj, cpj, a1, a2 = _s5_tables(ssm_a_re[0], ssm_a_im[0], ssm_log_dt[0], ssm_b_re[0], ssm_b_im[0],
                                      ssm_c_re[0], ssm_c_im[0], ssm_d[0], levels)
    y_ssm = _s5(u, u_m, mj, bpj, cpj, a1, a2, bsz, nc, levels)

    tq = min(512, seq)
    nq = seq // tq
    dh = ATT_HEAD_DIM
    ones_rows = 16
    lfm = lf_m[:, :heads] * LOG2E
    c_meta = jnp.cumsum(lfm, axis=0) - jnp.sum(lfm, axis=0, keepdims=True)
    ckm = jnp.full((heads, LANES), -NEG_BIG, F32).at[:, :N_META].set(c_meta.T)
    ckm = jnp.broadcast_to(ckm[:, :, None], (heads, LANES, LANES))
    c_main = jnp.cumsum(lf[:, :heads].reshape(bsz, seq, heads) * LOG2E, axis=1)
    ck = jnp.broadcast_to(c_main.transpose(0, 2, 1)[..., None], (bsz, heads, seq, LANES))
    km = jnp.pad(k_m, ((0, LANES - N_META), (0, 0)))
    va = vt.reshape(heads, dh, bsz, nq, tq).transpose(2, 0, 3, 1, 4)
    va = jnp.concatenate([va, jnp.ones((bsz, heads, nq, ones_rows, tq), BF16)], axis=3)
    vam = jnp.pad(vt_m.reshape(heads, dh, N_META), ((0, 0), (0, 0), (0, LANES - N_META)))
    vam = jnp.concatenate([vam, jnp.ones((heads, ones_rows, LANES), BF16)], axis=1)
    y_att = _fox(qt, k, va, ck, km, vam, ckm, bsz, seq, heads, tq)

    h2, h2p, logits_t = _mix_out(
        y_ssm, y_att, h, w_glu[0].astype(BF16), row2(b_glu[0]), w_out[0].astype(BF16),
        row2(ln_mix_g[0]), row2(ln_mix_b[0]), w_router[0].T.astype(BF16), tm)

    tn = min(512, t)
    idx_t, gate_t, rank_t, counts = _route(logits_t, router_bias[0].astype(F32).reshape(N_EXPERTS, 1), tn)

    bm = 256
    counts = counts[:, 0].astype(jnp.int32)
    pcounts = (counts + bm - 1) // bm * bm
    pends = jnp.cumsum(pcounts)
    pstarts = pends - pcounts
    start_of = jnp.sum(jnp.where(idx_t[:, :, None] == jnp.arange(N_EXPERTS, dtype=jnp.int32), pstarts, 0), axis=-1)
    pos = (start_of + rank_t).astype(jnp.int32)
    n_blocks = t * TOP_K // bm + N_EXPERTS
    bstart = jnp.arange(n_blocks, dtype=jnp.int32) * bm
    block_e = jnp.minimum(jnp.sum(pends[None, :] <= bstart[:, None], axis=1), N_EXPERTS - 1).astype(jnp.int32)
    block_valid = (bstart < pends[-1]).astype(jnp.int32)
    block_first = block_valid * jnp.concatenate([jnp.ones((1,), jnp.int32),
                                                 (block_e[1:] != block_e[:-1]).astype(jnp.int32)])
    slot_of = ((jnp.cumsum(block_first) - 1) % 2).astype(jnp.int32)
    n_valid = pends[-1] // bm
    nxt = jnp.arange(n_blocks, dtype=jnp.int32) + pcounts[block_e] // bm
    next_e = jnp.where(nxt < n_valid, block_e[jnp.minimum(nxt, n_blocks - 1)], -1).astype(jnp.int32)
    wgu_sh = jnp.concatenate([w_gate_sh[0], w_up_sh[0]], axis=1).astype(BF16)
    xs, shared = _dispatch((pstarts + counts).astype(jnp.int32), pends.astype(jnp.int32),
                           (pends[-1:] // bm).astype(jnp.int32), pos, h2p, wgu_sh, w_down_sh[0].astype(BF16),
                           n_blocks, bm, tm)
    ys = _moe_grouped(block_e, block_valid, block_first, next_e, slot_of, xs,
                      w_gate_exp[0], w_up_exp[0], w_down_exp[0], bm)

    out = _final(pos, h2, shared, gate_t.T, ys, row2(ln_ffn_g[0]), row2(ln_ffn_b[0]), tm)
    return out.reshape(bsz, seq, d)
```

```python
import functools
import math

import jax
import jax.numpy as jnp
from jax import lax
from jax.experimental import pallas as pl
from jax.experimental.pallas import tpu as pltpu

N_META = 16
SSM_GROUP_CH = 16
SSM_STATE = 64
ATT_HEAD_DIM = 128
N_EXPERTS = 64
TOP_K = 8
N_EXPERT_GROUPS = 8
TOPK_GROUPS = 4
ROUTED_SCALE = 2.5
LN_EPS = 1e-5
DEPTH = 1
DEEPNORM_ALPHA = (2 * DEPTH) ** 0.25

S5_CHUNK = 16
FOX_SLOTS = 4
LANES = 128
ROW_TILE = 8
NEG_BIG = -1e30
LOG2E = 1.4426950408889634

F32 = jnp.float32
BF16 = jnp.bfloat16
HIGHEST = lax.Precision.HIGHEST


def _dot(a, b):
    return jnp.dot(a, b, preferred_element_type=F32)


def _dot_nt(a, b):
    return lax.dot_general(a, b, (((1,), (1,)), ((), ())), preferred_element_type=F32)


def _layer_norm(x, g, b):
    mu = jnp.mean(x, axis=-1, keepdims=True)
    xc = x - mu
    var = jnp.mean(xc * xc, axis=-1, keepdims=True)
    return xc * lax.rsqrt(var + LN_EPS) * g + b


def _pack_rows(x, o_ref):
    m, w = x.shape
    half = w // 2
    assert half == ROW_TILE * LANES
    bits = lax.bitcast_convert_type(x.astype(BF16).astype(F32), jnp.uint32)
    packed = bits[:, half:] | (bits[:, :half] >> 16)
    for s in range(ROW_TILE):
        o_ref[pl.ds(s, m, stride=ROW_TILE), :] = packed[:, s * LANES:(s + 1) * LANES]


def _unpack_rows(x_ref, base, m):
    lo, hi = [], []
    for s in range(ROW_TILE):
        w = x_ref[pl.ds(base + s, m, stride=ROW_TILE), :]
        lo.append(lax.bitcast_convert_type(w << 16, F32))
        hi.append(lax.bitcast_convert_type(w & jnp.uint32(0xFFFF0000), F32))
    return lo, hi


def _params(sem, vmem_mb):
    return pltpu.CompilerParams(dimension_semantics=sem, vmem_limit_bytes=vmem_mb * 1024 * 1024)


def _const_spec(shape):
    nd = len(shape)
    return pl.BlockSpec(shape, lambda *_: (0,) * nd, pipeline_mode=pl.Buffered(1))


def _ln_inproj_kernel(x_ref, g_ref, b_ref, w_ref, wqvt_ref, bf_ref, h_ref, u_ref, qt_ref, k_ref, vt_ref, lf_ref,
                      *, ssm_w, att_w):
    h = _layer_norm(x_ref[...], g_ref[...], b_ref[...])
    h_ref[...] = h
    hb = h.astype(BF16)
    o = 0
    u_ref[...] = _dot(hb, w_ref[:, o:o + ssm_w])
    o += ssm_w
    k_ref[...] = _dot(hb, w_ref[:, o:o + att_w]).astype(BF16)
    o += att_w
    qvt = _dot_nt(wqvt_ref[...], hb)
    qt_ref[...] = (qvt[:att_w] * (LOG2E * ATT_HEAD_DIM ** -0.5)).astype(BF16)
    vt_ref[...] = qvt[att_w:].astype(BF16)
    f = _dot(hb, w_ref[:, o:o + LANES]) + bf_ref[...]
    lf_ref[...] = jnp.minimum(f, 0.0) - jnp.log(1.0 + jnp.exp(-jnp.abs(f)))


def _ln_inproj(x2d, g, b, w_bf, wqvt_bf, bf_pad, tm, ssm_w, att_w):
    t, d = x2d.shape
    wcols = w_bf.shape[1]
    row = lambda i: (i, 0)
    col = lambda i: (0, i)
    return pl.pallas_call(
        functools.partial(_ln_inproj_kernel, ssm_w=ssm_w, att_w=att_w),
        grid=(t // tm,),
        in_specs=[
            pl.BlockSpec((tm, d), row),
            _const_spec((1, d)), _const_spec((1, d)),
            _const_spec((d, wcols)), _const_spec((2 * att_w, d)), _const_spec((1, LANES)),
        ],
        out_specs=[
            pl.BlockSpec((tm, d), row),
            pl.BlockSpec((tm, ssm_w), row),
            pl.BlockSpec((att_w, tm), col),
            pl.BlockSpec((tm, att_w), row),
            pl.BlockSpec((att_w, tm), col),
            pl.BlockSpec((tm, LANES), row),
        ],
        out_shape=[
            jax.ShapeDtypeStruct((t, d), F32),
            jax.ShapeDtypeStruct((t, ssm_w), F32),
            jax.ShapeDtypeStruct((att_w, t), BF16),
            jax.ShapeDtypeStruct((t, att_w), BF16),
            jax.ShapeDtypeStruct((att_w, t), BF16),
            jax.ShapeDtypeStruct((t, LANES), F32),
        ],
        compiler_params=_params(("parallel",), 56),
        name="ln_inproj",
    )(x2d, g, b, w_bf, wqvt_bf, bf_pad)


def _s5_tables(a_re, a_im, log_dt, b_re, b_im, c_re, c_im, d_skip, levels):
    c = S5_CHUNK
    g, p = a_re.shape
    hc = SSM_GROUP_CH
    lam = lax.complex(a_re.astype(F32), a_im.astype(F32))
    dt = jnp.exp(log_dt.astype(F32))[:, None]
    ldt = lam * dt
    abar = jnp.exp(ldt)
    bbar = ((abar - 1.0) / lam)[..., None] * lax.complex(b_re.astype(F32), b_im.astype(F32))
    cc = lax.complex(c_re.astype(F32), c_im.astype(F32))
    tau = jnp.arange(c + 1, dtype=F32)
    apow = jnp.exp(ldt[:, None, :] * tau[None, :, None])

    ca = cc[:, None, :, :] * apow[:, :c, None, :]
    taps = (jnp.einsum('gthp,gpk->gthk', ca.real, bbar.real, precision=HIGHEST)
            - jnp.einsum('gthp,gpk->gthk', ca.imag, bbar.imag, precision=HIGHEST))
    tt = jnp.arange(c)[:, None]
    ss = jnp.arange(c)[None, :]
    lag_is = ((tt - ss)[None, :, :] == jnp.arange(c)[:, None, None]).astype(F32)
    m = jnp.einsum('gvhk,vts->gskth', taps, lag_is, precision=HIGHEST)
    eye_t = jnp.eye(c, dtype=F32)[None, :, None, :, None]
    eye_h = jnp.eye(hc, dtype=F32)[None, None, :, None, :]
    m = m + eye_t * eye_h * d_skip.astype(F32)[:, None, None, None, :]
    m_t = m.reshape(g, c * hc, c * hc)

    bp = apow[:, c - 1 - jnp.arange(c), :][:, :, :, None] * bbar[:, None, :, :]
    bp = bp.transpose(0, 1, 3, 2).reshape(g, c * hc, p)
    bp_t = jnp.concatenate([bp.real, bp.imag], axis=-1)

    cp = cc[:, None, :, :] * apow[:, 1:c + 1, None, :]
    cp = cp.reshape(g, c * hc, p).transpose(0, 2, 1)
    cp_t = jnp.concatenate([cp.real, -cp.imag], axis=1)

    steps = (c * (2.0 ** jnp.arange(levels, dtype=F32)))
    alev = jnp.exp(ldt[:, None, :] * steps[None, :, None])

    gl = LANES // hc
    nj = g // gl
    m_rows = m_t.reshape(nj, gl, c, hc, c * hc).transpose(0, 2, 1, 3, 4).reshape(nj, c * LANES, c * hc)
    bp_rows = bp_t.reshape(nj, gl, c, hc, 2 * p).transpose(0, 2, 1, 3, 4).reshape(nj, c * LANES, 2 * p)
    cp_rows = cp_t.reshape(nj, gl, 2, p, c * hc).transpose(0, 2, 1, 3, 4).reshape(nj, 2 * gl * p, c * hc)

    def selector(n_outer, inner):
        cols, wcols = n_outer * inner, n_outer * gl * inner
        src, dst = jnp.arange(cols)[:, None], jnp.arange(wcols)[None, :]
        return ((src // inner == dst // (gl * inner)) & (src % inner == dst % inner)).astype(BF16)

    al = alev.reshape(nj, gl, levels, p).transpose(0, 2, 1, 3).reshape(nj, levels, gl * p)
    pad = ((0, 0), (0, 16 - levels), (0, 0))
    a1 = jnp.pad(jnp.concatenate([al.real, al.real], axis=-1), pad)
    a2 = jnp.pad(jnp.concatenate([-al.imag, al.imag], axis=-1), pad)
    return (m_rows.astype(BF16), bp_rows.astype(BF16), cp_rows.astype(BF16),
            selector(c, hc), selector(2, p), a1, a2)


def _shift_rows(x, sh):
    n = x.shape[0]
    if sh % 8 == 0:
        return jnp.concatenate([jnp.zeros((sh, x.shape[1]), x.dtype), x[:n - sh]], axis=0)
    rolled = pltpu.roll(x, sh, axis=0)
    rows = lax.broadcasted_iota(jnp.int32, x.shape, 0)
    return jnp.where(rows < sh, 0.0, rolled)


def _cmul(a1, a2, x):
    return a1 * x + a2 * pltpu.roll(x, x.shape[1] // 2, axis=1)


def _spread_block_diag(rows_ref, sel_ref, out_scr, row_span, col_span):
    gl = LANES // SSM_GROUP_CH
    rows = rows_ref[0]
    n_rows, width = out_scr.shape
    step = 2 * LANES
    r_grp = (lax.broadcasted_iota(jnp.int32, (n_rows, step), 0) >> (row_span.bit_length() - 1)) & (gl - 1)
    for c0 in range(0, width, step):
        wide = _dot(rows, sel_ref[:, c0:c0 + step])
        w_grp = ((lax.broadcasted_iota(jnp.int32, (n_rows, step), 1) + c0) >> (col_span.bit_length() - 1)) & (gl - 1)
        out_scr[:, c0:c0 + step] = jnp.where(r_grp == w_grp, wide, 0.0).astype(BF16)


def _s5_kernel(u_ref, um_ref, mrow_ref, bprow_ref, cprow_ref, selm_ref, selb_ref, a1_ref, a2_ref, y_ref,
               mj_scr, bpj_scr, cpj_scr, *, nc, levels):
    c = S5_CHUNK

    @pl.when(pl.program_id(1) == 0)
    def _():
        _spread_block_diag(mrow_ref, selm_ref, mj_scr, SSM_GROUP_CH, SSM_GROUP_CH)
        _spread_block_diag(bprow_ref, selb_ref, bpj_scr, SSM_GROUP_CH, SSM_STATE)
        _spread_block_diag(cprow_ref, selm_ref, cpj_scr, SSM_STATE, SSM_GROUP_CH)

    u = jnp.concatenate([u_ref[pl.ds(s, nc, stride=c), :].astype(BF16) for s in range(c)], axis=1)
    bpj = bpj_scr[...]
    w = _dot(u, bpj)
    um = jnp.concatenate([um_ref[s:s + 1, :] for s in range(c)], axis=1)
    x_meta = _dot(jnp.broadcast_to(um, (8, um.shape[1])).astype(BF16), bpj)
    first = lax.broadcasted_iota(jnp.int32, x_meta.shape, 0) == 0
    inject = jnp.where(first, _cmul(a1_ref[0, 0:1, :], a2_ref[0, 0:1, :], x_meta), 0.0)
    x = jnp.concatenate([w[:8] + inject, w[8:]], axis=0)
    for lvl in range(levels):
        x = x + _cmul(a1_ref[0, lvl:lvl + 1, :], a2_ref[0, lvl:lvl + 1, :], _shift_rows(x, 2 ** lvl))
    xp = _shift_rows(x, 1)
    x_in = jnp.concatenate([xp[:8] + jnp.where(first, x_meta, 0.0), xp[8:]], axis=0).astype(BF16)
    y = jax.nn.gelu(_dot(u, mj_scr[...]) + _dot(x_in, cpj_scr[...]))
    for s in range(c):
        y_ref[pl.ds(s, nc, stride=c), :] = y[:, s * LANES:(s + 1) * LANES]


def _s5(u, u_m, m_rows, bp_rows, cp_rows, sel_m, sel_b, a1, a2, bsz, nc, levels):
    t, ssm_w = u.shape
    nj, cl, cm = m_rows.shape
    sw, cb = cp_rows.shape[1], bp_rows.shape[2]
    nl = a1.shape[1]
    rows = nc * S5_CHUNK
    tab = lambda j, b: (j, 0, 0)
    return pl.pallas_call(
        functools.partial(_s5_kernel, nc=nc, levels=levels),
        grid=(nj, bsz),
        in_specs=[
            pl.BlockSpec((rows, LANES), lambda j, b: (b, j)),
            pl.BlockSpec((S5_CHUNK, LANES), lambda j, b: (0, j)),
            pl.BlockSpec((1, cl, cm), tab),
            pl.BlockSpec((1, cl, cb), tab),
            pl.BlockSpec((1, sw, cm), tab),
            _const_spec((cm, cl)), _const_spec((cb, sw)),
            pl.BlockSpec((1, nl, sw), tab),
            pl.BlockSpec((1, nl, sw), tab),
        ],
        out_specs=pl.BlockSpec((rows, LANES), lambda j, b: (b, j)),
        out_shape=jax.ShapeDtypeStruct((t, ssm_w), F32),
        scratch_shapes=[pltpu.VMEM((cl, cl), BF16), pltpu.VMEM((cl, sw), BF16), pltpu.VMEM((sw, cl), BF16)],
        compiler_params=_params(("parallel", "arbitrary"), 56),
        name="s5_scan",
    )(u, u_m, m_rows, bp_rows, cp_rows, sel_m, sel_b, a1, a2)


def _fox_kernel(qt_ref, k_ref, va_ref, ck_ref, km_ref, vam_ref, ckm_ref, o_ref,
                m_scr, acc_scr, s_scr, bm_scr, p_scr, al_scr, *, tq, dh):
    qi = pl.program_id(2)
    qt = qt_ref[...]
    reps = tq // LANES

    def scores(kb, ckb):
        return _dot(kb, qt) - jnp.concatenate([ckb] * reps, axis=1)

    def put_scores(slot, s):
        s_scr[slot] = s
        bm_scr[slot] = jnp.broadcast_to(jnp.max(s, axis=0, keepdims=True), (8, tq))

    def softmax_update(slot):
        m_prev = m_scr[0:1, :]
        m_new = jnp.maximum(m_prev, bm_scr[slot, 0:1, :])
        m_scr[...] = jnp.broadcast_to(m_new, m_scr.shape)
        return jnp.exp2(m_prev - m_new), jnp.exp2((s_scr[slot] - m_new).astype(BF16))

    s = scores(km_ref[...], ckm_ref[0])
    m0 = jnp.max(s, axis=0, keepdims=True)
    m_scr[...] = jnp.broadcast_to(m0, m_scr.shape)
    acc_scr[...] = _dot(vam_ref[0], jnp.exp2(s - m0).astype(BF16))

    start = pl.multiple_of(qi * tq, tq)
    s = scores(k_ref[pl.ds(start, tq), :], ck_ref[0, 0, pl.ds(start, tq), :])
    key = lax.broadcasted_iota(jnp.int32, (tq, tq), 0)
    qry = lax.broadcasted_iota(jnp.int32, (tq, tq), 1)
    ns = FOX_SLOTS
    put_scores(ns - 1, jnp.where(key <= qry, s, NEG_BIG))
    p_scr[ns - 2] = jnp.zeros((tq, tq), BF16)
    al_scr[ns - 2] = jnp.ones((8, tq), F32)

    def tick(t, u):
        sb, sc = (u - 1) % ns, (u - 2) % ns
        jc = jnp.where(t == 1, qi, jnp.clip(t - 2, 0, qi))
        acc_scr[...] = al_scr[sc, 0:1, :] * acc_scr[...] + _dot(va_ref[0, 0, jc], p_scr[sc])
        alpha, p = softmax_update(sb)
        p_scr[sb] = p
        al_scr[sb] = jnp.broadcast_to(alpha, (8, tq))
        ja = jnp.minimum(t, jnp.maximum(qi - 1, 0))
        start = pl.multiple_of(ja * tq, tq)
        off = jnp.where(t < qi, 0.0, -NEG_BIG)
        put_scores(u, scores(k_ref[pl.ds(start, tq), :], ck_ref[0, 0, pl.ds(start, tq), :] + off))

    def body(i, carry):
        for u in range(ns):
            tick(ns * i + u, u)
        return carry

    n_ticks = (qi + 3) // 2 * 2
    n_full = n_ticks // ns
    lax.fori_loop(0, n_full, body, 0)

    @pl.when(n_ticks % ns != 0)
    def _():
        tick(ns * n_full, 0)
        tick(ns * n_full + 1, 1)
    acc = acc_scr[...]
    o_ref[...] = (acc[:dh] / acc[dh:dh + 1]).T.astype(BF16)


def _fox(qt, k, va, ck, km, vam, ckm, bsz, seq, heads, tq):
    aw, t = qt.shape
    nq = seq // tq
    dh = ATT_HEAD_DIM
    da = va.shape[3]
    return pl.pallas_call(
        functools.partial(_fox_kernel, tq=tq, dh=dh),
        grid=(bsz, heads, nq),
        in_specs=[
            pl.BlockSpec((dh, tq), lambda b, h, i: (h, b * nq + i)),
            pl.BlockSpec((seq, dh), lambda b, h, i: (b, h)),
            pl.BlockSpec((1, 1, nq, da, tq), lambda b, h, i: (b, h, 0, 0, 0)),
            pl.BlockSpec((1, 1, seq, LANES), lambda b, h, i: (b, h, 0, 0)),
            pl.BlockSpec((LANES, dh), lambda b, h, i: (0, h)),
            pl.BlockSpec((1, da, LANES), lambda b, h, i: (h, 0, 0)),
            pl.BlockSpec((1, LANES, LANES), lambda b, h, i: (h, 0, 0)),
        ],
        out_specs=pl.BlockSpec((tq, dh), lambda b, h, i: (b * nq + i, h)),
        out_shape=jax.ShapeDtypeStruct((t, aw), BF16),
        scratch_shapes=[
            pltpu.VMEM((8, tq), F32),
            pltpu.VMEM((da, tq), F32),
            pltpu.VMEM((FOX_SLOTS, tq, tq), F32),
            pltpu.VMEM((FOX_SLOTS, 8, tq), F32),
            pltpu.VMEM((FOX_SLOTS, tq, tq), BF16),
            pltpu.VMEM((FOX_SLOTS, 8, tq), F32),
        ],
        compiler_params=_params(("parallel", "parallel", "arbitrary"), 48),
        name="fox_attention",
    )(qt, k, va, ck, km, vam, ckm)


def _mix_out_kernel(ys_ref, ya_ref, h_ref, wglu_ref, bglu_ref, wout_ref, g_ref, b_ref, wr_ref,
                    h2_ref, h2p_ref, lt_ref, *, ssm_w):
    ys = ys_ref[...]
    z = _dot(ys.astype(BF16), wglu_ref[...]) + bglu_ref[...]
    yg = (ys * jax.nn.sigmoid(z)).astype(BF16)
    mix = _dot(yg, wout_ref[0:ssm_w, :]) + _dot(ya_ref[...], wout_ref[ssm_w:, :])
    h2 = _layer_norm(DEEPNORM_ALPHA * h_ref[...] + mix, g_ref[...], b_ref[...])
    h2_ref[...] = h2
    _pack_rows(h2, h2p_ref)
    lt_ref[...] = _dot_nt(wr_ref[...], h2.astype(BF16))


def _mix_out(ys, ya, h, wglu, bglu, wout, g, b, wr_t, tm):
    t, d = h.shape
    ssm_w = ys.shape[1]
    att_w = ya.shape[1]
    e = wr_t.shape[0]
    row = lambda i: (i, 0)
    return pl.pallas_call(
        functools.partial(_mix_out_kernel, ssm_w=ssm_w),
        grid=(t // tm,),
        in_specs=[
            pl.BlockSpec((tm, ssm_w), row),
            pl.BlockSpec((tm, att_w), row),
            pl.BlockSpec((tm, d), row),
            _const_spec((ssm_w, ssm_w)), _const_spec((1, ssm_w)),
            _const_spec((d, d)), _const_spec((1, d)), _const_spec((1, d)),
            _const_spec((e, d)),
        ],
        out_specs=[
            pl.BlockSpec((tm, d), row),
            pl.BlockSpec((tm * ROW_TILE, LANES), row),
            pl.BlockSpec((e, tm), lambda i: (0, i)),
        ],
        out_shape=[
            jax.ShapeDtypeStruct((t, d), F32),
            jax.ShapeDtypeStruct((t * ROW_TILE, LANES), jnp.uint32),
            jax.ShapeDtypeStruct((e, t), F32),
        ],
        compiler_params=_params(("parallel",), 48),
        name="mix_out",
    )(ys, ya, h, wglu, bglu, wout, g, b, wr_t)


def _route_kernel(lt_ref, bias_ref, idx_ref, gate_ref, rank_ref, cnt_ref, run_scr, *, tn):
    e = N_EXPERTS
    ng = N_EXPERT_GROUPS
    gs = e // ng

    @pl.when(pl.program_id(0) == 0)
    def _():
        run_scr[...] = jnp.zeros_like(run_scr)

    scores = jax.nn.sigmoid(lt_ref[...])
    sel = scores + bias_ref[...]
    sel3 = sel.reshape(ng, gs, tn)
    mem = lax.broadcasted_iota(jnp.int32, (ng, gs, tn), 1)
    m1 = jnp.max(sel3, axis=1, keepdims=True)
    i1 = jnp.min(jnp.where(sel3 == m1, mem, gs), axis=1, keepdims=True)
    m2 = jnp.max(jnp.where(mem == i1, -jnp.inf, sel3), axis=1, keepdims=True)
    gscore = (m1 + m2).reshape(ng, tn)

    gio = lax.broadcasted_iota(jnp.int32, (ng, tn), 0)
    gmask = jnp.zeros((ng, tn), F32)
    for _ in range(TOPK_GROUPS):
        mx = jnp.max(gscore, axis=0, keepdims=True)
        gi = jnp.min(jnp.where(gscore == mx, gio, ng), axis=0, keepdims=True)
        hit = gio == gi
        gmask = jnp.where(hit, 1.0, gmask)
        gscore = jnp.where(hit, -jnp.inf, gscore)

    cand = jnp.where(gmask.reshape(ng, 1, tn) > 0.5, sel3, -jnp.inf).reshape(e, tn)
    eio = lax.broadcasted_iota(jnp.int32, (e, tn), 0)
    hits = []
    chosen = jnp.zeros((e, tn), F32)
    for _ in range(TOP_K):
        mx = jnp.max(cand, axis=0, keepdims=True)
        ei = jnp.min(jnp.where(cand == mx, eio, e), axis=0, keepdims=True)
        hit = eio == ei
        hits.append((ei, hit))
        chosen = jnp.where(hit, 1.0, chosen)
        cand = jnp.where(hit, -jnp.inf, cand)

    onehot = chosen.astype(BF16)
    r = lax.broadcasted_iota(jnp.int32, (tn, tn), 0)
    c = lax.broadcasted_iota(jnp.int32, (tn, tn), 1)
    tri = (r < c).astype(BF16)
    rank = _dot(onehot, tri) + run_scr[:, 0:1]
    run_scr[...] = run_scr[...] + jnp.sum(chosen, axis=1, keepdims=True)
    cnt_ref[...] = run_scr[...]

    gates = [jnp.sum(jnp.where(hit, scores, 0.0), axis=0, keepdims=True) for _, hit in hits]
    total = gates[0]
    for gk in gates[1:]:
        total = total + gk
    for kk, (ei, hit) in enumerate(hits):
        idx_ref[kk:kk + 1, :] = ei
        gate_ref[kk:kk + 1, :] = gates[kk] / total * ROUTED_SCALE
        rank_ref[kk:kk + 1, :] = jnp.sum(jnp.where(hit, rank, 0.0), axis=0, keepdims=True).astype(jnp.int32)


def _route(logits_t, bias, tn):
    e, t = logits_t.shape
    col = lambda i: (0, i)
    return pl.pallas_call(
        functools.partial(_route_kernel, tn=tn),
        grid=(t // tn,),
        in_specs=[pl.BlockSpec((e, tn), col), _const_spec((e, 1))],
        out_specs=[
            pl.BlockSpec((TOP_K, tn), col),
            pl.BlockSpec((TOP_K, tn), col),
            pl.BlockSpec((TOP_K, tn), col),
            pl.BlockSpec((e, LANES), lambda i: (0, 0)),
        ],
        out_shape=[
            jax.ShapeDtypeStruct((TOP_K, t), jnp.int32),
            jax.ShapeDtypeStruct((TOP_K, t), F32),
            jax.ShapeDtypeStruct((TOP_K, t), jnp.int32),
            jax.ShapeDtypeStruct((e, LANES), F32),
        ],
        scratch_shapes=[pltpu.VMEM((e, LANES), F32)],
        compiler_params=_params(("arbitrary",), 32),
        name="route",
    )(logits_t, bias)


def _tile_rows(ref, row):
    return ref.at[pl.ds(pl.multiple_of(row * ROW_TILE, ROW_TILE), ROW_TILE), :]


def _dispatch_kernel(ps_ref, pe_ref, tail_ref, pos_ref, x_ref, wgu_ref, wd_ref, xs_ref, sh_ref, zero_scr, sem,
                     *, tm, bm, n_blocks, ff):
    @pl.when(pl.program_id(0) == 0)
    def _():
        zero_scr[...] = jnp.zeros_like(zero_scr)

        def fill(start):
            def per_expert(e, carry):
                first = ps_ref[e]
                n_pad = pe_ref[e] - first
                size = bm // 2
                while size >= 1:
                    done = n_pad & ~(2 * size - 1)

                    @pl.when((n_pad & size) != 0)
                    def _(size=size, done=done):
                        rows = pl.ds(pl.multiple_of((first + done) * ROW_TILE, ROW_TILE), size * ROW_TILE)
                        cp = pltpu.make_async_copy(zero_scr.at[0:size * ROW_TILE, :], xs_ref.at[rows, :], sem)
                        cp.start() if start else cp.wait()

                    size //= 2
                return carry
            lax.fori_loop(0, N_EXPERTS, per_expert, 0)

            def per_block(b, c):
                rows = pl.ds(pl.multiple_of(b * bm * ROW_TILE, bm * ROW_TILE), bm * ROW_TILE)
                cp = pltpu.make_async_copy(zero_scr, xs_ref.at[rows, :], sem)
                cp.start() if start else cp.wait()
                return c
            lax.fori_loop(tail_ref[0], n_blocks, per_block, 0)

        fill(True)
        fill(False)

    def issue(t, carry):
        src = _tile_rows(x_ref, t)
        for kk in range(TOP_K):
            pltpu.make_async_copy(src, _tile_rows(xs_ref, pos_ref[kk, t]), sem).start()
        return carry

    lax.fori_loop(0, tm, issue, 0)
    lo, hi = _unpack_rows(x_ref, 0, tm)
    x = jnp.concatenate([p.astype(BF16) for p in lo + hi], axis=1)
    gu = _dot(x, wgu_ref[...])
    hid = jax.nn.silu(gu[:, :ff]) * gu[:, ff:]
    sh_ref[...] = _dot(hid.astype(BF16), wd_ref[...])
    for _ in range(TOP_K):
        pltpu.make_async_copy(x_ref, xs_ref.at[pl.ds(0, tm * ROW_TILE), :], sem).wait()


def _dispatch(pad_start, pad_end, tail_block, pos, h2p, wgu, wd, n_blocks, bm, tm):
    t = pos.shape[1]
    ff, d = wd.shape
    grid_spec = pltpu.PrefetchScalarGridSpec(
        num_scalar_prefetch=3,
        grid=(t // tm,),
        in_specs=[
            pl.BlockSpec((TOP_K, tm), lambda i, ps, pe, tl: (0, i), memory_space=pltpu.SMEM),
            pl.BlockSpec((tm * ROW_TILE, LANES), lambda i, ps, pe, tl: (i, 0)),
            _const_spec((d, 2 * ff)), _const_spec((ff, d)),
        ],
        out_specs=[pl.BlockSpec(memory_space=pl.ANY), pl.BlockSpec((tm, d), lambda i, ps, pe, tl: (i, 0))],
        scratch_shapes=[pltpu.VMEM((bm * ROW_TILE, LANES), jnp.uint32), pltpu.SemaphoreType.DMA(())],
    )
    return pl.pallas_call(
        functools.partial(_dispatch_kernel, tm=tm, bm=bm, n_blocks=n_blocks, ff=ff),
        grid_spec=grid_spec,
        out_shape=[jax.ShapeDtypeStruct((n_blocks * bm * ROW_TILE, LANES), jnp.uint32),
                   jax.ShapeDtypeStruct((t, d), F32)],
        compiler_params=_params(("arbitrary",), 40),
        name="dispatch",
    )(pad_start, pad_end, tail_block, pos, h2p, wgu, wd)


def _moe_kernel(be_ref, bv_ref, bf_ref, nx_ref, sl_ref, x_ref, wg_hbm, wu_hbm, wd_hbm, y_ref,
                wg_f, wu_f, wd_f, wg_s, wu_s, wd_s, sem, *, bm):
    b = pl.program_id(0)

    def weights(e, slot, start):
        for i, (src, dst) in enumerate(((wg_hbm, wg_f), (wu_hbm, wu_f), (wd_hbm, wd_f))):
            cp = pltpu.make_async_copy(src.at[e], dst.at[slot], sem.at[slot, i])
            cp.start() if start else cp.wait()

    @pl.when(bf_ref[b] == 1)
    def _():
        slot = sl_ref[b]

        @pl.when(b == 0)
        def _():
            weights(be_ref[0], slot, True)

        weights(be_ref[b], slot, False)

        @pl.when(nx_ref[b] >= 0)
        def _():
            weights(nx_ref[b], 1 - slot, True)

        wg_s[...] = wg_f[slot].astype(BF16)
        wu_s[...] = wu_f[slot].astype(BF16)
        wd_s[...] = wd_f[slot].astype(BF16)

    @pl.when(bv_ref[b] == 1)
    def _():
        lo, hi = _unpack_rows(x_ref, 0, bm)
        x = jnp.concatenate([p.astype(BF16) for p in lo + hi], axis=1)
        hid = jax.nn.silu(_dot(x, wg_s[...])) * _dot(x, wu_s[...])
        _pack_rows(_dot(hid.astype(BF16), wd_s[...]), y_ref)

    @pl.when(bv_ref[b] == 0)
    def _():
        y_ref[...] = jnp.zeros_like(y_ref)


def _moe_grouped(block_e, block_valid, block_first, next_e, slot_of, xs, wg, wu, wd, bm):
    rows = xs.shape[0] // ROW_TILE
    d, ff = wg.shape[1], wg.shape[2]
    n_blocks = rows // bm
    blk = lambda b, *_: (b, 0)
    grid_spec = pltpu.PrefetchScalarGridSpec(
        num_scalar_prefetch=5,
        grid=(n_blocks,),
        in_specs=[
            pl.BlockSpec((bm * ROW_TILE, LANES), blk),
            pl.BlockSpec(memory_space=pl.ANY), pl.BlockSpec(memory_space=pl.ANY), pl.BlockSpec(memory_space=pl.ANY),
        ],
        out_specs=pl.BlockSpec((bm * ROW_TILE, LANES), blk),
        scratch_shapes=[
            pltpu.VMEM((2, d, ff), F32), pltpu.VMEM((2, d, ff), F32), pltpu.VMEM((2, ff, d), F32),
            pltpu.VMEM((d, ff), BF16), pltpu.VMEM((d, ff), BF16), pltpu.VMEM((ff, d), BF16),
            pltpu.SemaphoreType.DMA((2, 3)),
        ],
    )
    return pl.pallas_call(
        functools.partial(_moe_kernel, bm=bm),
        grid_spec=grid_spec,
        out_shape=jax.ShapeDtypeStruct(xs.shape, jnp.uint32),
        compiler_params=_params(("arbitrary",), 56),
        name="moe_grouped",
    )(block_e, block_valid, block_first, next_e, slot_of, xs, wg, wu, wd)


def _final_kernel(pos_ref, posn_ref, h2_ref, sh_ref, gate_ref, g_ref, b_ref, ys_ref, o_ref, ybuf, sem, *, tm):
    i = pl.program_id(0)
    slot = i % 2
    tile = TOP_K * tm * ROW_TILE

    def gather(p_ref, dst_slot):
        def issue(t, carry):
            for kk in range(TOP_K):
                dst = ybuf.at[pl.ds(pl.multiple_of(dst_slot * tile + (kk * tm + t) * ROW_TILE, ROW_TILE), ROW_TILE), :]
                pltpu.make_async_copy(_tile_rows(ys_ref, p_ref[kk, t]), dst, sem.at[dst_slot]).start()
            return carry
        lax.fori_loop(0, tm, issue, 0)

    @pl.when(i == 0)
    def _():
        gather(pos_ref, slot)

    @pl.when(i + 1 < pl.num_programs(0))
    def _():
        gather(posn_ref, 1 - slot)

    base = pl.multiple_of(slot * tile, tile)
    for kk in range(TOP_K):
        pltpu.make_async_copy(ys_ref.at[pl.ds(0, tm * ROW_TILE), :],
                              ybuf.at[pl.ds(base + kk * tm * ROW_TILE, tm * ROW_TILE), :], sem.at[slot]).wait()
    gate = gate_ref[...]
    acc = None
    for kk in range(TOP_K):
        lo, hi = _unpack_rows(ybuf, base + kk * tm * ROW_TILE, tm)
        gk = gate[:, kk:kk + 1]
        terms = [gk * p for p in lo + hi]
        acc = terms if acc is None else [a + b for a, b in zip(acc, terms)]
    ffn = jnp.concatenate(acc, axis=1) + sh_ref[...]
    o_ref[...] = _layer_norm(DEEPNORM_ALPHA * h2_ref[...] + ffn, g_ref[...], b_ref[...])


def _final(pos, h2, shared, gate, ys, g, b, tm):
    t, d = h2.shape
    row = lambda i: (i, 0)
    return pl.pallas_call(
        functools.partial(_final_kernel, tm=tm),
        grid=(t // tm,),
        in_specs=[
            pl.BlockSpec((TOP_K, tm), lambda i: (0, i), memory_space=pltpu.SMEM),
            pl.BlockSpec((TOP_K, tm), lambda i: (0, jnp.minimum(i + 1, t // tm - 1)), memory_space=pltpu.SMEM),
            pl.BlockSpec((tm, d), row),
            pl.BlockSpec((tm, d), row),
            pl.BlockSpec((tm, TOP_K), row),
            _const_spec((1, d)), _const_spec((1, d)),
            pl.BlockSpec(memory_space=pl.ANY),
        ],
        out_specs=pl.BlockSpec((tm, d), row),
        out_shape=jax.ShapeDtypeStruct((t, d), F32),
        scratch_shapes=[pltpu.VMEM((2 * TOP_K * tm * ROW_TILE, LANES), jnp.uint32), pltpu.SemaphoreType.DMA((2,))],
        compiler_params=_params(("arbitrary",), 56),
        name="final",
    )(pos, pos, h2, shared, gate, g, b, ys)


def kernel(x, meta_tokens, ln_in_g, ln_in_b, w_in, b_forget, ssm_a_re, ssm_a_im, ssm_log_dt, ssm_b_re, ssm_b_im, ssm_c_re, ssm_c_im, ssm_d, w_glu, b_glu, w_out, ln_mix_g, ln_mix_b, w_router, router_bias, w_gate_exp, w_up_exp, w_down_exp, w_gate_sh, w_up_sh, w_down_sh, ln_ffn_g, ln_ffn_b):
    bsz, seq, d = x.shape
    t = bsz * seq
    ssm_w = w_glu.shape[1]
    heads = b_forget.shape[1]
    att_w = (w_in.shape[2] - ssm_w - heads) // 3
    groups = ssm_a_re.shape[1]
    assert meta_tokens.shape[0] == N_META == S5_CHUNK and att_w == heads * ATT_HEAD_DIM
    nc = seq // S5_CHUNK
    levels = int(math.log2(nc))
    assert 2 ** levels == nc and nc % 8 == 0
    row2 = lambda a: a.reshape(1, -1).astype(F32)

    w_u, w_q, w_k, w_v, w_f = jnp.split(w_in[0], [ssm_w, ssm_w + att_w, ssm_w + 2 * att_w, ssm_w + 3 * att_w], axis=1)
    w_f = jnp.pad(w_f, ((0, 0), (0, LANES - heads)))
    w_in_bf = jnp.concatenate([w_u, w_k, w_f], axis=1).astype(BF16)
    wqvt_bf = jnp.concatenate([w_q, w_v], axis=1).T.astype(BF16)
    bf_pad = jnp.pad(b_forget[0].astype(F32), (0, LANES - heads)).reshape(1, LANES)
    g_in, b_in = row2(ln_in_g), row2(ln_in_b)
    tm = min(256, t)
    h, u, qt, k, vt, lf = _ln_inproj(x.reshape(t, d), g_in, b_in, w_in_bf, wqvt_bf, bf_pad, tm, ssm_w, att_w)
    _, u_m, _, k_m, vt_m, lf_m = _ln_inproj(meta_tokens.astype(F32), g_in, b_in, w_in_bf, wqvt_bf, bf_pad, N_META,
                                            ssm_w, att_w)

    s5_tabs = _s5_tables(ssm_a_re[0], ssm_a_im[0], ssm_log_dt[0], ssm_b_re[0], ssm_b_im[0],
                         ssm_c_re[0], ssm_c_im[0], ssm_d[0], levels)
    y_ssm = _s5(u, u_m, *s5_tabs, bsz, nc, levels)

    tq = min(512, seq)
    nq = seq // tq
    dh = ATT_HEAD_DIM
    ones_rows = 16
    lfm = lf_m[:, :heads] * LOG2E
    c_meta = jnp.cumsum(lfm, axis=0) - jnp.sum(lfm, axis=0, keepdims=True)
    ckm = jnp.full((heads, LANES), -NEG_BIG, F32).at[:, :N_META].set(c_meta.T)
    ckm = jnp.broadcast_to(ckm[:, :, None], (heads, LANES, LANES))
    c_main = jnp.cumsum(lf[:, :heads].reshape(bsz, seq, heads) * LOG2E, axis=1)
    ck = jnp.broadcast_to(c_main.transpose(0, 2, 1)[..., None], (bsz, heads, seq, LANES))
    km = jnp.pad(k_m, ((0, LANES - N_META), (0, 0)))
    va = vt.reshape(heads, dh, bsz, nq, tq).transpose(2, 0, 3, 1, 4)
    va = jnp.concatenate([va, jnp.ones((bsz, heads, nq, ones_rows, tq), BF16)], axis=3)
    vam = jnp.pad(vt_m.reshape(heads, dh, N_META), ((0, 0), (0, 0), (0, LANES - N_META)))
    vam = jnp.concatenate([vam, jnp.ones((heads, ones_rows, LANES), BF16)], axis=1)
    y_att = _fox(qt, k, va, ck, km, vam, ckm, bsz, seq, heads, tq)

    h2, h2p, logits_t = _mix_out(
        y_ssm, y_att, h, w_glu[0].astype(BF16), row2(b_glu[0]), w_out[0].astype(BF16),
        row2(ln_mix_g[0]), row2(ln_mix_b[0]), w_router[0].T.astype(BF16), tm)

    tn = min(512, t)
    idx_t, gate_t, rank_t, counts = _route(logits_t, router_bias[0].astype(F32).reshape(N_EXPERTS, 1), tn)

    bm = 256
    counts = counts[:, 0].astype(jnp.int32)
    pcounts = (counts + bm - 1) // bm * bm
    pends = jnp.cumsum(pcounts)
    pstarts = pends - pcounts
    start_of = jnp.sum(jnp.where(idx_t[:, :, None] == jnp.arange(N_EXPERTS, dtype=jnp.int32), pstarts, 0), axis=-1)
    pos = (start_of + rank_t).astype(jnp.int32)
    n_blocks = t * TOP_K // bm + N_EXPERTS
    bstart = jnp.arange(n_blocks, dtype=jnp.int32) * bm
    block_e = jnp.minimum(jnp.sum(pends[None, :] <= bstart[:, None], axis=1), N_EXPERTS - 1).astype(jnp.int32)
    block_valid = (bstart < pends[-1]).astype(jnp.int32)
    block_first = block_valid * jnp.concatenate([jnp.ones((1,), jnp.int32),
                                                 (block_e[1:] != block_e[:-1]).astype(jnp.int32)])
    slot_of = ((jnp.cumsum(block_first) - 1) % 2).astype(jnp.int32)
    n_valid = pends[-1] // bm
    nxt = jnp.arange(n_blocks, dtype=jnp.int32) + pcounts[block_e] // bm
    next_e = jnp.where(nxt < n_valid, block_e[jnp.minimum(nxt, n_blocks - 1)], -1).astype(jnp.int32)
    wgu_sh = jnp.concatenate([w_gate_sh[0], w_up_sh[0]], axis=1).astype(BF16)
    xs, shared = _dispatch((pstarts + counts).astype(jnp.int32), pends.astype(jnp.int32),
                           (pends[-1:] // bm).astype(jnp.int32), pos, h2p, wgu_sh, w_down_sh[0].astype(BF16),
                           n_blocks, bm, tm)
    ys = _moe_grouped(block_e, block_valid, block_first, next_e, slot_of, xs,
                      w_gate_exp[0], w_up_exp[0], w_down_exp[0], bm)

    out = _final(pos, h2, shared, gate_t.T, ys, row2(ln_ffn_g[0]), row2(ln_ffn_b[0]), tm)
    return out.reshape(bsz, seq, d)
```

```python
import functools
import math

import jax
import jax.numpy as jnp
from jax import lax
from jax.experimental import pallas as pl
from jax.experimental.pallas import tpu as pltpu

N_META = 16
SSM_GROUP_CH = 16
SSM_STATE = 64
ATT_HEAD_DIM = 128
N_EXPERTS = 64
TOP_K = 8
N_EXPERT_GROUPS = 8
TOPK_GROUPS = 4
ROUTED_SCALE = 2.5
LN_EPS = 1e-5
DEPTH = 1
DEEPNORM_ALPHA = (2 * DEPTH) ** 0.25

S5_CHUNK = 16
FOX_SLOTS = 4
LANES = 128
ROW_TILE = 8
ONES_ROWS = 16
NEG_BIG = -1e30
LOG2E = 1.4426950408889634

F32 = jnp.float32
BF16 = jnp.bfloat16
HIGHEST = lax.Precision.HIGHEST


def _dot(a, b):
    return jnp.dot(a, b, preferred_element_type=F32)


def _dot_nt(a, b):
    return lax.dot_general(a, b, (((1,), (1,)), ((), ())), preferred_element_type=F32)


def _layer_norm(x, g, b):
    mu = jnp.mean(x, axis=-1, keepdims=True)
    xc = x - mu
    var = jnp.mean(xc * xc, axis=-1, keepdims=True)
    return xc * lax.rsqrt(var + LN_EPS) * g + b


def _pack_rows(x, o_ref):
    m, w = x.shape
    half = w // 2
    assert half == ROW_TILE * LANES
    bits = lax.bitcast_convert_type(x.astype(BF16).astype(F32), jnp.uint32)
    packed = bits[:, half:] | (bits[:, :half] >> 16)
    for s in range(ROW_TILE):
        o_ref[pl.ds(s, m, stride=ROW_TILE), :] = packed[:, s * LANES:(s + 1) * LANES]


def _unpack_rows(x_ref, base, m):
    lo, hi = [], []
    for s in range(ROW_TILE):
        w = x_ref[pl.ds(base + s, m, stride=ROW_TILE), :]
        lo.append(lax.bitcast_convert_type(w << 16, F32))
        hi.append(lax.bitcast_convert_type(w & jnp.uint32(0xFFFF0000), F32))
    return lo, hi


def _params(sem, vmem_mb):
    return pltpu.CompilerParams(dimension_semantics=sem, vmem_limit_bytes=vmem_mb * 1024 * 1024)


def _const_spec(shape):
    nd = len(shape)
    return pl.BlockSpec(shape, lambda *_: (0,) * nd, pipeline_mode=pl.Buffered(1))


def _ln_inproj_kernel(x_ref, g_ref, b_ref, w_ref, wqvt_ref, bf_ref, h_ref, u_ref, qt_ref, k_ref, va_ref, lf_ref,
                      *, ssm_w, att_w):
    h = _layer_norm(x_ref[...], g_ref[...], b_ref[...])
    h_ref[...] = h
    hb = h.astype(BF16)
    o = 0
    u_ref[...] = _dot(hb, w_ref[:, o:o + ssm_w])
    o += ssm_w
    k_ref[...] = _dot(hb, w_ref[:, o:o + att_w]).astype(BF16)
    o += att_w
    qvt = _dot_nt(wqvt_ref[...], hb)
    qt_ref[...] = (qvt[:att_w] * (LOG2E * ATT_HEAD_DIM ** -0.5)).astype(BF16)
    vt = qvt[att_w:].astype(BF16)
    dh = ATT_HEAD_DIM
    for hh in range(att_w // dh):
        va_ref[0, hh, 0, 0:dh, :] = vt[hh * dh:(hh + 1) * dh]
        va_ref[0, hh, 0, dh:, :] = jnp.ones((ONES_ROWS, vt.shape[1]), BF16)
    f = _dot(hb, w_ref[:, o:o + LANES]) + bf_ref[...]
    lf_ref[...] = jnp.minimum(f, 0.0) - jnp.log(1.0 + jnp.exp(-jnp.abs(f)))


def _ln_inproj(x2d, g, b, w_bf, wqvt_bf, bf_pad, tm, ssm_w, att_w, seq, tq):
    t, d = x2d.shape
    wcols = w_bf.shape[1]
    heads, da = att_w // ATT_HEAD_DIM, ATT_HEAD_DIM + ONES_ROWS
    row = lambda i: (i, 0)
    col = lambda i: (0, i)
    per_seq, per_q = seq // tm, tq // tm
    va_idx = lambda i: (i // per_seq, 0, (i % per_seq) // per_q, 0, (i % per_seq) % per_q)
    return pl.pallas_call(
        functools.partial(_ln_inproj_kernel, ssm_w=ssm_w, att_w=att_w),
        grid=(t // tm,),
        in_specs=[
            pl.BlockSpec((tm, d), row),
            _const_spec((1, d)), _const_spec((1, d)),
            _const_spec((d, wcols)), _const_spec((2 * att_w, d)), _const_spec((1, LANES)),
        ],
        out_specs=[
            pl.BlockSpec((tm, d), row),
            pl.BlockSpec((tm, ssm_w), row),
            pl.BlockSpec((att_w, tm), col),
            pl.BlockSpec((tm, att_w), row),
            pl.BlockSpec((1, heads, 1, da, tm), va_idx),
            pl.BlockSpec((tm, LANES), row),
        ],
        out_shape=[
            jax.ShapeDtypeStruct((t, d), F32),
            jax.ShapeDtypeStruct((t, ssm_w), F32),
            jax.ShapeDtypeStruct((att_w, t), BF16),
            jax.ShapeDtypeStruct((t, att_w), BF16),
            jax.ShapeDtypeStruct((t // seq, heads, seq // tq, da, tq), BF16),
            jax.ShapeDtypeStruct((t, LANES), F32),
        ],
        compiler_params=_params(("parallel",), 56),
        name="ln_inproj",
    )(x2d, g, b, w_bf, wqvt_bf, bf_pad)


def _s5_tables(a_re, a_im, log_dt, b_re, b_im, c_re, c_im, d_skip, levels):
    c = S5_CHUNK
    g, p = a_re.shape
    hc = SSM_GROUP_CH
    lam = lax.complex(a_re.astype(F32), a_im.astype(F32))
    dt = jnp.exp(log_dt.astype(F32))[:, None]
    ldt = lam * dt
    abar = jnp.exp(ldt)
    bbar = ((abar - 1.0) / lam)[..., None] * lax.complex(b_re.astype(F32), b_im.astype(F32))
    cc = lax.complex(c_re.astype(F32), c_im.astype(F32))
    tau = jnp.arange(c + 1, dtype=F32)
    apow = jnp.exp(ldt[:, None, :] * tau[None, :, None])

    ca = cc[:, None, :, :] * apow[:, :c, None, :]
    taps = (jnp.einsum('gthp,gpk->gthk', ca.real, bbar.real, precision=HIGHEST)
            - jnp.einsum('gthp,gpk->gthk', ca.imag, bbar.imag, precision=HIGHEST))
    taps = taps + (jnp.eye(hc, dtype=F32) * d_skip.astype(F32)[:, :, None])[:, None] * (jnp.arange(c) == 0)[None, :, None, None]
    lagmat = taps.transpose(0, 3, 1, 2).reshape(g, hc, c * hc)
    m_t = jnp.stack([jnp.pad(lagmat[:, :, :(c - s) * hc], ((0, 0), (0, 0), (s * hc, 0))) for s in range(c)], axis=1)
    m_t = m_t.reshape(g, c * hc, c * hc)

    bp = apow[:, c - 1 - jnp.arange(c), :][:, :, :, None] * bbar[:, None, :, :]
    bp = bp.transpose(0, 1, 3, 2).reshape(g, c * hc, p)
    bp_t = jnp.concatenate([bp.real, bp.imag], axis=-1)

    cp = cc[:, None, :, :] * apow[:, 1:c + 1, None, :]
    cp = cp.reshape(g, c * hc, p).transpose(0, 2, 1)
    cp_t = jnp.concatenate([cp.real, -cp.imag], axis=1)

    steps = (c * (2.0 ** jnp.arange(levels, dtype=F32)))
    alev = jnp.exp(ldt[:, None, :] * steps[None, :, None])

    gl = LANES // hc
    nj = g // gl
    m_rows = m_t.reshape(nj, gl, c, hc, c * hc).transpose(0, 2, 1, 3, 4).reshape(nj, c * LANES, c * hc)
    bp_rows = bp_t.reshape(nj, gl, c, hc, 2 * p).transpose(0, 2, 1, 3, 4).reshape(nj, c * LANES, 2 * p)
    cp_rows = cp_t.reshape(nj, gl, 2, p, c * hc).transpose(0, 2, 1, 3, 4).reshape(nj, 2 * gl * p, c * hc)

    def selector(n_outer, inner):
        cols, wcols = n_outer * inner, n_outer * gl * inner
        src, dst = jnp.arange(cols)[:, None], jnp.arange(wcols)[None, :]
        return ((src // inner == dst // (gl * inner)) & (src % inner == dst % inner)).astype(BF16)

    al = alev.reshape(nj, gl, levels, p).transpose(0, 2, 1, 3).reshape(nj, levels, gl * p)
    pad = ((0, 0), (0, 16 - levels), (0, 0))
    a1 = jnp.pad(jnp.concatenate([al.real, al.real], axis=-1), pad)
    a2 = jnp.pad(jnp.concatenate([-al.imag, al.imag], axis=-1), pad)
    return (m_rows.astype(BF16), bp_rows.astype(BF16), cp_rows.astype(BF16),
            selector(c, hc), selector(2, p), a1, a2)


def _shift_rows(x, sh):
    n = x.shape[0]
    if sh % 8 == 0:
        return jnp.concatenate([jnp.zeros((sh, x.shape[1]), x.dtype), x[:n - sh]], axis=0)
    rolled = pltpu.roll(x, sh, axis=0)
    rows = lax.broadcasted_iota(jnp.int32, x.shape, 0)
    return jnp.where(rows < sh, 0.0, rolled)


def _cmul(a1, a2, x):
    return a1 * x + a2 * pltpu.roll(x, x.shape[1] // 2, axis=1)


def _spread_block_diag(rows_ref, sel_ref, out_scr, row_span, col_span):
    gl = LANES // SSM_GROUP_CH
    rows = rows_ref[0]
    n_rows, width = out_scr.shape
    step = 2 * LANES
    r_grp = (lax.broadcasted_iota(jnp.int32, (n_rows, step), 0) >> (row_span.bit_length() - 1)) & (gl - 1)
    for c0 in range(0, width, step):
        wide = _dot(rows, sel_ref[:, c0:c0 + step])
        w_grp = ((lax.broadcasted_iota(jnp.int32, (n_rows, step), 1) + c0) >> (col_span.bit_length() - 1)) & (gl - 1)
        out_scr[:, c0:c0 + step] = jnp.where(r_grp == w_grp, wide, 0.0).astype(BF16)


def _s5_kernel(u_ref, um_ref, mrow_ref, bprow_ref, cprow_ref, selm_ref, selb_ref, a1_ref, a2_ref, y_ref,
               mj_scr, bpj_scr, cpj_scr, *, nc, levels):
    c = S5_CHUNK

    @pl.when(pl.program_id(1) == 0)
    def _():
        _spread_block_diag(mrow_ref, selm_ref, mj_scr, SSM_GROUP_CH, SSM_GROUP_CH)
        _spread_block_diag(bprow_ref, selb_ref, bpj_scr, SSM_GROUP_CH, SSM_STATE)
        _spread_block_diag(cprow_ref, selm_ref, cpj_scr, SSM_STATE, SSM_GROUP_CH)

    u = jnp.concatenate([u_ref[pl.ds(s, nc, stride=c), :].astype(BF16) for s in range(c)], axis=1)
    bpj = bpj_scr[...]
    w = _dot(u, bpj)
    um = jnp.concatenate([um_ref[s:s + 1, :] for s in range(c)], axis=1)
    x_meta = _dot(jnp.broadcast_to(um, (8, um.shape[1])).astype(BF16), bpj)
    first = lax.broadcasted_iota(jnp.int32, x_meta.shape, 0) == 0
    inject = jnp.where(first, _cmul(a1_ref[0, 0:1, :], a2_ref[0, 0:1, :], x_meta), 0.0)
    x = jnp.concatenate([w[:8] + inject, w[8:]], axis=0)
    for lvl in range(levels):
        x = x + _cmul(a1_ref[0, lvl:lvl + 1, :], a2_ref[0, lvl:lvl + 1, :], _shift_rows(x, 2 ** lvl))
    xp = _shift_rows(x, 1)
    x_in = jnp.concatenate([xp[:8] + jnp.where(first, x_meta, 0.0), xp[8:]], axis=0).astype(BF16)
    y = jax.nn.gelu(_dot(u, mj_scr[...]) + _dot(x_in, cpj_scr[...]))
    for s in range(c):
        y_ref[pl.ds(s, nc, stride=c), :] = y[:, s * LANES:(s + 1) * LANES]


def _s5(u, u_m, m_rows, bp_rows, cp_rows, sel_m, sel_b, a1, a2, bsz, nc, levels):
    t, ssm_w = u.shape
    nj, cl, cm = m_rows.shape
    sw, cb = cp_rows.shape[1], bp_rows.shape[2]
    nl = a1.shape[1]
    rows = nc * S5_CHUNK
    tab = lambda j, b: (j, 0, 0)
    return pl.pallas_call(
        functools.partial(_s5_kernel, nc=nc, levels=levels),
        grid=(nj, bsz),
        in_specs=[
            pl.BlockSpec((rows, LANES), lambda j, b: (b, j)),
            pl.BlockSpec((S5_CHUNK, LANES), lambda j, b: (0, j)),
            pl.BlockSpec((1, cl, cm), tab),
            pl.BlockSpec((1, cl, cb), tab),
            pl.BlockSpec((1, sw, cm), tab),
            _const_spec((cm, cl)), _const_spec((cb, sw)),
            pl.BlockSpec((1, nl, sw), tab),
            pl.BlockSpec((1, nl, sw), tab),
        ],
        out_specs=pl.BlockSpec((rows, LANES), lambda j, b: (b, j)),
        out_shape=jax.ShapeDtypeStruct((t, ssm_w), F32),
        scratch_shapes=[pltpu.VMEM((cl, cl), BF16), pltpu.VMEM((cl, sw), BF16), pltpu.VMEM((sw, cl), BF16)],
        compiler_params=_params(("parallel", "arbitrary"), 56),
        name="s5_scan",
    )(u, u_m, m_rows, bp_rows, cp_rows, sel_m, sel_b, a1, a2)


def _fox_kernel(qt_ref, k_ref, va_ref, ck_ref, km_ref, vam_ref, ckm_ref, o_ref,
                m_scr, acc_scr, s_scr, bm_scr, p_scr, al_scr, *, tq, dh):
    qi = pl.program_id(2)
    qt = qt_ref[...]
    reps = tq // LANES

    def scores(kb, ckb):
        return _dot(kb, qt) - jnp.concatenate([ckb] * reps, axis=1)

    def put_scores(slot, s):
        s_scr[slot] = s
        bm_scr[slot] = jnp.broadcast_to(jnp.max(s, axis=0, keepdims=True), (8, tq))

    def softmax_update(slot):
        m_prev = m_scr[0:1, :]
        m_new = jnp.maximum(m_prev, bm_scr[slot, 0:1, :])
        m_scr[...] = jnp.broadcast_to(m_new, m_scr.shape)
        return jnp.exp2(m_prev - m_new), jnp.exp2((s_scr[slot] - m_new).astype(BF16))

    s = scores(km_ref[...], ckm_ref[0])
    m0 = jnp.max(s, axis=0, keepdims=True)
    m_scr[...] = jnp.broadcast_to(m0, m_scr.shape)
    acc_scr[...] = _dot(vam_ref[0], jnp.exp2(s - m0).astype(BF16))

    start = pl.multiple_of(qi * tq, tq)
    s = scores(k_ref[pl.ds(start, tq), :], ck_ref[0, 0, pl.ds(start, tq), :])
    key = lax.broadcasted_iota(jnp.int32, (tq, tq), 0)
    qry = lax.broadcasted_iota(jnp.int32, (tq, tq), 1)
    ns = FOX_SLOTS
    put_scores(ns - 1, jnp.where(key <= qry, s, NEG_BIG))
    p_scr[ns - 2] = jnp.zeros((tq, tq), BF16)
    al_scr[ns - 2] = jnp.ones((8, tq), F32)

    def tick(t, u):
        sb, sc = (u - 1) % ns, (u - 2) % ns
        jc = jnp.where(t == 1, qi, jnp.clip(t - 2, 0, qi))
        acc_scr[...] = al_scr[sc, 0:1, :] * acc_scr[...] + _dot(va_ref[0, 0, jc], p_scr[sc])
        alpha, p = softmax_update(sb)
        p_scr[sb] = p
        al_scr[sb] = jnp.broadcast_to(alpha, (8, tq))
        ja = jnp.minimum(t, jnp.maximum(qi - 1, 0))
        start = pl.multiple_of(ja * tq, tq)
        off = jnp.where(t < qi, 0.0, -NEG_BIG)
        put_scores(u, scores(k_ref[pl.ds(start, tq), :], ck_ref[0, 0, pl.ds(start, tq), :] + off))

    def body(i, carry):
        for u in range(ns):
            tick(ns * i + u, u)
        return carry

    n_ticks = (qi + 3) // 2 * 2
    n_full = n_ticks // ns
    lax.fori_loop(0, n_full, body, 0)

    @pl.when(n_ticks % ns != 0)
    def _():
        tick(ns * n_full, 0)
        tick(ns * n_full + 1, 1)
    acc = acc_scr[...]
    o_ref[...] = (acc[:dh] / acc[dh:dh + 1]).T.astype(BF16)


def _fox(qt, k, va, ck, km, vam, ckm, bsz, seq, heads, tq):
    aw, t = qt.shape
    nq = seq // tq
    dh = ATT_HEAD_DIM
    da = va.shape[3]
    return pl.pallas_call(
        functools.partial(_fox_kernel, tq=tq, dh=dh),
        grid=(bsz, heads, nq),
        in_specs=[
            pl.BlockSpec((dh, tq), lambda b, h, i: (h, b * nq + i)),
            pl.BlockSpec((seq, dh), lambda b, h, i: (b, h)),
            pl.BlockSpec((1, 1, nq, da, tq), lambda b, h, i: (b, h, 0, 0, 0)),
            pl.BlockSpec((1, 1, seq, LANES), lambda b, h, i: (b, h, 0, 0)),
            pl.BlockSpec((LANES, dh), lambda b, h, i: (0, h)),
            pl.BlockSpec((1, da, LANES), lambda b, h, i: (h, 0, 0)),
            pl.BlockSpec((1, LANES, LANES), lambda b, h, i: (h, 0, 0)),
        ],
        out_specs=pl.BlockSpec((tq, dh), lambda b, h, i: (b * nq + i, h)),
        out_shape=jax.ShapeDtypeStruct((t, aw), BF16),
        scratch_shapes=[
            pltpu.VMEM((8, tq), F32),
            pltpu.VMEM((da, tq), F32),
            pltpu.VMEM((FOX_SLOTS, tq, tq), F32),
            pltpu.VMEM((FOX_SLOTS, 8, tq), F32),
            pltpu.VMEM((FOX_SLOTS, tq, tq), BF16),
            pltpu.VMEM((FOX_SLOTS, 8, tq), F32),
        ],
        compiler_params=_params(("parallel", "parallel", "arbitrary"), 48),
        name="fox_attention",
    )(qt, k, va, ck, km, vam, ckm)


def _mix_out_kernel(ys_ref, ya_ref, h_ref, wglu_ref, bglu_ref, wout_ref, g_ref, b_ref, wr_ref,
                    h2_ref, h2p_ref, lt_ref, *, ssm_w):
    ys = ys_ref[...]
    z = _dot(ys.astype(BF16), wglu_ref[...]) + bglu_ref[...]
    yg = (ys * jax.nn.sigmoid(z)).astype(BF16)
    mix = _dot(yg, wout_ref[0:ssm_w, :]) + _dot(ya_ref[...], wout_ref[ssm_w:, :])
    h2 = _layer_norm(DEEPNORM_ALPHA * h_ref[...] + mix, g_ref[...], b_ref[...])
    h2_ref[...] = h2
    _pack_rows(h2, h2p_ref)
    lt_ref[...] = _dot_nt(wr_ref[...], h2.astype(BF16))


def _mix_out(ys, ya, h, wglu, bglu, wout, g, b, wr_t, tm):
    t, d = h.shape
    ssm_w = ys.shape[1]
    att_w = ya.shape[1]
    e = wr_t.shape[0]
    row = lambda i: (i, 0)
    return pl.pallas_call(
        functools.partial(_mix_out_kernel, ssm_w=ssm_w),
        grid=(t // tm,),
        in_specs=[
            pl.BlockSpec((tm, ssm_w), row),
            pl.BlockSpec((tm, att_w), row),
            pl.BlockSpec((tm, d), row),
            _const_spec((ssm_w, ssm_w)), _const_spec((1, ssm_w)),
            _const_spec((d, d)), _const_spec((1, d)), _const_spec((1, d)),
            _const_spec((e, d)),
        ],
        out_specs=[
            pl.BlockSpec((tm, d), row),
            pl.BlockSpec((tm * ROW_TILE, LANES), row),
            pl.BlockSpec((e, tm), lambda i: (0, i)),
        ],
        out_shape=[
            jax.ShapeDtypeStruct((t, d), F32),
            jax.ShapeDtypeStruct((t * ROW_TILE, LANES), jnp.uint32),
            jax.ShapeDtypeStruct((e, t), F32),
        ],
        compiler_params=_params(("parallel",), 48),
        name="mix_out",
    )(ys, ya, h, wglu, bglu, wout, g, b, wr_t)


def _route_kernel(lt_ref, bias_ref, idx_ref, gate_ref, rank_ref, cnt_ref, run_scr, *, tn):
    e = N_EXPERTS
    ng = N_EXPERT_GROUPS
    gs = e // ng

    @pl.when(pl.program_id(0) == 0)
    def _():
        run_scr[...] = jnp.zeros_like(run_scr)

    scores = jax.nn.sigmoid(lt_ref[...])
    sel = scores + bias_ref[...]
    sel3 = sel.reshape(ng, gs, tn)
    mem = lax.broadcasted_iota(jnp.int32, (ng, gs, tn), 1)
    m1 = jnp.max(sel3, axis=1, keepdims=True)
    i1 = jnp.min(jnp.where(sel3 == m1, mem, gs), axis=1, keepdims=True)
    m2 = jnp.max(jnp.where(mem == i1, -jnp.inf, sel3), axis=1, keepdims=True)
    gscore = (m1 + m2).reshape(ng, tn)

    gio = lax.broadcasted_iota(jnp.int32, (ng, tn), 0)
    gmask = jnp.zeros((ng, tn), F32)
    for _ in range(TOPK_GROUPS):
        mx = jnp.max(gscore, axis=0, keepdims=True)
        gi = jnp.min(jnp.where(gscore == mx, gio, ng), axis=0, keepdims=True)
        hit = gio == gi
        gmask = jnp.where(hit, 1.0, gmask)
        gscore = jnp.where(hit, -jnp.inf, gscore)

    cand = jnp.where(gmask.reshape(ng, 1, tn) > 0.5, sel3, -jnp.inf).reshape(e, tn)
    eio = lax.broadcasted_iota(jnp.int32, (e, tn), 0)
    hits = []
    chosen = jnp.zeros((e, tn), F32)
    for _ in range(TOP_K):
        mx = jnp.max(cand, axis=0, keepdims=True)
        ei = jnp.min(jnp.where(cand == mx, eio, e), axis=0, keepdims=True)
        hit = eio == ei
        hits.append((ei, hit))
        chosen = jnp.where(hit, 1.0, chosen)
        cand = jnp.where(hit, -jnp.inf, cand)

    onehot = chosen.astype(BF16)
    r = lax.broadcasted_iota(jnp.int32, (tn, tn), 0)
    c = lax.broadcasted_iota(jnp.int32, (tn, tn), 1)
    tri = (r < c).astype(BF16)
    rank = _dot(onehot, tri) + run_scr[:, 0:1]
    run_scr[...] = run_scr[...] + jnp.sum(chosen, axis=1, keepdims=True)
    cnt_ref[...] = run_scr[...]

    gates = [jnp.sum(jnp.where(hit, scores, 0.0), axis=0, keepdims=True) for _, hit in hits]
    total = gates[0]
    for gk in gates[1:]:
        total = total + gk
    for kk, (ei, hit) in enumerate(hits):
        idx_ref[kk:kk + 1, :] = ei
        gate_ref[kk:kk + 1, :] = gates[kk] / total * ROUTED_SCALE
        rank_ref[kk:kk + 1, :] = jnp.sum(jnp.where(hit, rank, 0.0), axis=0, keepdims=True).astype(jnp.int32)


def _route(logits_t, bias, tn):
    e, t = logits_t.shape
    col = lambda i: (0, i)
    return pl.pallas_call(
        functools.partial(_route_kernel, tn=tn),
        grid=(t // tn,),
        in_specs=[pl.BlockSpec((e, tn), col), _const_spec((e, 1))],
        out_specs=[
            pl.BlockSpec((TOP_K, tn), col),
            pl.BlockSpec((TOP_K, tn), col),
            pl.BlockSpec((TOP_K, tn), col),
            pl.BlockSpec((e, LANES), lambda i: (0, 0)),
        ],
        out_shape=[
            jax.ShapeDtypeStruct((TOP_K, t), jnp.int32),
            jax.ShapeDtypeStruct((TOP_K, t), F32),
            jax.ShapeDtypeStruct((TOP_K, t), jnp.int32),
            jax.ShapeDtypeStruct((e, LANES), F32),
        ],
        scratch_shapes=[pltpu.VMEM((e, LANES), F32)],
        compiler_params=_params(("arbitrary",), 32),
        name="route",
    )(logits_t, bias)


def _tile_rows(ref, row):
    return ref.at[pl.ds(pl.multiple_of(row * ROW_TILE, ROW_TILE), ROW_TILE), :]


def _dispatch_kernel(ps_ref, pe_ref, tail_ref, pos_ref, x_ref, wgu_ref, wd_ref, xs_ref, sh_ref, zero_scr, sem,
                     *, tm, bm, n_blocks, ff):
    @pl.when(pl.program_id(0) == 0)
    def _():
        zero_scr[...] = jnp.zeros_like(zero_scr)

        def fill(start):
            def per_expert(e, carry):
                first = ps_ref[e]
                n_pad = pe_ref[e] - first
                size = bm // 2
                while size >= 1:
                    done = n_pad & ~(2 * size - 1)

                    @pl.when((n_pad & size) != 0)
                    def _(size=size, done=done):
                        rows = pl.ds(pl.multiple_of((first + done) * ROW_TILE, ROW_TILE), size * ROW_TILE)
                        cp = pltpu.make_async_copy(zero_scr.at[0:size * ROW_TILE, :], xs_ref.at[rows, :], sem)
                        cp.start() if start else cp.wait()

                    size //= 2
                return carry
            lax.fori_loop(0, N_EXPERTS, per_expert, 0)

            def per_block(b, c):
                rows = pl.ds(pl.multiple_of(b * bm * ROW_TILE, bm * ROW_TILE), bm * ROW_TILE)
                cp = pltpu.make_async_copy(zero_scr, xs_ref.at[rows, :], sem)
                cp.start() if start else cp.wait()
                return c
            lax.fori_loop(tail_ref[0], n_blocks, per_block, 0)

        fill(True)
        fill(False)

    def issue(t, carry):
        src = _tile_rows(x_ref, t)
        for kk in range(TOP_K):
            pltpu.make_async_copy(src, _tile_rows(xs_ref, pos_ref[kk, t]), sem).start()
        return carry

    lax.fori_loop(0, tm, issue, 0)
    lo, hi = _unpack_rows(x_ref, 0, tm)
    x = jnp.concatenate([p.astype(BF16) for p in lo + hi], axis=1)
    gu = _dot(x, wgu_ref[...])
    hid = jax.nn.silu(gu[:, :ff]) * gu[:, ff:]
    sh_ref[...] = _dot(hid.astype(BF16), wd_ref[...])
    for _ in range(TOP_K):
        pltpu.make_async_copy(x_ref, xs_ref.at[pl.ds(0, tm * ROW_TILE), :], sem).wait()


def _dispatch(pad_start, pad_end, tail_block, pos, h2p, wgu, wd, n_blocks, bm, tm):
    t = pos.shape[1]
    ff, d = wd.shape
    grid_spec = pltpu.PrefetchScalarGridSpec(
        num_scalar_prefetch=3,
        grid=(t // tm,),
        in_specs=[
            pl.BlockSpec((TOP_K, tm), lambda i, ps, pe, tl: (0, i), memory_space=pltpu.SMEM),
            pl.BlockSpec((tm * ROW_TILE, LANES), lambda i, ps, pe, tl: (i, 0)),
            _const_spec((d, 2 * ff)), _const_spec((ff, d)),
        ],
        out_specs=[pl.BlockSpec(memory_space=pl.ANY), pl.BlockSpec((tm, d), lambda i, ps, pe, tl: (i, 0))],
        scratch_shapes=[pltpu.VMEM((bm * ROW_TILE, LANES), jnp.uint32), pltpu.SemaphoreType.DMA(())],
    )
    return pl.pallas_call(
        functools.partial(_dispatch_kernel, tm=tm, bm=bm, n_blocks=n_blocks, ff=ff),
        grid_spec=grid_spec,
        out_shape=[jax.ShapeDtypeStruct((n_blocks * bm * ROW_TILE, LANES), jnp.uint32),
                   jax.ShapeDtypeStruct((t, d), F32)],
        compiler_params=_params(("arbitrary",), 40),
        name="dispatch",
    )(pad_start, pad_end, tail_block, pos, h2p, wgu, wd)


def _moe_kernel(be_ref, bv_ref, bf_ref, nx_ref, sl_ref, x_ref, wg_hbm, wu_hbm, wd_hbm, y_ref,
                wg_f, wu_f, wd_f, wg_s, wu_s, wd_s, sem, *, bm):
    b = pl.program_id(0)

    def weights(e, slot, start):
        for i, (src, dst) in enumerate(((wg_hbm, wg_f), (wu_hbm, wu_f), (wd_hbm, wd_f))):
            cp = pltpu.make_async_copy(src.at[e], dst.at[slot], sem.at[slot, i])
            cp.start() if start else cp.wait()

    @pl.when(bf_ref[b] == 1)
    def _():
        slot = sl_ref[b]

        @pl.when(b == 0)
        def _():
            weights(be_ref[0], slot, True)

        weights(be_ref[b], slot, False)

        @pl.when(nx_ref[b] >= 0)
        def _():
            weights(nx_ref[b], 1 - slot, True)

        wg_s[...] = wg_f[slot].astype(BF16)
        wu_s[...] = wu_f[slot].astype(BF16)
        wd_s[...] = wd_f[slot].astype(BF16)

    @pl.when(bv_ref[b] == 1)
    def _():
        lo, hi = _unpack_rows(x_ref, 0, bm)
        x = jnp.concatenate([p.astype(BF16) for p in lo + hi], axis=1)
        hid = jax.nn.silu(_dot(x, wg_s[...])) * _dot(x, wu_s[...])
        _pack_rows(_dot(hid.astype(BF16), wd_s[...]), y_ref)

    @pl.when(bv_ref[b] == 0)
    def _():
        y_ref[...] = jnp.zeros_like(y_ref)


def _moe_grouped(block_e, block_valid, block_first, next_e, slot_of, xs, wg, wu, wd, bm):
    rows = xs.shape[0] // ROW_TILE
    d, ff = wg.shape[1], wg.shape[2]
    n_blocks = rows // bm
    blk = lambda b, *_: (b, 0)
    grid_spec = pltpu.PrefetchScalarGridSpec(
        num_scalar_prefetch=5,
        grid=(n_blocks,),
        in_specs=[
            pl.BlockSpec((bm * ROW_TILE, LANES), blk),
            pl.BlockSpec(memory_space=pl.ANY), pl.BlockSpec(memory_space=pl.ANY), pl.BlockSpec(memory_space=pl.ANY),
        ],
        out_specs=pl.BlockSpec((bm * ROW_TILE, LANES), blk),
        scratch_shapes=[
            pltpu.VMEM((2, d, ff), F32), pltpu.VMEM((2, d, ff), F32), pltpu.VMEM((2, ff, d), F32),
            pltpu.VMEM((d, ff), BF16), pltpu.VMEM((d, ff), BF16), pltpu.VMEM((ff, d), BF16),
            pltpu.SemaphoreType.DMA((2, 3)),
        ],
    )
    return pl.pallas_call(
        functools.partial(_moe_kernel, bm=bm),
        grid_spec=grid_spec,
        out_shape=jax.ShapeDtypeStruct(xs.shape, jnp.uint32),
        compiler_params=_params(("arbitrary",), 56),
        name="moe_grouped",
    )(block_e, block_valid, block_first, next_e, slot_of, xs, wg, wu, wd)


def _final_kernel(pos_ref, posn_ref, h2_ref, sh_ref, gate_ref, g_ref, b_ref, ys_ref, o_ref, ybuf, sem, *, tm):
    i = pl.program_id(0)
    slot = i % 2
    tile = TOP_K * tm * ROW_TILE

    def gather(p_ref, dst_slot):
        def issue(t, carry):
            for kk in range(TOP_K):
                dst = ybuf.at[pl.ds(pl.multiple_of(dst_slot * tile + (kk * tm + t) * ROW_TILE, ROW_TILE), ROW_TILE), :]
                pltpu.make_async_copy(_tile_rows(ys_ref, p_ref[kk, t]), dst, sem.at[dst_slot]).start()
            return carry
        lax.fori_loop(0, tm, issue, 0)

    @pl.when(i == 0)
    def _():
        gather(pos_ref, slot)

    @pl.when(i + 1 < pl.num_programs(0))
    def _():
        gather(posn_ref, 1 - slot)

    base = pl.multiple_of(slot * tile, tile)
    for kk in range(TOP_K):
        pltpu.make_async_copy(ys_ref.at[pl.ds(0, tm * ROW_TILE), :],
                              ybuf.at[pl.ds(base + kk * tm * ROW_TILE, tm * ROW_TILE), :], sem.at[slot]).wait()
    gate = gate_ref[...]
    acc = None
    for kk in range(TOP_K):
        lo, hi = _unpack_rows(ybuf, base + kk * tm * ROW_TILE, tm)
        gk = gate[:, kk:kk + 1]
        terms = [gk * p for p in lo + hi]
        acc = terms if acc is None else [a + b for a, b in zip(acc, terms)]
    ffn = jnp.concatenate(acc, axis=1) + sh_ref[...]
    o_ref[...] = _layer_norm(DEEPNORM_ALPHA * h2_ref[...] + ffn, g_ref[...], b_ref[...])


def _final(pos, h2, shared, gate, ys, g, b, tm):
    t, d = h2.shape
    row = lambda i: (i, 0)
    return pl.pallas_call(
        functools.partial(_final_kernel, tm=tm),
        grid=(t // tm,),
        in_specs=[
            pl.BlockSpec((TOP_K, tm), lambda i: (0, i), memory_space=pltpu.SMEM),
            pl.BlockSpec((TOP_K, tm), lambda i: (0, jnp.minimum(i + 1, t // tm - 1)), memory_space=pltpu.SMEM),
            pl.BlockSpec((tm, d), row),
            pl.BlockSpec((tm, d), row),
            pl.BlockSpec((tm, TOP_K), row),
            _const_spec((1, d)), _const_spec((1, d)),
            pl.BlockSpec(memory_space=pl.ANY),
        ],
        out_specs=pl.BlockSpec((tm, d), row),
        out_shape=jax.ShapeDtypeStruct((t, d), F32),
        scratch_shapes=[pltpu.VMEM((2 * TOP_K * tm * ROW_TILE, LANES), jnp.uint32), pltpu.SemaphoreType.DMA((2,))],
        compiler_params=_params(("arbitrary",), 56),
        name="final",
    )(pos, pos, h2, shared, gate, g, b, ys)


def kernel(x, meta_tokens, ln_in_g, ln_in_b, w_in, b_forget, ssm_a_re, ssm_a_im, ssm_log_dt, ssm_b_re, ssm_b_im, ssm_c_re, ssm_c_im, ssm_d, w_glu, b_glu, w_out, ln_mix_g, ln_mix_b, w_router, router_bias, w_gate_exp, w_up_exp, w_down_exp, w_gate_sh, w_up_sh, w_down_sh, ln_ffn_g, ln_ffn_b):
    bsz, seq, d = x.shape
    t = bsz * seq
    ssm_w = w_glu.shape[1]
    heads = b_forget.shape[1]
    att_w = (w_in.shape[2] - ssm_w - heads) // 3
    groups = ssm_a_re.shape[1]
    assert meta_tokens.shape[0] == N_META == S5_CHUNK and att_w == heads * ATT_HEAD_DIM
    nc = seq // S5_CHUNK
    levels = int(math.log2(nc))
    assert 2 ** levels == nc and nc % 8 == 0
    row2 = lambda a: a.reshape(1, -1).astype(F32)

    w_u, w_q, w_k, w_v, w_f = jnp.split(w_in[0], [ssm_w, ssm_w + att_w, ssm_w + 2 * att_w, ssm_w + 3 * att_w], axis=1)
    w_f = jnp.pad(w_f, ((0, 0), (0, LANES - heads)))
    w_in_bf = jnp.concatenate([w_u, w_k, w_f], axis=1).astype(BF16)
    wqvt_bf = jnp.concatenate([w_q, w_v], axis=1).T.astype(BF16)
    bf_pad = jnp.pad(b_forget[0].astype(F32), (0, LANES - heads)).reshape(1, LANES)
    g_in, b_in = row2(ln_in_g), row2(ln_in_b)
    tm = min(256, t)
    tq = min(512, seq)
    h, u, qt, k, va, lf = _ln_inproj(x.reshape(t, d), g_in, b_in, w_in_bf, wqvt_bf, bf_pad, tm, ssm_w, att_w,
                                     seq, tq)
    _, u_m, _, k_m, va_m, lf_m = _ln_inproj(meta_tokens.astype(F32), g_in, b_in, w_in_bf, wqvt_bf, bf_pad, N_META,
                                            ssm_w, att_w, N_META, N_META)

    s5_tabs = _s5_tables(ssm_a_re[0], ssm_a_im[0], ssm_log_dt[0], ssm_b_re[0], ssm_b_im[0],
                         ssm_c_re[0], ssm_c_im[0], ssm_d[0], levels)
    y_ssm = _s5(u, u_m, *s5_tabs, bsz, nc, levels)

    lfm = lf_m[:, :heads] * LOG2E
    c_meta = jnp.cumsum(lfm, axis=0) - jnp.sum(lfm, axis=0, keepdims=True)
    ckm = jnp.full((heads, LANES), -NEG_BIG, F32).at[:, :N_META].set(c_meta.T)
    ckm = jnp.broadcast_to(ckm[:, :, None], (heads, LANES, LANES))
    c_main = jnp.cumsum(lf[:, :heads].reshape(bsz, seq, heads) * LOG2E, axis=1)
    ck = jnp.broadcast_to(c_main.transpose(0, 2, 1)[..., None], (bsz, heads, seq, LANES))
    km = jnp.pad(k_m, ((0, LANES - N_META), (0, 0)))
    vam = jnp.pad(va_m[0, :, 0], ((0, 0), (0, 0), (0, LANES - N_META)))
    y_att = _fox(qt, k, va, ck, km, vam, ckm, bsz, seq, heads, tq)

    h2, h2p, logits_t = _mix_out(
        y_ssm, y_att, h, w_glu[0].astype(BF16), row2(b_glu[0]), w_out[0].astype(BF16),
        row2(ln_mix_g[0]), row2(ln_mix_b[0]), w_router[0].T.astype(BF16), tm)

    tn = min(512, t)
    idx_t, gate_t, rank_t, counts = _route(logits_t, router_bias[0].astype(F32).reshape(N_EXPERTS, 1), tn)

    bm = 256
    counts = counts[:, 0].astype(jnp.int32)
    pcounts = (counts + bm - 1) // bm * bm
    pends = jnp.cumsum(pcounts)
    pstarts = pends - pcounts
    start_of = jnp.sum(jnp.where(idx_t[:, :, None] == jnp.arange(N_EXPERTS, dtype=jnp.int32), pstarts, 0), axis=-1)
    pos = (start_of + rank_t).astype(jnp.int32)
    n_blocks = t * TOP_K // bm + N_EXPERTS
    bstart = jnp.arange(n_blocks, dtype=jnp.int32) * bm
    block_e = jnp.minimum(jnp.sum(pends[None, :] <= bstart[:, None], axis=1), N_EXPERTS - 1).astype(jnp.int32)
    block_valid = (bstart < pends[-1]).astype(jnp.int32)
    block_first = block_valid * jnp.concatenate([jnp.ones((1,), jnp.int32),
                                                 (block_e[1:] != block_e[:-1]).astype(jnp.int32)])
    slot_of = ((jnp.cumsum(block_first) - 1) % 2).astype(jnp.int32)
    n_valid = pends[-1] // bm
    nxt = jnp.arange(n_blocks, dtype=jnp.int32) + pcounts[block_e] // bm
    next_e = jnp.where(nxt < n_valid, block_e[jnp.minimum(nxt, n_blocks - 1)], -1).astype(jnp.int32)
    wgu_sh = jnp.concatenate([w_gate_sh[0], w_up_sh[0]], axis=1).astype(BF16)
    xs, shared = _dispatch((pstarts + counts).astype(jnp.int32), pends.astype(jnp.int32),
                           (pends[-1:] // bm).astype(jnp.int32), pos, h2p, wgu_sh, w_down_sh[0].astype(BF16),
                           n_blocks, bm, tm)
    ys = _moe_grouped(block_e, block_valid, block_first, next_e, slot_of, xs,
                      w_gate_exp[0], w_up_exp[0], w_down_exp[0], bm)

    out = _final(pos, h2, shared, gate_t.T, ys, row2(ln_ffn_g[0]), row2(ln_ffn_b[0]), tm)
    return out.reshape(bsz, seq, d)
```

```python
import functools
import math

import jax
import jax.numpy as jnp
from jax import lax
from jax.experimental import pallas as pl
from jax.experimental.pallas import tpu as pltpu

N_META = 16
SSM_GROUP_CH = 16
SSM_STATE = 64
ATT_HEAD_DIM = 128
N_EXPERTS = 64
TOP_K = 8
N_EXPERT_GROUPS = 8
TOPK_GROUPS = 4
ROUTED_SCALE = 2.5
LN_EPS = 1e-5
DEPTH = 1
DEEPNORM_ALPHA = (2 * DEPTH) ** 0.25

S5_CHUNK = 16
FOX_SLOTS = 4
LANES = 128
ROW_TILE = 8
ONES_ROWS = 16
NEG_BIG = -1e30
LOG2E = 1.4426950408889634

F32 = jnp.float32
BF16 = jnp.bfloat16
HIGHEST = lax.Precision.HIGHEST


def _dot(a, b):
    return jnp.dot(a, b, preferred_element_type=F32)


def _dot_nt(a, b):
    return lax.dot_general(a, b, (((1,), (1,)), ((), ())), preferred_element_type=F32)


def _layer_norm(x, g, b):
    mu = jnp.mean(x, axis=-1, keepdims=True)
    xc = x - mu
    var = jnp.mean(xc * xc, axis=-1, keepdims=True)
    return xc * lax.rsqrt(var + LN_EPS) * g + b


def _pack_rows(x, o_ref):
    m, w = x.shape
    half = w // 2
    assert half == ROW_TILE * LANES
    bits = lax.bitcast_convert_type(x.astype(BF16).astype(F32), jnp.uint32)
    packed = bits[:, half:] | (bits[:, :half] >> 16)
    for s in range(ROW_TILE):
        o_ref[pl.ds(s, m, stride=ROW_TILE), :] = packed[:, s * LANES:(s + 1) * LANES]


def _unpack_rows(x_ref, base, m):
    lo, hi = [], []
    for s in range(ROW_TILE):
        w = x_ref[pl.ds(base + s, m, stride=ROW_TILE), :]
        lo.append(lax.bitcast_convert_type(w << 16, F32))
        hi.append(lax.bitcast_convert_type(w & jnp.uint32(0xFFFF0000), F32))
    return lo, hi


def _params(sem, vmem_mb):
    return pltpu.CompilerParams(dimension_semantics=sem, vmem_limit_bytes=vmem_mb * 1024 * 1024)


def _const_spec(shape):
    nd = len(shape)
    return pl.BlockSpec(shape, lambda *_: (0,) * nd, pipeline_mode=pl.Buffered(1))


def _ln_inproj_kernel(x_ref, g_ref, b_ref, w_ref, wqvt_ref, bf_ref, h_ref, u_ref, qt_ref, k_ref, va_ref, lf_ref,
                      *, ssm_w, att_w):
    h = _layer_norm(x_ref[...], g_ref[...], b_ref[...])
    h_ref[...] = h
    hb = h.astype(BF16)
    o = 0
    u_ref[...] = _dot(hb, w_ref[:, o:o + ssm_w])
    o += ssm_w
    k_ref[...] = _dot(hb, w_ref[:, o:o + att_w]).astype(BF16)
    o += att_w
    qvt = _dot_nt(wqvt_ref[...], hb)
    qt_ref[...] = (qvt[:att_w] * (LOG2E * ATT_HEAD_DIM ** -0.5)).astype(BF16)
    vt = qvt[att_w:].astype(BF16)
    dh = ATT_HEAD_DIM
    for hh in range(att_w // dh):
        va_ref[0, hh, 0, 0:dh, :] = vt[hh * dh:(hh + 1) * dh]
        va_ref[0, hh, 0, dh:, :] = jnp.ones((ONES_ROWS, vt.shape[1]), BF16)
    f = _dot(hb, w_ref[:, o:o + LANES]) + bf_ref[...]
    lf_ref[...] = jnp.minimum(f, 0.0) - jnp.log(1.0 + jnp.exp(-jnp.abs(f)))


def _ln_inproj(x2d, g, b, w_bf, wqvt_bf, bf_pad, tm, ssm_w, att_w, seq, tq):
    t, d = x2d.shape
    wcols = w_bf.shape[1]
    heads, da = att_w // ATT_HEAD_DIM, ATT_HEAD_DIM + ONES_ROWS
    row = lambda i: (i, 0)
    col = lambda i: (0, i)
    per_seq, per_q = seq // tm, tq // tm
    va_idx = lambda i: (i // per_seq, 0, (i % per_seq) // per_q, 0, (i % per_seq) % per_q)
    return pl.pallas_call(
        functools.partial(_ln_inproj_kernel, ssm_w=ssm_w, att_w=att_w),
        grid=(t // tm,),
        in_specs=[
            pl.BlockSpec((tm, d), row),
            _const_spec((1, d)), _const_spec((1, d)),
            _const_spec((d, wcols)), _const_spec((2 * att_w, d)), _const_spec((1, LANES)),
        ],
        out_specs=[
            pl.BlockSpec((tm, d), row),
            pl.BlockSpec((tm, ssm_w), row),
            pl.BlockSpec((att_w, tm), col),
            pl.BlockSpec((tm, att_w), row),
            pl.BlockSpec((1, heads, 1, da, tm), va_idx),
            pl.BlockSpec((tm, LANES), row),
        ],
        out_shape=[
            jax.ShapeDtypeStruct((t, d), F32),
            jax.ShapeDtypeStruct((t, ssm_w), F32),
            jax.ShapeDtypeStruct((att_w, t), BF16),
            jax.ShapeDtypeStruct((t, att_w), BF16),
            jax.ShapeDtypeStruct((t // seq, heads, seq // tq, da, tq), BF16),
            jax.ShapeDtypeStruct((t, LANES), F32),
        ],
        compiler_params=_params(("parallel",), 56),
        name="ln_inproj",
    )(x2d, g, b, w_bf, wqvt_bf, bf_pad)


def _s5_tables(a_re, a_im, log_dt, b_re, b_im, c_re, c_im, d_skip, levels):
    c = S5_CHUNK
    g, p = a_re.shape
    hc = SSM_GROUP_CH
    lam = lax.complex(a_re.astype(F32), a_im.astype(F32))
    dt = jnp.exp(log_dt.astype(F32))[:, None]
    ldt = lam * dt
    abar = jnp.exp(ldt)
    bbar = ((abar - 1.0) / lam)[..., None] * lax.complex(b_re.astype(F32), b_im.astype(F32))
    cc = lax.complex(c_re.astype(F32), c_im.astype(F32))
    tau = jnp.arange(c + 1, dtype=F32)
    apow = jnp.exp(ldt[:, None, :] * tau[None, :, None])

    ca = cc[:, None, :, :] * apow[:, :c, None, :]
    taps = (jnp.einsum('gthp,gpk->gthk', ca.real, bbar.real, precision=HIGHEST)
            - jnp.einsum('gthp,gpk->gthk', ca.imag, bbar.imag, precision=HIGHEST))
    taps = taps + (jnp.eye(hc, dtype=F32) * d_skip.astype(F32)[:, :, None])[:, None] * (jnp.arange(c) == 0)[None, :, None, None]
    lagmat = taps.transpose(0, 3, 1, 2).reshape(g, hc, c * hc)

    bp = apow[:, c - 1 - jnp.arange(c), :][:, :, :, None] * bbar[:, None, :, :]
    bp = bp.transpose(0, 1, 3, 2).reshape(g, c * hc, p)
    bp_t = jnp.concatenate([bp.real, bp.imag], axis=-1)

    cp = cc[:, None, :, :] * apow[:, 1:c + 1, None, :]
    cp = cp.reshape(g, c * hc, p).transpose(0, 2, 1)
    cp_t = jnp.concatenate([cp.real, -cp.imag], axis=1)

    steps = (c * (2.0 ** jnp.arange(levels, dtype=F32)))
    alev = jnp.exp(ldt[:, None, :] * steps[None, :, None])

    gl = LANES // hc
    nj = g // gl
    lag_rows = lagmat.reshape(nj, gl * hc, c * hc)
    bp_rows = bp_t.reshape(nj, gl, c, hc, 2 * p).transpose(0, 2, 1, 3, 4).reshape(nj, c * LANES, 2 * p)
    cp_rows = cp_t.reshape(nj, gl, 2, p, c * hc).transpose(0, 2, 1, 3, 4).reshape(nj, 2 * gl * p, c * hc)

    def selector(n_outer, inner):
        cols, wcols = n_outer * inner, n_outer * gl * inner
        src, dst = jnp.arange(cols)[:, None], jnp.arange(wcols)[None, :]
        return ((src // inner == dst // (gl * inner)) & (src % inner == dst % inner)).astype(BF16)

    al = alev.reshape(nj, gl, levels, p).transpose(0, 2, 1, 3).reshape(nj, levels, gl * p)
    pad = ((0, 0), (0, 16 - levels), (0, 0))
    a1 = jnp.pad(jnp.concatenate([al.real, al.real], axis=-1), pad)
    a2 = jnp.pad(jnp.concatenate([-al.imag, al.imag], axis=-1), pad)
    return (lag_rows, bp_rows.astype(BF16), cp_rows.astype(BF16), selector(c, hc), selector(2, p), a1, a2)


def _shift_rows(x, sh):
    n = x.shape[0]
    if sh % 8 == 0:
        return jnp.concatenate([jnp.zeros((sh, x.shape[1]), x.dtype), x[:n - sh]], axis=0)
    rolled = pltpu.roll(x, sh, axis=0)
    rows = lax.broadcasted_iota(jnp.int32, x.shape, 0)
    return jnp.where(rows < sh, 0.0, rolled)


def _cmul(a1, a2, x):
    return a1 * x + a2 * pltpu.roll(x, x.shape[1] // 2, axis=1)


def _spread_block_diag(rows, sel_ref, out_scr, row_span, col_span):
    gl = LANES // SSM_GROUP_CH
    n_rows, width = out_scr.shape
    step = 2 * LANES
    r_grp = (lax.broadcasted_iota(jnp.int32, (n_rows, step), 0) >> (row_span.bit_length() - 1)) & (gl - 1)
    for c0 in range(0, width, step):
        wide = _dot(rows, sel_ref[:, c0:c0 + step])
        w_grp = ((lax.broadcasted_iota(jnp.int32, (n_rows, step), 1) + c0) >> (col_span.bit_length() - 1)) & (gl - 1)
        out_scr[:, c0:c0 + step] = jnp.where(r_grp == w_grp, wide, 0.0).astype(BF16)


def _s5_kernel(u_ref, um_ref, lag_ref, bprow_ref, cprow_ref, selm_ref, selb_ref, a1_ref, a2_ref, y_ref,
               mrow_scr, mj_scr, bpj_scr, cpj_scr, *, nc, levels):
    c = S5_CHUNK
    hc = SSM_GROUP_CH

    @pl.when(pl.program_id(1) == 0)
    def _():
        lag = lag_ref[0]
        col = lax.broadcasted_iota(jnp.int32, lag.shape, 1)
        for s in range(c):
            blk = lag if s == 0 else jnp.where(col >= s * hc, pltpu.roll(lag, s * hc, axis=1), 0.0)
            mrow_scr[s * LANES:(s + 1) * LANES, :] = blk.astype(BF16)
        _spread_block_diag(mrow_scr[...], selm_ref, mj_scr, hc, hc)
        _spread_block_diag(bprow_ref[0], selb_ref, bpj_scr, hc, SSM_STATE)
        _spread_block_diag(cprow_ref[0], selm_ref, cpj_scr, SSM_STATE, hc)

    u = jnp.concatenate([u_ref[pl.ds(s, nc, stride=c), :].astype(BF16) for s in range(c)], axis=1)
    bpj = bpj_scr[...]
    w = _dot(u, bpj)
    um = jnp.concatenate([um_ref[s:s + 1, :] for s in range(c)], axis=1)
    x_meta = _dot(jnp.broadcast_to(um, (8, um.shape[1])).astype(BF16), bpj)
    first = lax.broadcasted_iota(jnp.int32, x_meta.shape, 0) == 0
    inject = jnp.where(first, _cmul(a1_ref[0, 0:1, :], a2_ref[0, 0:1, :], x_meta), 0.0)
    x = jnp.concatenate([w[:8] + inject, w[8:]], axis=0)
    for lvl in range(levels):
        x = x + _cmul(a1_ref[0, lvl:lvl + 1, :], a2_ref[0, lvl:lvl + 1, :], _shift_rows(x, 2 ** lvl))
    xp = _shift_rows(x, 1)
    x_in = jnp.concatenate([xp[:8] + jnp.where(first, x_meta, 0.0), xp[8:]], axis=0).astype(BF16)
    y = jax.nn.gelu(_dot(u, mj_scr[...]) + _dot(x_in, cpj_scr[...]))
    for s in range(c):
        y_ref[pl.ds(s, nc, stride=c), :] = y[:, s * LANES:(s + 1) * LANES]


def _s5(u, u_m, lag_rows, bp_rows, cp_rows, sel_m, sel_b, a1, a2, bsz, nc, levels):
    t, ssm_w = u.shape
    nj, cl, cb = bp_rows.shape
    sw, cm = cp_rows.shape[1], cp_rows.shape[2]
    nl = a1.shape[1]
    rows = nc * S5_CHUNK
    tab = lambda j, b: (j, 0, 0)
    return pl.pallas_call(
        functools.partial(_s5_kernel, nc=nc, levels=levels),
        grid=(nj, bsz),
        in_specs=[
            pl.BlockSpec((rows, LANES), lambda j, b: (b, j)),
            pl.BlockSpec((S5_CHUNK, LANES), lambda j, b: (0, j)),
            pl.BlockSpec((1, LANES, cm), tab),
            pl.BlockSpec((1, cl, cb), tab),
            pl.BlockSpec((1, sw, cm), tab),
            _const_spec((cm, cl)), _const_spec((cb, sw)),
            pl.BlockSpec((1, nl, sw), tab),
            pl.BlockSpec((1, nl, sw), tab),
        ],
        out_specs=pl.BlockSpec((rows, LANES), lambda j, b: (b, j)),
        out_shape=jax.ShapeDtypeStruct((t, ssm_w), F32),
        scratch_shapes=[pltpu.VMEM((cl, cm), BF16),
                        pltpu.VMEM((cl, cl), BF16), pltpu.VMEM((cl, sw), BF16), pltpu.VMEM((sw, cl), BF16)],
        compiler_params=_params(("parallel", "arbitrary"), 56),
        name="s5_scan",
    )(u, u_m, lag_rows, bp_rows, cp_rows, sel_m, sel_b, a1, a2)


def _fox_kernel(qt_ref, k_ref, va_ref, ck_ref, km_ref, vam_ref, ckm_ref, o_ref,
                m_scr, acc_scr, s_scr, bm_scr, p_scr, al_scr, *, tq, dh):
    qi = pl.program_id(2)
    qt = qt_ref[...]
    reps = tq // LANES

    def scores(kb, ckb):
        return _dot(kb, qt) - jnp.concatenate([ckb] * reps, axis=1)

    def put_scores(slot, s):
        s_scr[slot] = s
        bm_scr[slot] = jnp.broadcast_to(jnp.max(s, axis=0, keepdims=True), (8, tq))

    def softmax_update(slot):
        m_prev = m_scr[0:1, :]
        m_new = jnp.maximum(m_prev, bm_scr[slot, 0:1, :])
        m_scr[...] = jnp.broadcast_to(m_new, m_scr.shape)
        return jnp.exp2(m_prev - m_new), jnp.exp2((s_scr[slot] - m_new).astype(BF16))

    s = scores(km_ref[...], ckm_ref[0])
    m0 = jnp.max(s, axis=0, keepdims=True)
    m_scr[...] = jnp.broadcast_to(m0, m_scr.shape)
    acc_scr[...] = _dot(vam_ref[0], jnp.exp2(s - m0).astype(BF16))

    start = pl.multiple_of(qi * tq, tq)
    s = scores(k_ref[pl.ds(start, tq), :], ck_ref[0, 0, pl.ds(start, tq), :])
    key = lax.broadcasted_iota(jnp.int32, (tq, tq), 0)
    qry = lax.broadcasted_iota(jnp.int32, (tq, tq), 1)
    ns = FOX_SLOTS
    put_scores(ns - 1, jnp.where(key <= qry, s, NEG_BIG))
    p_scr[ns - 2] = jnp.zeros((tq, tq), BF16)
    al_scr[ns - 2] = jnp.ones((8, tq), F32)

    def tick(t, u):
        sb, sc = (u - 1) % ns, (u - 2) % ns
        jc = jnp.where(t == 1, qi, jnp.clip(t - 2, 0, qi))
        acc_scr[...] = al_scr[sc, 0:1, :] * acc_scr[...] + _dot(va_ref[0, 0, jc], p_scr[sc])
        alpha, p = softmax_update(sb)
        p_scr[sb] = p
        al_scr[sb] = jnp.broadcast_to(alpha, (8, tq))
        ja = jnp.minimum(t, jnp.maximum(qi - 1, 0))
        start = pl.multiple_of(ja * tq, tq)
        off = jnp.where(t < qi, 0.0, -NEG_BIG)
        put_scores(u, scores(k_ref[pl.ds(start, tq), :], ck_ref[0, 0, pl.ds(start, tq), :] + off))

    def body(i, carry):
        for u in range(ns):
            tick(ns * i + u, u)
        return carry

    n_ticks = (qi + 3) // 2 * 2
    n_full = n_ticks // ns
    lax.fori_loop(0, n_full, body, 0)

    @pl.when(n_ticks % ns != 0)
    def _():
        tick(ns * n_full, 0)
        tick(ns * n_full + 1, 1)
    acc = acc_scr[...]
    o_ref[...] = (acc[:dh] / acc[dh:dh + 1]).T.astype(BF16)


def _fox(qt, k, va, ck, km, vam, ckm, bsz, seq, heads, tq):
    aw, t = qt.shape
    nq = seq // tq
    dh = ATT_HEAD_DIM
    da = va.shape[3]
    return pl.pallas_call(
        functools.partial(_fox_kernel, tq=tq, dh=dh),
        grid=(bsz, heads, nq),
        in_specs=[
            pl.BlockSpec((dh, tq), lambda b, h, i: (h, b * nq + i)),
            pl.BlockSpec((seq, dh), lambda b, h, i: (b, h)),
            pl.BlockSpec((1, 1, nq, da, tq), lambda b, h, i: (b, h, 0, 0, 0)),
            pl.BlockSpec((1, 1, seq, LANES), lambda b, h, i: (b, h, 0, 0)),
            pl.BlockSpec((LANES, dh), lambda b, h, i: (0, h)),
            pl.BlockSpec((1, da, LANES), lambda b, h, i: (h, 0, 0)),
            pl.BlockSpec((1, LANES, LANES), lambda b, h, i: (h, 0, 0)),
        ],
        out_specs=pl.BlockSpec((tq, dh), lambda b, h, i: (b * nq + i, h)),
        out_shape=jax.ShapeDtypeStruct((t, aw), BF16),
        scratch_shapes=[
            pltpu.VMEM((8, tq), F32),
            pltpu.VMEM((da, tq), F32),
            pltpu.VMEM((FOX_SLOTS, tq, tq), F32),
            pltpu.VMEM((FOX_SLOTS, 8, tq), F32),
            pltpu.VMEM((FOX_SLOTS, tq, tq), BF16),
            pltpu.VMEM((FOX_SLOTS, 8, tq), F32),
        ],
        compiler_params=_params(("parallel", "parallel", "arbitrary"), 48),
        name="fox_attention",
    )(qt, k, va, ck, km, vam, ckm)


def _mix_out_kernel(ys_ref, ya_ref, h_ref, wglu_ref, bglu_ref, wout_ref, g_ref, b_ref, wr_ref,
                    h2_ref, h2p_ref, lt_ref, *, ssm_w):
    ys = ys_ref[...]
    z = _dot(ys.astype(BF16), wglu_ref[...]) + bglu_ref[...]
    yg = (ys * jax.nn.sigmoid(z)).astype(BF16)
    mix = _dot(yg, wout_ref[0:ssm_w, :]) + _dot(ya_ref[...], wout_ref[ssm_w:, :])
    h2 = _layer_norm(DEEPNORM_ALPHA * h_ref[...] + mix, g_ref[...], b_ref[...])
    h2_ref[...] = h2
    _pack_rows(h2, h2p_ref)
    lt_ref[...] = _dot_nt(wr_ref[...], h2.astype(BF16))


def _mix_out(ys, ya, h, wglu, bglu, wout, g, b, wr_t, tm):
    t, d = h.shape
    ssm_w = ys.shape[1]
    att_w = ya.shape[1]
    e = wr_t.shape[0]
    row = lambda i: (i, 0)
    return pl.pallas_call(
        functools.partial(_mix_out_kernel, ssm_w=ssm_w),
        grid=(t // tm,),
        in_specs=[
            pl.BlockSpec((tm, ssm_w), row),
            pl.BlockSpec((tm, att_w), row),
            pl.BlockSpec((tm, d), row),
            _const_spec((ssm_w, ssm_w)), _const_spec((1, ssm_w)),
            _const_spec((d, d)), _const_spec((1, d)), _const_spec((1, d)),
            _const_spec((e, d)),
        ],
        out_specs=[
            pl.BlockSpec((tm, d), row),
            pl.BlockSpec((tm * ROW_TILE, LANES), row),
            pl.BlockSpec((e, tm), lambda i: (0, i)),
        ],
        out_shape=[
            jax.ShapeDtypeStruct((t, d), F32),
            jax.ShapeDtypeStruct((t * ROW_TILE, LANES), jnp.uint32),
            jax.ShapeDtypeStruct((e, t), F32),
        ],
        compiler_params=_params(("parallel",), 48),
        name="mix_out",
    )(ys, ya, h, wglu, bglu, wout, g, b, wr_t)


def _route_kernel(lt_ref, bias_ref, idx_ref, gate_ref, rank_ref, cnt_ref, run_scr, *, tn):
    e = N_EXPERTS
    ng = N_EXPERT_GROUPS
    gs = e // ng

    @pl.when(pl.program_id(0) == 0)
    def _():
        run_scr[...] = jnp.zeros_like(run_scr)

    scores = jax.nn.sigmoid(lt_ref[...])
    sel = scores + bias_ref[...]
    sel3 = sel.reshape(ng, gs, tn)
    mem = lax.broadcasted_iota(jnp.int32, (ng, gs, tn), 1)
    m1 = jnp.max(sel3, axis=1, keepdims=True)
    i1 = jnp.min(jnp.where(sel3 == m1, mem, gs), axis=1, keepdims=True)
    m2 = jnp.max(jnp.where(mem == i1, -jnp.inf, sel3), axis=1, keepdims=True)
    gscore = (m1 + m2).reshape(ng, tn)

    gio = lax.broadcasted_iota(jnp.int32, (ng, tn), 0)
    gmask = jnp.zeros((ng, tn), F32)
    for _ in range(TOPK_GROUPS):
        mx = jnp.max(gscore, axis=0, keepdims=True)
        gi = jnp.min(jnp.where(gscore == mx, gio, ng), axis=0, keepdims=True)
        hit = gio == gi
        gmask = jnp.where(hit, 1.0, gmask)
        gscore = jnp.where(hit, -jnp.inf, gscore)

    cand = jnp.where(gmask.reshape(ng, 1, tn) > 0.5, sel3, -jnp.inf).reshape(e, tn)
    eio = lax.broadcasted_iota(jnp.int32, (e, tn), 0)
    hits = []
    chosen = jnp.zeros((e, tn), F32)
    for _ in range(TOP_K):
        mx = jnp.max(cand, axis=0, keepdims=True)
        ei = jnp.min(jnp.where(cand == mx, eio, e), axis=0, keepdims=True)
        hit = eio == ei
        hits.append((ei, hit))
        chosen = jnp.where(hit, 1.0, chosen)
        cand = jnp.where(hit, -jnp.inf, cand)

    onehot = chosen.astype(BF16)
    r = lax.broadcasted_iota(jnp.int32, (tn, tn), 0)
    c = lax.broadcasted_iota(jnp.int32, (tn, tn), 1)
    tri = (r < c).astype(BF16)
    rank = _dot(onehot, tri) + run_scr[:, 0:1]
    run_scr[...] = run_scr[...] + jnp.sum(chosen, axis=1, keepdims=True)
    cnt_ref[...] = run_scr[...]

    gates = [jnp.sum(jnp.where(hit, scores, 0.0), axis=0, keepdims=True) for _, hit in hits]
    total = gates[0]
    for gk in gates[1:]:
        total = total + gk
    for kk, (ei, hit) in enumerate(hits):
        idx_ref[kk:kk + 1, :] = ei
        gate_ref[kk:kk + 1, :] = gates[kk] / total * ROUTED_SCALE
        rank_ref[kk:kk + 1, :] = jnp.sum(jnp.where(hit, rank, 0.0), axis=0, keepdims=True).astype(jnp.int32)


def _route(logits_t, bias, tn):
    e, t = logits_t.shape
    col = lambda i: (0, i)
    return pl.pallas_call(
        functools.partial(_route_kernel, tn=tn),
        grid=(t // tn,),
        in_specs=[pl.BlockSpec((e, tn), col), _const_spec((e, 1))],
        out_specs=[
            pl.BlockSpec((TOP_K, tn), col),
            pl.BlockSpec((TOP_K, tn), col),
            pl.BlockSpec((TOP_K, tn), col),
            pl.BlockSpec((e, LANES), lambda i: (0, 0)),
        ],
        out_shape=[
            jax.ShapeDtypeStruct((TOP_K, t), jnp.int32),
            jax.ShapeDtypeStruct((TOP_K, t), F32),
            jax.ShapeDtypeStruct((TOP_K, t), jnp.int32),
            jax.ShapeDtypeStruct((e, LANES), F32),
        ],
        scratch_shapes=[pltpu.VMEM((e, LANES), F32)],
        compiler_params=_params(("arbitrary",), 32),
        name="route",
    )(logits_t, bias)


def _tile_rows(ref, row):
    return ref.at[pl.ds(pl.multiple_of(row * ROW_TILE, ROW_TILE), ROW_TILE), :]


def _dispatch_kernel(ps_ref, pe_ref, tail_ref, pos_ref, x_ref, wgu_ref, wd_ref, xs_ref, sh_ref, zero_scr, sem,
                     *, tm, bm, n_blocks, ff):
    @pl.when(pl.program_id(0) == 0)
    def _():
        zero_scr[...] = jnp.zeros_like(zero_scr)

        def fill(start):
            def per_expert(e, carry):
                first = ps_ref[e]
                n_pad = pe_ref[e] - first
                size = bm // 2
                while size >= 1:
                    done = n_pad & ~(2 * size - 1)

                    @pl.when((n_pad & size) != 0)
                    def _(size=size, done=done):
                        rows = pl.ds(pl.multiple_of((first + done) * ROW_TILE, ROW_TILE), size * ROW_TILE)
                        cp = pltpu.make_async_copy(zero_scr.at[0:size * ROW_TILE, :], xs_ref.at[rows, :], sem)
                        cp.start() if start else cp.wait()

                    size //= 2
                return carry
            lax.fori_loop(0, N_EXPERTS, per_expert, 0)

            def per_block(b, c):
                rows = pl.ds(pl.multiple_of(b * bm * ROW_TILE, bm * ROW_TILE), bm * ROW_TILE)
                cp = pltpu.make_async_copy(zero_scr, xs_ref.at[rows, :], sem)
                cp.start() if start else cp.wait()
                return c
            lax.fori_loop(tail_ref[0], n_blocks, per_block, 0)

        fill(True)
        fill(False)

    def issue(t, carry):
        src = _tile_rows(x_ref, t)
        for kk in range(TOP_K):
            pltpu.make_async_copy(src, _tile_rows(xs_ref, pos_ref[kk, t]), sem).start()
        return carry

    lax.fori_loop(0, tm, issue, 0)
    lo, hi = _unpack_rows(x_ref, 0, tm)
    x = jnp.concatenate([p.astype(BF16) for p in lo + hi], axis=1)
    gu = _dot(x, wgu_ref[...])
    hid = jax.nn.silu(gu[:, :ff]) * gu[:, ff:]
    sh_ref[...] = _dot(hid.astype(BF16), wd_ref[...])
    for _ in range(TOP_K):
        pltpu.make_async_copy(x_ref, xs_ref.at[pl.ds(0, tm * ROW_TILE), :], sem).wait()


def _dispatch(pad_start, pad_end, tail_block, pos, h2p, wgu, wd, n_blocks, bm, tm):
    t = pos.shape[1]
    ff, d = wd.shape
    grid_spec = pltpu.PrefetchScalarGridSpec(
        num_scalar_prefetch=3,
        grid=(t // tm,),
        in_specs=[
            pl.BlockSpec((TOP_K, tm), lambda i, ps, pe, tl: (0, i), memory_space=pltpu.SMEM),
            pl.BlockSpec((tm * ROW_TILE, LANES), lambda i, ps, pe, tl: (i, 0)),
            _const_spec((d, 2 * ff)), _const_spec((ff, d)),
        ],
        out_specs=[pl.BlockSpec(memory_space=pl.ANY), pl.BlockSpec((tm, d), lambda i, ps, pe, tl: (i, 0))],
        scratch_shapes=[pltpu.VMEM((bm * ROW_TILE, LANES), jnp.uint32), pltpu.SemaphoreType.DMA(())],
    )
    return pl.pallas_call(
        functools.partial(_dispatch_kernel, tm=tm, bm=bm, n_blocks=n_blocks, ff=ff),
        grid_spec=grid_spec,
        out_shape=[jax.ShapeDtypeStruct((n_blocks * bm * ROW_TILE, LANES), jnp.uint32),
                   jax.ShapeDtypeStruct((t, d), F32)],
        compiler_params=_params(("arbitrary",), 40),
        name="dispatch",
    )(pad_start, pad_end, tail_block, pos, h2p, wgu, wd)


def _moe_kernel(be_ref, bv_ref, bf_ref, nx_ref, sl_ref, x_ref, wg_hbm, wu_hbm, wd_hbm, y_ref,
                wg_f, wu_f, wd_f, wg_s, wu_s, wd_s, sem, *, bm):
    b = pl.program_id(0)

    def weights(e, slot, start):
        for i, (src, dst) in enumerate(((wg_hbm, wg_f), (wu_hbm, wu_f), (wd_hbm, wd_f))):
            cp = pltpu.make_async_copy(src.at[e], dst.at[slot], sem.at[slot, i])
            cp.start() if start else cp.wait()

    @pl.when(bf_ref[b] == 1)
    def _():
        slot = sl_ref[b]

        @pl.when(b == 0)
        def _():
            weights(be_ref[0], slot, True)

        weights(be_ref[b], slot, False)

        @pl.when(nx_ref[b] >= 0)
        def _():
            weights(nx_ref[b], 1 - slot, True)

        wg_s[...] = wg_f[slot].astype(BF16)
        wu_s[...] = wu_f[slot].astype(BF16)
        wd_s[...] = wd_f[slot].astype(BF16)

    @pl.when(bv_ref[b] == 1)
    def _():
        lo, hi = _unpack_rows(x_ref, 0, bm)
        x = jnp.concatenate([p.astype(BF16) for p in lo + hi], axis=1)
        hid = jax.nn.silu(_dot(x, wg_s[...])) * _dot(x, wu_s[...])
        _pack_rows(_dot(hid.astype(BF16), wd_s[...]), y_ref)

    @pl.when(bv_ref[b] == 0)
    def _():
        y_ref[...] = jnp.zeros_like(y_ref)


def _moe_grouped(block_e, block_valid, block_first, next_e, slot_of, xs, wg, wu, wd, bm):
    rows = xs.shape[0] // ROW_TILE
    d, ff = wg.shape[1], wg.shape[2]
    n_blocks = rows // bm
    blk = lambda b, *_: (b, 0)
    grid_spec = pltpu.PrefetchScalarGridSpec(
        num_scalar_prefetch=5,
        grid=(n_blocks,),
        in_specs=[
            pl.BlockSpec((bm * ROW_TILE, LANES), blk),
            pl.BlockSpec(memory_space=pl.ANY), pl.BlockSpec(memory_space=pl.ANY), pl.BlockSpec(memory_space=pl.ANY),
        ],
        out_specs=pl.BlockSpec((bm * ROW_TILE, LANES), blk),
        scratch_shapes=[
            pltpu.VMEM((2, d, ff), F32), pltpu.VMEM((2, d, ff), F32), pltpu.VMEM((2, ff, d), F32),
            pltpu.VMEM((d, ff), BF16), pltpu.VMEM((d, ff), BF16), pltpu.VMEM((ff, d), BF16),
            pltpu.SemaphoreType.DMA((2, 3)),
        ],
    )
    return pl.pallas_call(
        functools.partial(_moe_kernel, bm=bm),
        grid_spec=grid_spec,
        out_shape=jax.ShapeDtypeStruct(xs.shape, jnp.uint32),
        compiler_params=_params(("arbitrary",), 56),
        name="moe_grouped",
    )(block_e, block_valid, block_first, next_e, slot_of, xs, wg, wu, wd)


def _final_kernel(pos_ref, posn_ref, h2_ref, sh_ref, gate_ref, g_ref, b_ref, ys_ref, o_ref, ybuf, sem, *, tm):
    i = pl.program_id(0)
    slot = i % 2
    tile = TOP_K * tm * ROW_TILE

    def gather(p_ref, dst_slot):
        def issue(t, carry):
            for kk in range(TOP_K):
                dst = ybuf.at[pl.ds(pl.multiple_of(dst_slot * tile + (kk * tm + t) * ROW_TILE, ROW_TILE), ROW_TILE), :]
                pltpu.make_async_copy(_tile_rows(ys_ref, p_ref[kk, t]), dst, sem.at[dst_slot]).start()
            return carry
        lax.fori_loop(0, tm, issue, 0)

    @pl.when(i == 0)
    def _():
        gather(pos_ref, slot)

    @pl.when(i + 1 < pl.num_programs(0))
    def _():
        gather(posn_ref, 1 - slot)

    base = pl.multiple_of(slot * tile, tile)
    for kk in range(TOP_K):
        pltpu.make_async_copy(ys_ref.at[pl.ds(0, tm * ROW_TILE), :],
                              ybuf.at[pl.ds(base + kk * tm * ROW_TILE, tm * ROW_TILE), :], sem.at[slot]).wait()
    gate = gate_ref[...]
    acc = None
    for kk in range(TOP_K):
        lo, hi = _unpack_rows(ybuf, base + kk * tm * ROW_TILE, tm)
        gk = gate[:, kk:kk + 1]
        terms = [gk * p for p in lo + hi]
        acc = terms if acc is None else [a + b for a, b in zip(acc, terms)]
    ffn = jnp.concatenate(acc, axis=1) + sh_ref[...]
    o_ref[...] = _layer_norm(DEEPNORM_ALPHA * h2_ref[...] + ffn, g_ref[...], b_ref[...])


def _final(pos, h2, shared, gate, ys, g, b, tm):
    t, d = h2.shape
    row = lambda i: (i, 0)
    return pl.pallas_call(
        functools.partial(_final_kernel, tm=tm),
        grid=(t // tm,),
        in_specs=[
            pl.BlockSpec((TOP_K, tm), lambda i: (0, i), memory_space=pltpu.SMEM),
            pl.BlockSpec((TOP_K, tm), lambda i: (0, jnp.minimum(i + 1, t // tm - 1)), memory_space=pltpu.SMEM),
            pl.BlockSpec((tm, d), row),
            pl.BlockSpec((tm, d), row),
            pl.BlockSpec((tm, TOP_K), row),
            _const_spec((1, d)), _const_spec((1, d)),
            pl.BlockSpec(memory_space=pl.ANY),
        ],
        out_specs=pl.BlockSpec((tm, d), row),
        out_shape=jax.ShapeDtypeStruct((t, d), F32),
        scratch_shapes=[pltpu.VMEM((2 * TOP_K * tm * ROW_TILE, LANES), jnp.uint32), pltpu.SemaphoreType.DMA((2,))],
        compiler_params=_params(("arbitrary",), 56),
        name="final",
    )(pos, pos, h2, shared, gate, g, b, ys)


def kernel(x, meta_tokens, ln_in_g, ln_in_b, w_in, b_forget, ssm_a_re, ssm_a_im, ssm_log_dt, ssm_b_re, ssm_b_im, ssm_c_re, ssm_c_im, ssm_d, w_glu, b_glu, w_out, ln_mix_g, ln_mix_b, w_router, router_bias, w_gate_exp, w_up_exp, w_down_exp, w_gate_sh, w_up_sh, w_down_sh, ln_ffn_g, ln_ffn_b):
    bsz, seq, d = x.shape
    t = bsz * seq
    ssm_w = w_glu.shape[1]
    heads = b_forget.shape[1]
    att_w = (w_in.shape[2] - ssm_w - heads) // 3
    groups = ssm_a_re.shape[1]
    assert meta_tokens.shape[0] == N_META == S5_CHUNK and att_w == heads * ATT_HEAD_DIM
    nc = seq // S5_CHUNK
    levels = int(math.log2(nc))
    assert 2 ** levels == nc and nc % 8 == 0
    row2 = lambda a: a.reshape(1, -1).astype(F32)

    w_u, w_q, w_k, w_v, w_f = jnp.split(w_in[0], [ssm_w, ssm_w + att_w, ssm_w + 2 * att_w, ssm_w + 3 * att_w], axis=1)
    w_f = jnp.pad(w_f, ((0, 0), (0, LANES - heads)))
    w_in_bf = jnp.concatenate([w_u, w_k, w_f], axis=1).astype(BF16)
    wqvt_bf = jnp.concatenate([w_q, w_v], axis=1).T.astype(BF16)
    bf_pad = jnp.pad(b_forget[0].astype(F32), (0, LANES - heads)).reshape(1, LANES)
    g_in, b_in = row2(ln_in_g), row2(ln_in_b)
    tm = min(256, t)
    tq = min(512, seq)
    h, u, qt, k, va, lf = _ln_inproj(x.reshape(t, d), g_in, b_in, w_in_bf, wqvt_bf, bf_pad, tm, ssm_w, att_w,
                                     seq, tq)
    _, u_m, _, k_m, va_m, lf_m = _ln_inproj(meta_tokens.astype(F32), g_in, b_in, w_in_bf, wqvt_bf, bf_pad, N_META,
                                            ssm_w, att_w, N_META, N_META)

    s5_tabs = _s5_tables(ssm_a_re[0], ssm_a_im[0], ssm_log_dt[0], ssm_b_re[0], ssm_b_im[0],
                         ssm_c_re[0], ssm_c_im[0], ssm_d[0], levels)
    y_ssm = _s5(u, u_m, *s5_tabs, bsz, nc, levels)

    lfm = lf_m[:, :heads] * LOG2E
    c_meta = jnp.cumsum(lfm, axis=0) - jnp.sum(lfm, axis=0, keepdims=True)
    ckm = jnp.full((heads, LANES), -NEG_BIG, F32).at[:, :N_META].set(c_meta.T)
    ckm = jnp.broadcast_to(ckm[:, :, None], (heads, LANES, LANES))
    c_main = jnp.cumsum(lf[:, :heads].reshape(bsz, seq, heads) * LOG2E, axis=1)
    ck = jnp.broadcast_to(c_main.transpose(0, 2, 1)[..., None], (bsz, heads, seq, LANES))
    km = jnp.pad(k_m, ((0, LANES - N_META), (0, 0)))
    vam = jnp.pad(va_m[0, :, 0], ((0, 0), (0, 0), (0, LANES - N_META)))
    y_att = _fox(qt, k, va, ck, km, vam, ckm, bsz, seq, heads, tq)

    h2, h2p, logits_t = _mix_out(
        y_ssm, y_att, h, w_glu[0].astype(BF16), row2(b_glu[0]), w_out[0].astype(BF16),
        row2(ln_mix_g[0]), row2(ln_mix_b[0]), w_router[0].T.astype(BF16), tm)

    tn = min(512, t)
    idx_t, gate_t, rank_t, counts = _route(logits_t, router_bias[0].astype(F32).reshape(N_EXPERTS, 1), tn)

    bm = 256
    counts = counts[:, 0].astype(jnp.int32)
    pcounts = (counts + bm - 1) // bm * bm
    pends = jnp.cumsum(pcounts)
    pstarts = pends - pcounts
    start_of = jnp.sum(jnp.where(idx_t[:, :, None] == jnp.arange(N_EXPERTS, dtype=jnp.int32), pstarts, 0), axis=-1)
    pos = (start_of + rank_t).astype(jnp.int32)
    n_blocks = t * TOP_K // bm + N_EXPERTS
    bstart = jnp.arange(n_blocks, dtype=jnp.int32) * bm
    block_e = jnp.minimum(jnp.sum(pends[None, :] <= bstart[:, None], axis=1), N_EXPERTS - 1).astype(jnp.int32)
    block_valid = (bstart < pends[-1]).astype(jnp.int32)
    block_first = block_valid * jnp.concatenate([jnp.ones((1,), jnp.int32),
                                                 (block_e[1:] != block_e[:-1]).astype(jnp.int32)])
    slot_of = ((jnp.cumsum(block_first) - 1) % 2).astype(jnp.int32)
    n_valid = pends[-1] // bm
    nxt = jnp.arange(n_blocks, dtype=jnp.int32) + pcounts[block_e] // bm
    next_e = jnp.where(nxt < n_valid, block_e[jnp.minimum(nxt, n_blocks - 1)], -1).astype(jnp.int32)
    wgu_sh = jnp.concatenate([w_gate_sh[0], w_up_sh[0]], axis=1).astype(BF16)
    xs, shared = _dispatch((pstarts + counts).astype(jnp.int32), pends.astype(jnp.int32),
                           (pends[-1:] // bm).astype(jnp.int32), pos, h2p, wgu_sh, w_down_sh[0].astype(BF16),
                           n_blocks, bm, tm)
    ys = _moe_grouped(block_e, block_valid, block_first, next_e, slot_of, xs,
                      w_gate_exp[0], w_up_exp[0], w_down_exp[0], bm)

    out = _final(pos, h2, shared, gate_t.T, ys, row2(ln_ffn_g[0]), row2(ln_ffn_b[0]), tm)
    return out.reshape(bsz, seq, d)
```

```python
import functools
import math

import jax
import jax.numpy as jnp
from jax import lax
from jax.experimental import pallas as pl
from jax.experimental.pallas import tpu as pltpu

N_META = 16
SSM_GROUP_CH = 16
SSM_STATE = 64
ATT_HEAD_DIM = 128
N_EXPERTS = 64
TOP_K = 8
N_EXPERT_GROUPS = 8
TOPK_GROUPS = 4
ROUTED_SCALE = 2.5
LN_EPS = 1e-5
DEPTH = 1
DEEPNORM_ALPHA = (2 * DEPTH) ** 0.25

S5_CHUNK = 16
FOX_SLOTS = 4
LANES = 128
ROW_TILE = 8
ONES_ROWS = 16
NEG_BIG = -1e30
LOG2E = 1.4426950408889634

F32 = jnp.float32
BF16 = jnp.bfloat16
HIGHEST = lax.Precision.HIGHEST


def _dot(a, b):
    return jnp.dot(a, b, preferred_element_type=F32)


def _dot_nt(a, b):
    return lax.dot_general(a, b, (((1,), (1,)), ((), ())), preferred_element_type=F32)


def _layer_norm(x, g, b):
    mu = jnp.mean(x, axis=-1, keepdims=True)
    xc = x - mu
    var = jnp.mean(xc * xc, axis=-1, keepdims=True)
    return xc * lax.rsqrt(var + LN_EPS) * g + b


def _pack_rows(x, o_ref):
    m, w = x.shape
    half = w // 2
    assert half == ROW_TILE * LANES
    bits = lax.bitcast_convert_type(x.astype(BF16).astype(F32), jnp.uint32)
    packed = bits[:, half:] | (bits[:, :half] >> 16)
    for s in range(ROW_TILE):
        o_ref[pl.ds(s, m, stride=ROW_TILE), :] = packed[:, s * LANES:(s + 1) * LANES]


def _unpack_rows(x_ref, base, m):
    lo, hi = [], []
    for s in range(ROW_TILE):
        w = x_ref[pl.ds(base + s, m, stride=ROW_TILE), :]
        lo.append(lax.bitcast_convert_type(w << 16, F32))
        hi.append(lax.bitcast_convert_type(w & jnp.uint32(0xFFFF0000), F32))
    return lo, hi


def _params(sem, vmem_mb):
    return pltpu.CompilerParams(dimension_semantics=sem, vmem_limit_bytes=vmem_mb * 1024 * 1024)


def _const_spec(shape):
    nd = len(shape)
    return pl.BlockSpec(shape, lambda *_: (0,) * nd, pipeline_mode=pl.Buffered(1))


def _ln_inproj_kernel(x_ref, g_ref, b_ref, w_ref, wqvt_ref, bf_ref, h_ref, u_ref, qt_ref, k_ref, va_ref, lf_ref,
                      *, ssm_w, att_w):
    h = _layer_norm(x_ref[...], g_ref[...], b_ref[...])
    h_ref[...] = h
    hb = h.astype(BF16)
    o = 0
    u_ref[...] = _dot(hb, w_ref[:, o:o + ssm_w])
    o += ssm_w
    k_ref[...] = _dot(hb, w_ref[:, o:o + att_w]).astype(BF16)
    o += att_w
    qvt = _dot_nt(wqvt_ref[...], hb)
    qt_ref[...] = (qvt[:att_w] * (LOG2E * ATT_HEAD_DIM ** -0.5)).astype(BF16)
    vt = qvt[att_w:].astype(BF16)
    dh = ATT_HEAD_DIM
    for hh in range(att_w // dh):
        va_ref[0, hh, 0, 0:dh, :] = vt[hh * dh:(hh + 1) * dh]
        va_ref[0, hh, 0, dh:, :] = jnp.ones((ONES_ROWS, vt.shape[1]), BF16)
    f = _dot(hb, w_ref[:, o:o + LANES]) + bf_ref[...]
    lf_ref[...] = jnp.minimum(f, 0.0) - jnp.log(1.0 + jnp.exp(-jnp.abs(f)))


def _ln_inproj(x2d, g, b, w_bf, wqvt_bf, bf_pad, tm, ssm_w, att_w, seq, tq):
    t, d = x2d.shape
    wcols = w_bf.shape[1]
    heads, da = att_w // ATT_HEAD_DIM, ATT_HEAD_DIM + ONES_ROWS
    row = lambda i: (i, 0)
    col = lambda i: (0, i)
    per_seq, per_q = seq // tm, tq // tm
    va_idx = lambda i: (i // per_seq, 0, (i % per_seq) // per_q, 0, (i % per_seq) % per_q)
    return pl.pallas_call(
        functools.partial(_ln_inproj_kernel, ssm_w=ssm_w, att_w=att_w),
        grid=(t // tm,),
        in_specs=[
            pl.BlockSpec((tm, d), row),
            _const_spec((1, d)), _const_spec((1, d)),
            _const_spec((d, wcols)), _const_spec((2 * att_w, d)), _const_spec((1, LANES)),
        ],
        out_specs=[
            pl.BlockSpec((tm, d), row),
            pl.BlockSpec((tm, ssm_w), row),
            pl.BlockSpec((att_w, tm), col),
            pl.BlockSpec((tm, att_w), row),
            pl.BlockSpec((1, heads, 1, da, tm), va_idx),
            pl.BlockSpec((tm, LANES), row),
        ],
        out_shape=[
            jax.ShapeDtypeStruct((t, d), F32),
            jax.ShapeDtypeStruct((t, ssm_w), F32),
            jax.ShapeDtypeStruct((att_w, t), BF16),
            jax.ShapeDtypeStruct((t, att_w), BF16),
            jax.ShapeDtypeStruct((t // seq, heads, seq // tq, da, tq), BF16),
            jax.ShapeDtypeStruct((t, LANES), F32),
        ],
        compiler_params=_params(("parallel",), 56),
        name="ln_inproj",
    )(x2d, g, b, w_bf, wqvt_bf, bf_pad)


def _s5_tables(a_re, a_im, log_dt, b_re, b_im, c_re, c_im, d_skip, levels):
    c = S5_CHUNK
    g, p = a_re.shape
    hc = SSM_GROUP_CH
    lam = lax.complex(a_re.astype(F32), a_im.astype(F32))
    dt = jnp.exp(log_dt.astype(F32))[:, None]
    ldt = lam * dt
    abar = jnp.exp(ldt)
    bbar = ((abar - 1.0) / lam)[..., None] * lax.complex(b_re.astype(F32), b_im.astype(F32))
    cc = lax.complex(c_re.astype(F32), c_im.astype(F32))
    tau = jnp.arange(c + 1, dtype=F32)
    apow = jnp.exp(ldt[:, None, :] * tau[None, :, None])

    ca = cc[:, None, :, :] * apow[:, :c, None, :]
    taps = jnp.einsum('gthp,gpk->gthk', jnp.concatenate([ca.real, -ca.imag], axis=-1),
                      jnp.concatenate([bbar.real, bbar.imag], axis=1), precision=HIGHEST)
    taps = taps + (jnp.eye(hc, dtype=F32) * d_skip.astype(F32)[:, :, None])[:, None] * (jnp.arange(c) == 0)[None, :, None, None]
    lagmat = taps.transpose(0, 3, 1, 2).reshape(g, hc, c * hc)

    bp = apow[:, c - 1 - jnp.arange(c), :][:, :, :, None] * bbar[:, None, :, :]
    bp = bp.transpose(0, 1, 3, 2).reshape(g, c * hc, p)
    bp_t = jnp.concatenate([bp.real, bp.imag], axis=-1)

    cp = cc[:, None, :, :] * apow[:, 1:c + 1, None, :]
    cp = cp.reshape(g, c * hc, p).transpose(0, 2, 1)
    cp_t = jnp.concatenate([cp.real, -cp.imag], axis=1)

    steps = (c * (2.0 ** jnp.arange(levels, dtype=F32)))
    alev = jnp.exp(ldt[:, None, :] * steps[None, :, None])

    gl = LANES // hc
    nj = g // gl
    lag_rows = lagmat.reshape(nj, gl * hc, c * hc)
    bp_rows = bp_t.reshape(nj, gl, c, hc, 2 * p).transpose(0, 2, 1, 3, 4).reshape(nj, c * LANES, 2 * p)
    cp_rows = cp_t.reshape(nj, gl, 2, p, c * hc).transpose(0, 2, 1, 3, 4).reshape(nj, 2 * gl * p, c * hc)

    def selector(n_outer, inner):
        cols, wcols = n_outer * inner, n_outer * gl * inner
        src, dst = jnp.arange(cols)[:, None], jnp.arange(wcols)[None, :]
        return ((src // inner == dst // (gl * inner)) & (src % inner == dst % inner)).astype(BF16)

    al = alev.reshape(nj, gl, levels, p).transpose(0, 2, 1, 3).reshape(nj, levels, gl * p)
    pad = ((0, 0), (0, 16 - levels), (0, 0))
    a1 = jnp.pad(jnp.concatenate([al.real, al.real], axis=-1), pad)
    a2 = jnp.pad(jnp.concatenate([-al.imag, al.imag], axis=-1), pad)
    return (lag_rows, bp_rows.astype(BF16), cp_rows.astype(BF16), selector(c, hc), selector(2, p), a1, a2)


def _shift_rows(x, sh):
    n = x.shape[0]
    if sh % 8 == 0:
        return jnp.concatenate([jnp.zeros((sh, x.shape[1]), x.dtype), x[:n - sh]], axis=0)
    rolled = pltpu.roll(x, sh, axis=0)
    rows = lax.broadcasted_iota(jnp.int32, x.shape, 0)
    return jnp.where(rows < sh, 0.0, rolled)


def _cmul(a1, a2, x):
    return a1 * x + a2 * pltpu.roll(x, x.shape[1] // 2, axis=1)


def _spread_block_diag(rows, sel_ref, out_scr, row_span, col_span):
    gl = LANES // SSM_GROUP_CH
    n_rows, width = out_scr.shape
    step = 2 * LANES
    r_grp = (lax.broadcasted_iota(jnp.int32, (n_rows, step), 0) >> (row_span.bit_length() - 1)) & (gl - 1)
    for c0 in range(0, width, step):
        wide = _dot(rows, sel_ref[:, c0:c0 + step])
        w_grp = ((lax.broadcasted_iota(jnp.int32, (n_rows, step), 1) + c0) >> (col_span.bit_length() - 1)) & (gl - 1)
        out_scr[:, c0:c0 + step] = jnp.where(r_grp == w_grp, wide, 0.0).astype(BF16)


def _s5_kernel(u_ref, um_ref, lag_ref, bprow_ref, cprow_ref, selm_ref, selb_ref, a1_ref, a2_ref, y_ref,
               mrow_scr, mj_scr, bpj_scr, cpj_scr, *, nc, levels):
    c = S5_CHUNK
    hc = SSM_GROUP_CH

    @pl.when(pl.program_id(1) == 0)
    def _():
        lag = lag_ref[0]
        col = lax.broadcasted_iota(jnp.int32, lag.shape, 1)
        for s in range(c):
            blk = lag if s == 0 else jnp.where(col >= s * hc, pltpu.roll(lag, s * hc, axis=1), 0.0)
            mrow_scr[s * LANES:(s + 1) * LANES, :] = blk.astype(BF16)
        _spread_block_diag(mrow_scr[...], selm_ref, mj_scr, hc, hc)
        _spread_block_diag(bprow_ref[0], selb_ref, bpj_scr, hc, SSM_STATE)
        _spread_block_diag(cprow_ref[0], selm_ref, cpj_scr, SSM_STATE, hc)

    u = jnp.concatenate([u_ref[pl.ds(s, nc, stride=c), :].astype(BF16) for s in range(c)], axis=1)
    bpj = bpj_scr[...]
    w = _dot(u, bpj)
    um = jnp.concatenate([um_ref[s:s + 1, :] for s in range(c)], axis=1)
    x_meta = _dot(jnp.broadcast_to(um, (8, um.shape[1])).astype(BF16), bpj)
    first = lax.broadcasted_iota(jnp.int32, x_meta.shape, 0) == 0
    inject = jnp.where(first, _cmul(a1_ref[0, 0:1, :], a2_ref[0, 0:1, :], x_meta), 0.0)
    x = jnp.concatenate([w[:8] + inject, w[8:]], axis=0)
    for lvl in range(levels):
        x = x + _cmul(a1_ref[0, lvl:lvl + 1, :], a2_ref[0, lvl:lvl + 1, :], _shift_rows(x, 2 ** lvl))
    xp = _shift_rows(x, 1)
    x_in = jnp.concatenate([xp[:8] + jnp.where(first, x_meta, 0.0), xp[8:]], axis=0).astype(BF16)
    y = jax.nn.gelu(_dot(u, mj_scr[...]) + _dot(x_in, cpj_scr[...]))
    for s in range(c):
        y_ref[pl.ds(s, nc, stride=c), :] = y[:, s * LANES:(s + 1) * LANES]


def _s5(u, u_m, lag_rows, bp_rows, cp_rows, sel_m, sel_b, a1, a2, bsz, nc, levels):
    t, ssm_w = u.shape
    nj, cl, cb = bp_rows.shape
    sw, cm = cp_rows.shape[1], cp_rows.shape[2]
    nl = a1.shape[1]
    rows = nc * S5_CHUNK
    tab = lambda j, b: (j, 0, 0)
    return pl.pallas_call(
        functools.partial(_s5_kernel, nc=nc, levels=levels),
        grid=(nj, bsz),
        in_specs=[
            pl.BlockSpec((rows, LANES), lambda j, b: (b, j)),
            pl.BlockSpec((S5_CHUNK, LANES), lambda j, b: (0, j)),
            pl.BlockSpec((1, LANES, cm), tab),
            pl.BlockSpec((1, cl, cb), tab),
            pl.BlockSpec((1, sw, cm), tab),
            _const_spec((cm, cl)), _const_spec((cb, sw)),
            pl.BlockSpec((1, nl, sw), tab),
            pl.BlockSpec((1, nl, sw), tab),
        ],
        out_specs=pl.BlockSpec((rows, LANES), lambda j, b: (b, j)),
        out_shape=jax.ShapeDtypeStruct((t, ssm_w), F32),
        scratch_shapes=[pltpu.VMEM((cl, cm), BF16),
                        pltpu.VMEM((cl, cl), BF16), pltpu.VMEM((cl, sw), BF16), pltpu.VMEM((sw, cl), BF16)],
        compiler_params=_params(("parallel", "arbitrary"), 56),
        name="s5_scan",
    )(u, u_m, lag_rows, bp_rows, cp_rows, sel_m, sel_b, a1, a2)


def _fox_kernel(qt_ref, k_ref, va_ref, ck_ref, km_ref, vam_ref, ckm_ref, o_ref,
                m_scr, acc_scr, s_scr, bm_scr, p_scr, al_scr, *, tq, dh):
    qi = pl.program_id(2)
    qt = qt_ref[...]
    reps = tq // LANES

    def scores(kb, ckb):
        return _dot(kb, qt) - jnp.concatenate([ckb] * reps, axis=1)

    def put_scores(slot, s):
        s_scr[slot] = s
        bm_scr[slot] = jnp.broadcast_to(jnp.max(s, axis=0, keepdims=True), (8, tq))

    def softmax_update(slot):
        m_prev = m_scr[0:1, :]
        m_new = jnp.maximum(m_prev, bm_scr[slot, 0:1, :])
        m_scr[...] = jnp.broadcast_to(m_new, m_scr.shape)
        return jnp.exp2(m_prev - m_new), jnp.exp2((s_scr[slot] - m_new).astype(BF16))

    s = scores(km_ref[...], ckm_ref[0])
    m0 = jnp.max(s, axis=0, keepdims=True)
    m_scr[...] = jnp.broadcast_to(m0, m_scr.shape)
    acc_scr[...] = _dot(vam_ref[0], jnp.exp2(s - m0).astype(BF16))

    start = pl.multiple_of(qi * tq, tq)
    s = scores(k_ref[pl.ds(start, tq), :], ck_ref[0, 0, pl.ds(start, tq), :])
    key = lax.broadcasted_iota(jnp.int32, (tq, tq), 0)
    qry = lax.broadcasted_iota(jnp.int32, (tq, tq), 1)
    ns = FOX_SLOTS
    put_scores(ns - 1, jnp.where(key <= qry, s, NEG_BIG))
    p_scr[ns - 2] = jnp.zeros((tq, tq), BF16)
    al_scr[ns - 2] = jnp.ones((8, tq), F32)

    def tick(t, u):
        sb, sc = (u - 1) % ns, (u - 2) % ns
        jc = jnp.where(t == 1, qi, jnp.clip(t - 2, 0, qi))
        acc_scr[...] = al_scr[sc, 0:1, :] * acc_scr[...] + _dot(va_ref[0, 0, jc], p_scr[sc])
        alpha, p = softmax_update(sb)
        p_scr[sb] = p
        al_scr[sb] = jnp.broadcast_to(alpha, (8, tq))
        ja = jnp.minimum(t, jnp.maximum(qi - 1, 0))
        start = pl.multiple_of(ja * tq, tq)
        off = jnp.where(t < qi, 0.0, -NEG_BIG)
        put_scores(u, scores(k_ref[pl.ds(start, tq), :], ck_ref[0, 0, pl.ds(start, tq), :] + off))

    def body(i, carry):
        for u in range(ns):
            tick(ns * i + u, u)
        return carry

    n_ticks = (qi + 3) // 2 * 2
    n_full = n_ticks // ns
    lax.fori_loop(0, n_full, body, 0)

    @pl.when(n_ticks % ns != 0)
    def _():
        tick(ns * n_full, 0)
        tick(ns * n_full + 1, 1)
    acc = acc_scr[...]
    o_ref[...] = (acc[:dh] / acc[dh:dh + 1]).T.astype(BF16)


def _fox(qt, k, va, ck, km, vam, ckm, bsz, seq, heads, tq):
    aw, t = qt.shape
    nq = seq // tq
    dh = ATT_HEAD_DIM
    da = va.shape[3]
    return pl.pallas_call(
        functools.partial(_fox_kernel, tq=tq, dh=dh),
        grid=(bsz, heads, nq),
        in_specs=[
            pl.BlockSpec((dh, tq), lambda b, h, i: (h, b * nq + i)),
            pl.BlockSpec((seq, dh), lambda b, h, i: (b, h)),
            pl.BlockSpec((1, 1, nq, da, tq), lambda b, h, i: (b, h, 0, 0, 0)),
            pl.BlockSpec((1, 1, seq, LANES), lambda b, h, i: (b, h, 0, 0)),
            pl.BlockSpec((LANES, dh), lambda b, h, i: (0, h)),
            pl.BlockSpec((1, da, LANES), lambda b, h, i: (h, 0, 0)),
            pl.BlockSpec((1, LANES, LANES), lambda b, h, i: (h, 0, 0)),
        ],
        out_specs=pl.BlockSpec((tq, dh), lambda b, h, i: (b * nq + i, h)),
        out_shape=jax.ShapeDtypeStruct((t, aw), BF16),
        scratch_shapes=[
            pltpu.VMEM((8, tq), F32),
            pltpu.VMEM((da, tq), F32),
            pltpu.VMEM((FOX_SLOTS, tq, tq), F32),
            pltpu.VMEM((FOX_SLOTS, 8, tq), F32),
            pltpu.VMEM((FOX_SLOTS, tq, tq), BF16),
            pltpu.VMEM((FOX_SLOTS, 8, tq), F32),
        ],
        compiler_params=_params(("parallel", "parallel", "arbitrary"), 48),
        name="fox_attention",
    )(qt, k, va, ck, km, vam, ckm)


def _mix_out_kernel(ys_ref, ya_ref, h_ref, wglu_ref, bglu_ref, wout_ref, g_ref, b_ref, wr_ref,
                    h2_ref, h2p_ref, lt_ref, *, ssm_w):
    ys = ys_ref[...]
    z = _dot(ys.astype(BF16), wglu_ref[...]) + bglu_ref[...]
    yg = (ys * jax.nn.sigmoid(z)).astype(BF16)
    mix = _dot(yg, wout_ref[0:ssm_w, :]) + _dot(ya_ref[...], wout_ref[ssm_w:, :])
    h2 = _layer_norm(DEEPNORM_ALPHA * h_ref[...] + mix, g_ref[...], b_ref[...])
    h2_ref[...] = h2
    _pack_rows(h2, h2p_ref)
    lt_ref[...] = _dot_nt(wr_ref[...], h2.astype(BF16))


def _mix_out(ys, ya, h, wglu, bglu, wout, g, b, wr_t, tm):
    t, d = h.shape
    ssm_w = ys.shape[1]
    att_w = ya.shape[1]
    e = wr_t.shape[0]
    row = lambda i: (i, 0)
    return pl.pallas_call(
        functools.partial(_mix_out_kernel, ssm_w=ssm_w),
        grid=(t // tm,),
        in_specs=[
            pl.BlockSpec((tm, ssm_w), row),
            pl.BlockSpec((tm, att_w), row),
            pl.BlockSpec((tm, d), row),
            _const_spec((ssm_w, ssm_w)), _const_spec((1, ssm_w)),
            _const_spec((d, d)), _const_spec((1, d)), _const_spec((1, d)),
            _const_spec((e, d)),
        ],
        out_specs=[
            pl.BlockSpec((tm, d), row),
            pl.BlockSpec((tm * ROW_TILE, LANES), row),
            pl.BlockSpec((e, tm), lambda i: (0, i)),
        ],
        out_shape=[
            jax.ShapeDtypeStruct((t, d), F32),
            jax.ShapeDtypeStruct((t * ROW_TILE, LANES), jnp.uint32),
            jax.ShapeDtypeStruct((e, t), F32),
        ],
        compiler_params=_params(("parallel",), 48),
        name="mix_out",
    )(ys, ya, h, wglu, bglu, wout, g, b, wr_t)


def _route_kernel(lt_ref, bias_ref, idx_ref, gate_ref, rank_ref, cnt_ref, run_scr, *, tn):
    e = N_EXPERTS
    ng = N_EXPERT_GROUPS
    gs = e // ng

    @pl.when(pl.program_id(0) == 0)
    def _():
        run_scr[...] = jnp.zeros_like(run_scr)

    scores = jax.nn.sigmoid(lt_ref[...])
    sel = scores + bias_ref[...]
    sel3 = sel.reshape(ng, gs, tn)
    mem = lax.broadcasted_iota(jnp.int32, (ng, gs, tn), 1)
    m1 = jnp.max(sel3, axis=1, keepdims=True)
    i1 = jnp.min(jnp.where(sel3 == m1, mem, gs), axis=1, keepdims=True)
    m2 = jnp.max(jnp.where(mem == i1, -jnp.inf, sel3), axis=1, keepdims=True)
    gscore = (m1 + m2).reshape(ng, tn)

    gio = lax.broadcasted_iota(jnp.int32, (ng, tn), 0)
    gmask = jnp.zeros((ng, tn), F32)
    for _ in range(TOPK_GROUPS):
        mx = jnp.max(gscore, axis=0, keepdims=True)
        gi = jnp.min(jnp.where(gscore == mx, gio, ng), axis=0, keepdims=True)
        hit = gio == gi
        gmask = jnp.where(hit, 1.0, gmask)
        gscore = jnp.where(hit, -jnp.inf, gscore)

    cand = jnp.where(gmask.reshape(ng, 1, tn) > 0.5, sel3, -jnp.inf).reshape(e, tn)
    eio = lax.broadcasted_iota(jnp.int32, (e, tn), 0)
    hits = []
    chosen = jnp.zeros((e, tn), F32)
    for _ in range(TOP_K):
        mx = jnp.max(cand, axis=0, keepdims=True)
        ei = jnp.min(jnp.where(cand == mx, eio, e), axis=0, keepdims=True)
        hit = eio == ei
        hits.append((ei, hit))
        chosen = jnp.where(hit, 1.0, chosen)
        cand = jnp.where(hit, -jnp.inf, cand)

    onehot = chosen.astype(BF16)
    r = lax.broadcasted_iota(jnp.int32, (tn, tn), 0)
    c = lax.broadcasted_iota(jnp.int32, (tn, tn), 1)
    tri = (r < c).astype(BF16)
    rank = _dot(onehot, tri) + run_scr[:, 0:1]
    run_scr[...] = run_scr[...] + jnp.sum(chosen, axis=1, keepdims=True)
    cnt_ref[...] = run_scr[...]

    gates = [jnp.sum(jnp.where(hit, scores, 0.0), axis=0, keepdims=True) for _, hit in hits]
    total = gates[0]
    for gk in gates[1:]:
        total = total + gk
    for kk, (ei, hit) in enumerate(hits):
        idx_ref[kk:kk + 1, :] = ei
        gate_ref[kk:kk + 1, :] = gates[kk] / total * ROUTED_SCALE
        rank_ref[kk:kk + 1, :] = jnp.sum(jnp.where(hit, rank, 0.0), axis=0, keepdims=True).astype(jnp.int32)


def _route(logits_t, bias, tn):
    e, t = logits_t.shape
    col = lambda i: (0, i)
    return pl.pallas_call(
        functools.partial(_route_kernel, tn=tn),
        grid=(t // tn,),
        in_specs=[pl.BlockSpec((e, tn), col), _const_spec((e, 1))],
        out_specs=[
            pl.BlockSpec((TOP_K, tn), col),
            pl.BlockSpec((TOP_K, tn), col),
            pl.BlockSpec((TOP_K, tn), col),
            pl.BlockSpec((e, LANES), lambda i: (0, 0)),
        ],
        out_shape=[
            jax.ShapeDtypeStruct((TOP_K, t), jnp.int32),
            jax.ShapeDtypeStruct((TOP_K, t), F32),
            jax.ShapeDtypeStruct((TOP_K, t), jnp.int32),
            jax.ShapeDtypeStruct((e, LANES), F32),
        ],
        scratch_shapes=[pltpu.VMEM((e, LANES), F32)],
        compiler_params=_params(("arbitrary",), 32),
        name="route",
    )(logits_t, bias)


def _tile_rows(ref, row):
    return ref.at[pl.ds(pl.multiple_of(row * ROW_TILE, ROW_TILE), ROW_TILE), :]


def _dispatch_kernel(ps_ref, pe_ref, tail_ref, pos_ref, x_ref, wgu_ref, wd_ref, xs_ref, sh_ref, zero_scr, sem,
                     *, tm, bm, n_blocks, ff):
    @pl.when(pl.program_id(0) == 0)
    def _():
        zero_scr[...] = jnp.zeros_like(zero_scr)

        def fill(start):
            def per_expert(e, carry):
                first = ps_ref[e]
                n_pad = pe_ref[e] - first
                size = bm // 2
                while size >= 1:
                    done = n_pad & ~(2 * size - 1)

                    @pl.when((n_pad & size) != 0)
                    def _(size=size, done=done):
                        rows = pl.ds(pl.multiple_of((first + done) * ROW_TILE, ROW_TILE), size * ROW_TILE)
                        cp = pltpu.make_async_copy(zero_scr.at[0:size * ROW_TILE, :], xs_ref.at[rows, :], sem)
                        cp.start() if start else cp.wait()

                    size //= 2
                return carry
            lax.fori_loop(0, N_EXPERTS, per_expert, 0)

            def per_block(b, c):
                rows = pl.ds(pl.multiple_of(b * bm * ROW_TILE, bm * ROW_TILE), bm * ROW_TILE)
                cp = pltpu.make_async_copy(zero_scr, xs_ref.at[rows, :], sem)
                cp.start() if start else cp.wait()
                return c
            lax.fori_loop(tail_ref[0], n_blocks, per_block, 0)

        fill(True)
        fill(False)

    def issue(t, carry):
        src = _tile_rows(x_ref, t)
        for kk in range(TOP_K):
            pltpu.make_async_copy(src, _tile_rows(xs_ref, pos_ref[kk, t]), sem).start()
        return carry

    lax.fori_loop(0, tm, issue, 0)
    lo, hi = _unpack_rows(x_ref, 0, tm)
    x = jnp.concatenate([p.astype(BF16) for p in lo + hi], axis=1)
    gu = _dot(x, wgu_ref[...])
    hid = jax.nn.silu(gu[:, :ff]) * gu[:, ff:]
    sh_ref[...] = _dot(hid.astype(BF16), wd_ref[...])
    for _ in range(TOP_K):
        pltpu.make_async_copy(x_ref, xs_ref.at[pl.ds(0, tm * ROW_TILE), :], sem).wait()


def _dispatch(pad_start, pad_end, tail_block, pos, h2p, wgu, wd, n_blocks, bm, tm):
    t = pos.shape[1]
    ff, d = wd.shape
    grid_spec = pltpu.PrefetchScalarGridSpec(
        num_scalar_prefetch=3,
        grid=(t // tm,),
        in_specs=[
            pl.BlockSpec((TOP_K, tm), lambda i, ps, pe, tl: (0, i), memory_space=pltpu.SMEM),
            pl.BlockSpec((tm * ROW_TILE, LANES), lambda i, ps, pe, tl: (i, 0)),
            _const_spec((d, 2 * ff)), _const_spec((ff, d)),
        ],
        out_specs=[pl.BlockSpec(memory_space=pl.ANY), pl.BlockSpec((tm, d), lambda i, ps, pe, tl: (i, 0))],
        scratch_shapes=[pltpu.VMEM((bm * ROW_TILE, LANES), jnp.uint32), pltpu.SemaphoreType.DMA(())],
    )
    return pl.pallas_call(
        functools.partial(_dispatch_kernel, tm=tm, bm=bm, n_blocks=n_blocks, ff=ff),
        grid_spec=grid_spec,
        out_shape=[jax.ShapeDtypeStruct((n_blocks * bm * ROW_TILE, LANES), jnp.uint32),
                   jax.ShapeDtypeStruct((t, d), F32)],
        compiler_params=_params(("arbitrary",), 40),
        name="dispatch",
    )(pad_start, pad_end, tail_block, pos, h2p, wgu, wd)


def _moe_kernel(be_ref, bv_ref, bf_ref, nx_ref, sl_ref, x_ref, wg_hbm, wu_hbm, wd_hbm, y_ref,
                wg_f, wu_f, wd_f, wg_s, wu_s, wd_s, sem, *, bm):
    b = pl.program_id(0)

    def weights(e, slot, start):
        for i, (src, dst) in enumerate(((wg_hbm, wg_f), (wu_hbm, wu_f), (wd_hbm, wd_f))):
            cp = pltpu.make_async_copy(src.at[e], dst.at[slot], sem.at[slot, i])
            cp.start() if start else cp.wait()

    @pl.when(bf_ref[b] == 1)
    def _():
        slot = sl_ref[b]

        @pl.when(b == 0)
        def _():
            weights(be_ref[0], slot, True)

        weights(be_ref[b], slot, False)

        @pl.when(nx_ref[b] >= 0)
        def _():
            weights(nx_ref[b], 1 - slot, True)

        wg_s[...] = wg_f[slot].astype(BF16)
        wu_s[...] = wu_f[slot].astype(BF16)
        wd_s[...] = wd_f[slot].astype(BF16)

    @pl.when(bv_ref[b] == 1)
    def _():
        lo, hi = _unpack_rows(x_ref, 0, bm)
        x = jnp.concatenate([p.astype(BF16) for p in lo + hi], axis=1)
        hid = jax.nn.silu(_dot(x, wg_s[...])) * _dot(x, wu_s[...])
        _pack_rows(_dot(hid.astype(BF16), wd_s[...]), y_ref)

    @pl.when(bv_ref[b] == 0)
    def _():
        y_ref[...] = jnp.zeros_like(y_ref)


def _moe_grouped(block_e, block_valid, block_first, next_e, slot_of, xs, wg, wu, wd, bm):
    rows = xs.shape[0] // ROW_TILE
    d, ff = wg.shape[1], wg.shape[2]
    n_blocks = rows // bm
    blk = lambda b, *_: (b, 0)
    grid_spec = pltpu.PrefetchScalarGridSpec(
        num_scalar_prefetch=5,
        grid=(n_blocks,),
        in_specs=[
            pl.BlockSpec((bm * ROW_TILE, LANES), blk),
            pl.BlockSpec(memory_space=pl.ANY), pl.BlockSpec(memory_space=pl.ANY), pl.BlockSpec(memory_space=pl.ANY),
        ],
        out_specs=pl.BlockSpec((bm * ROW_TILE, LANES), blk),
        scratch_shapes=[
            pltpu.VMEM((2, d, ff), F32), pltpu.VMEM((2, d, ff), F32), pltpu.VMEM((2, ff, d), F32),
            pltpu.VMEM((d, ff), BF16), pltpu.VMEM((d, ff), BF16), pltpu.VMEM((ff, d), BF16),
            pltpu.SemaphoreType.DMA((2, 3)),
        ],
    )
    return pl.pallas_call(
        functools.partial(_moe_kernel, bm=bm),
        grid_spec=grid_spec,
        out_shape=jax.ShapeDtypeStruct(xs.shape, jnp.uint32),
        compiler_params=_params(("arbitrary",), 56),
        name="moe_grouped",
    )(block_e, block_valid, block_first, next_e, slot_of, xs, wg, wu, wd)


def _final_kernel(pos_ref, posn_ref, h2_ref, sh_ref, gate_ref, g_ref, b_ref, ys_ref, o_ref, ybuf, sem, *, tm):
    i = pl.program_id(0)
    slot = i % 2
    tile = TOP_K * tm * ROW_TILE

    def gather(p_ref, dst_slot):
        def issue(t, carry):
            for kk in range(TOP_K):
                dst = ybuf.at[pl.ds(pl.multiple_of(dst_slot * tile + (kk * tm + t) * ROW_TILE, ROW_TILE), ROW_TILE), :]
                pltpu.make_async_copy(_tile_rows(ys_ref, p_ref[kk, t]), dst, sem.at[dst_slot]).start()
            return carry
        lax.fori_loop(0, tm, issue, 0)

    @pl.when(i == 0)
    def _():
        gather(pos_ref, slot)

    @pl.when(i + 1 < pl.num_programs(0))
    def _():
        gather(posn_ref, 1 - slot)

    base = pl.multiple_of(slot * tile, tile)
    for kk in range(TOP_K):
        pltpu.make_async_copy(ys_ref.at[pl.ds(0, tm * ROW_TILE), :],
                              ybuf.at[pl.ds(base + kk * tm * ROW_TILE, tm * ROW_TILE), :], sem.at[slot]).wait()
    gate = gate_ref[...]
    acc = None
    for kk in range(TOP_K):
        lo, hi = _unpack_rows(ybuf, base + kk * tm * ROW_TILE, tm)
        gk = gate[:, kk:kk + 1]
        terms = [gk * p for p in lo + hi]
        acc = terms if acc is None else [a + b for a, b in zip(acc, terms)]
    ffn = jnp.concatenate(acc, axis=1) + sh_ref[...]
    o_ref[...] = _layer_norm(DEEPNORM_ALPHA * h2_ref[...] + ffn, g_ref[...], b_ref[...])


def _final(pos, h2, shared, gate, ys, g, b, tm):
    t, d = h2.shape
    row = lambda i: (i, 0)
    return pl.pallas_call(
        functools.partial(_final_kernel, tm=tm),
        grid=(t // tm,),
        in_specs=[
            pl.BlockSpec((TOP_K, tm), lambda i: (0, i), memory_space=pltpu.SMEM),
            pl.BlockSpec((TOP_K, tm), lambda i: (0, jnp.minimum(i + 1, t // tm - 1)), memory_space=pltpu.SMEM),
            pl.BlockSpec((tm, d), row),
            pl.BlockSpec((tm, d), row),
            pl.BlockSpec((tm, TOP_K), row),
            _const_spec((1, d)), _const_spec((1, d)),
            pl.BlockSpec(memory_space=pl.ANY),
        ],
        out_specs=pl.BlockSpec((tm, d), row),
        out_shape=jax.ShapeDtypeStruct((t, d), F32),
        scratch_shapes=[pltpu.VMEM((2 * TOP_K * tm * ROW_TILE, LANES), jnp.uint32), pltpu.SemaphoreType.DMA((2,))],
        compiler_params=_params(("arbitrary",), 56),
        name="final",
    )(pos, pos, h2, shared, gate, g, b, ys)


def kernel(x, meta_tokens, ln_in_g, ln_in_b, w_in, b_forget, ssm_a_re, ssm_a_im, ssm_log_dt, ssm_b_re, ssm_b_im, ssm_c_re, ssm_c_im, ssm_d, w_glu, b_glu, w_out, ln_mix_g, ln_mix_b, w_router, router_bias, w_gate_exp, w_up_exp, w_down_exp, w_gate_sh, w_up_sh, w_down_sh, ln_ffn_g, ln_ffn_b):
    bsz, seq, d = x.shape
    t = bsz * seq
    ssm_w = w_glu.shape[1]
    heads = b_forget.shape[1]
    att_w = (w_in.shape[2] - ssm_w - heads) // 3
    groups = ssm_a_re.shape[1]
    assert meta_tokens.shape[0] == N_META == S5_CHUNK and att_w == heads * ATT_HEAD_DIM
    nc = seq // S5_CHUNK
    levels = int(math.log2(nc))
    assert 2 ** levels == nc and nc % 8 == 0
    row2 = lambda a: a.reshape(1, -1).astype(F32)

    w_u, w_q, w_k, w_v, w_f = jnp.split(w_in[0], [ssm_w, ssm_w + att_w, ssm_w + 2 * att_w, ssm_w + 3 * att_w], axis=1)
    w_f = jnp.pad(w_f, ((0, 0), (0, LANES - heads)))
    w_in_bf = jnp.concatenate([w_u, w_k, w_f], axis=1).astype(BF16)
    wqvt_bf = jnp.concatenate([w_q, w_v], axis=1).T.astype(BF16)
    bf_pad = jnp.pad(b_forget[0].astype(F32), (0, LANES - heads)).reshape(1, LANES)
    g_in, b_in = row2(ln_in_g), row2(ln_in_b)
    tm = min(256, t)
    tq = min(512, seq)
    h, u, qt, k, va, lf = _ln_inproj(x.reshape(t, d), g_in, b_in, w_in_bf, wqvt_bf, bf_pad, tm, ssm_w, att_w,
                                     seq, tq)
    _, u_m, _, k_m, va_m, lf_m = _ln_inproj(meta_tokens.astype(F32), g_in, b_in, w_in_bf, wqvt_bf, bf_pad, N_META,
                                            ssm_w, att_w, N_META, N_META)

    s5_tabs = _s5_tables(ssm_a_re[0], ssm_a_im[0], ssm_log_dt[0], ssm_b_re[0], ssm_b_im[0],
                         ssm_c_re[0], ssm_c_im[0], ssm_d[0], levels)
    y_ssm = _s5(u, u_m, *s5_tabs, bsz, nc, levels)

    lfm = lf_m[:, :heads] * LOG2E
    c_meta = jnp.cumsum(lfm, axis=0) - jnp.sum(lfm, axis=0, keepdims=True)
    ckm = jnp.full((heads, LANES), -NEG_BIG, F32).at[:, :N_META].set(c_meta.T)
    ckm = jnp.broadcast_to(ckm[:, :, None], (heads, LANES, LANES))
    c_main = jnp.cumsum(lf[:, :heads].reshape(bsz, seq, heads) * LOG2E, axis=1)
    ck = jnp.broadcast_to(c_main.transpose(0, 2, 1)[..., None], (bsz, heads, seq, LANES))
    km = jnp.pad(k_m, ((0, LANES - N_META), (0, 0)))
    vam = jnp.pad(va_m[0, :, 0], ((0, 0), (0, 0), (0, LANES - N_META)))
    y_att = _fox(qt, k, va, ck, km, vam, ckm, bsz, seq, heads, tq)

    h2, h2p, logits_t = _mix_out(
        y_ssm, y_att, h, w_glu[0].astype(BF16), row2(b_glu[0]), w_out[0].astype(BF16),
        row2(ln_mix_g[0]), row2(ln_mix_b[0]), w_router[0].T.astype(BF16), tm)

    tn = min(512, t)
    idx_t, gate_t, rank_t, counts = _route(logits_t, router_bias[0].astype(F32).reshape(N_EXPERTS, 1), tn)

    bm = 256
    counts = counts[:, 0].astype(jnp.int32)
    pcounts = (counts + bm - 1) // bm * bm
    pends = jnp.cumsum(pcounts)
    pstarts = pends - pcounts
    start_of = jnp.sum(jnp.where(idx_t[:, :, None] == jnp.arange(N_EXPERTS, dtype=jnp.int32), pstarts, 0), axis=-1)
    pos = (start_of + rank_t).astype(jnp.int32)
    n_blocks = t * TOP_K // bm + N_EXPERTS
    bstart = jnp.arange(n_blocks, dtype=jnp.int32) * bm
    block_e = jnp.minimum(jnp.sum(pends[None, :] <= bstart[:, None], axis=1), N_EXPERTS - 1).astype(jnp.int32)
    block_valid = (bstart < pends[-1]).astype(jnp.int32)
    block_first = block_valid * jnp.concatenate([jnp.ones((1,), jnp.int32),
                                                 (block_e[1:] != block_e[:-1]).astype(jnp.int32)])
    slot_of = ((jnp.cumsum(block_first) - 1) % 2).astype(jnp.int32)
    experts = jnp.arange(N_EXPERTS, dtype=jnp.int32)
    later_used = (experts[None, :] > experts[:, None]) & (counts[None, :] > 0)
    next_used = jnp.min(jnp.where(later_used, experts[None, :], N_EXPERTS), axis=1)
    next_used = jnp.where(next_used < N_EXPERTS, next_used, -1)
    next_e = jnp.sum(jnp.where(block_e[:, None] == experts[None, :], next_used[None, :], 0), axis=1).astype(jnp.int32)
    wgu_sh = jnp.concatenate([w_gate_sh[0], w_up_sh[0]], axis=1).astype(BF16)
    xs, shared = _dispatch((pstarts + counts).astype(jnp.int32), pends.astype(jnp.int32),
                           (pends[-1:] // bm).astype(jnp.int32), pos, h2p, wgu_sh, w_down_sh[0].astype(BF16),
                           n_blocks, bm, tm)
    ys = _moe_grouped(block_e, block_valid, block_first, next_e, slot_of, xs,
                      w_gate_exp[0], w_up_exp[0], w_down_exp[0], bm)

    out = _final(pos, h2, shared, gate_t.T, ys, row2(ln_ffn_g[0]), row2(ln_ffn_b[0]), tm)
    return out.reshape(bsz, seq, d)
```

```python
import functools
import math

import jax
import jax.numpy as jnp
from jax import lax
from jax.experimental import pallas as pl
from jax.experimental.pallas import tpu as pltpu

N_META = 16
SSM_GROUP_CH = 16
SSM_STATE = 64
ATT_HEAD_DIM = 128
N_EXPERTS = 64
TOP_K = 8
N_EXPERT_GROUPS = 8
TOPK_GROUPS = 4
ROUTED_SCALE = 2.5
LN_EPS = 1e-5
DEPTH = 1
DEEPNORM_ALPHA = (2 * DEPTH) ** 0.25

S5_CHUNK = 16
FOX_SLOTS = 4
LANES = 128
ROW_TILE = 8
ONES_ROWS = 16
NEG_BIG = -1e30
LOG2E = 1.4426950408889634

F32 = jnp.float32
BF16 = jnp.bfloat16
HIGHEST = lax.Precision.HIGHEST


def _dot(a, b):
    return jnp.dot(a, b, preferred_element_type=F32)


def _dot_nt(a, b):
    return lax.dot_general(a, b, (((1,), (1,)), ((), ())), preferred_element_type=F32)


def _layer_norm(x, g, b):
    mu = jnp.mean(x, axis=-1, keepdims=True)
    xc = x - mu
    var = jnp.mean(xc * xc, axis=-1, keepdims=True)
    return xc * lax.rsqrt(var + LN_EPS) * g + b


def _pack_rows(x, o_ref):
    m, w = x.shape
    half = w // 2
    assert half == ROW_TILE * LANES
    bits = lax.bitcast_convert_type(x.astype(BF16).astype(F32), jnp.uint32)
    packed = bits[:, half:] | (bits[:, :half] >> 16)
    for s in range(ROW_TILE):
        o_ref[pl.ds(s, m, stride=ROW_TILE), :] = packed[:, s * LANES:(s + 1) * LANES]


def _unpack_rows(x_ref, base, m):
    lo, hi = [], []
    for s in range(ROW_TILE):
        w = x_ref[pl.ds(base + s, m, stride=ROW_TILE), :]
        lo.append(lax.bitcast_convert_type(w << 16, F32))
        hi.append(lax.bitcast_convert_type(w & jnp.uint32(0xFFFF0000), F32))
    return lo, hi


def _params(sem, vmem_mb):
    return pltpu.CompilerParams(dimension_semantics=sem, vmem_limit_bytes=vmem_mb * 1024 * 1024)


def _const_spec(shape):
    nd = len(shape)
    return pl.BlockSpec(shape, lambda *_: (0,) * nd, pipeline_mode=pl.Buffered(1))


def _ln_inproj_kernel(x_ref, g_ref, b_ref, w_ref, wqvt_ref, bf_ref, h_ref, u_ref, qt_ref, k_ref, va_ref, lf_ref,
                      *, ssm_w, att_w):
    h = _layer_norm(x_ref[...], g_ref[...], b_ref[...])
    h_ref[...] = h
    hb = h.astype(BF16)
    o = 0
    u_ref[...] = _dot(hb, w_ref[:, o:o + ssm_w])
    o += ssm_w
    k_ref[...] = _dot(hb, w_ref[:, o:o + att_w]).astype(BF16)
    o += att_w
    qvt = _dot_nt(wqvt_ref[...], hb)
    qt_ref[...] = (qvt[:att_w] * (LOG2E * ATT_HEAD_DIM ** -0.5)).astype(BF16)
    vt = qvt[att_w:].astype(BF16)
    dh = ATT_HEAD_DIM
    for hh in range(att_w // dh):
        va_ref[0, hh, 0, 0:dh, :] = vt[hh * dh:(hh + 1) * dh]
        va_ref[0, hh, 0, dh:, :] = jnp.ones((ONES_ROWS, vt.shape[1]), BF16)
    f = _dot(hb, w_ref[:, o:o + LANES]) + bf_ref[...]
    lf_ref[...] = jnp.minimum(f, 0.0) - jnp.log(1.0 + jnp.exp(-jnp.abs(f)))


def _ln_inproj(x2d, g, b, w_bf, wqvt_bf, bf_pad, tm, ssm_w, att_w, seq, tq):
    t, d = x2d.shape
    wcols = w_bf.shape[1]
    heads, da = att_w // ATT_HEAD_DIM, ATT_HEAD_DIM + ONES_ROWS
    row = lambda i: (i, 0)
    col = lambda i: (0, i)
    per_seq, per_q = seq // tm, tq // tm
    va_idx = lambda i: (i // per_seq, 0, (i % per_seq) // per_q, 0, (i % per_seq) % per_q)
    return pl.pallas_call(
        functools.partial(_ln_inproj_kernel, ssm_w=ssm_w, att_w=att_w),
        grid=(t // tm,),
        in_specs=[
            pl.BlockSpec((tm, d), row),
            _const_spec((1, d)), _const_spec((1, d)),
            _const_spec((d, wcols)), _const_spec((2 * att_w, d)), _const_spec((1, LANES)),
        ],
        out_specs=[
            pl.BlockSpec((tm, d), row),
            pl.BlockSpec((tm, ssm_w), row),
            pl.BlockSpec((att_w, tm), col),
            pl.BlockSpec((tm, att_w), row),
            pl.BlockSpec((1, heads, 1, da, tm), va_idx),
            pl.BlockSpec((tm, LANES), row),
        ],
        out_shape=[
            jax.ShapeDtypeStruct((t, d), F32),
            jax.ShapeDtypeStruct((t, ssm_w), F32),
            jax.ShapeDtypeStruct((att_w, t), BF16),
            jax.ShapeDtypeStruct((t, att_w), BF16),
            jax.ShapeDtypeStruct((t // seq, heads, seq // tq, da, tq), BF16),
            jax.ShapeDtypeStruct((t, LANES), F32),
        ],
        compiler_params=_params(("parallel",), 56),
        name="ln_inproj",
    )(x2d, g, b, w_bf, wqvt_bf, bf_pad)


def _s5_tables(a_re, a_im, log_dt, b_re, b_im, c_re, c_im, d_skip, levels):
    c = S5_CHUNK
    g, p = a_re.shape
    hc = SSM_GROUP_CH
    lam = lax.complex(a_re.astype(F32), a_im.astype(F32))
    dt = jnp.exp(log_dt.astype(F32))[:, None]
    ldt = lam * dt
    abar = jnp.exp(ldt)
    bbar = ((abar - 1.0) / lam)[..., None] * lax.complex(b_re.astype(F32), b_im.astype(F32))
    cc = lax.complex(c_re.astype(F32), c_im.astype(F32))
    tau = jnp.arange(c + 1, dtype=F32)
    apow = jnp.exp(ldt[:, None, :] * tau[None, :, None])

    ca = cc[:, None, :, :] * apow[:, :c, None, :]
    taps = jnp.einsum('gthp,gpk->gthk', jnp.concatenate([ca.real, -ca.imag], axis=-1),
                      jnp.concatenate([bbar.real, bbar.imag], axis=1), precision=HIGHEST)
    taps = taps + (jnp.eye(hc, dtype=F32) * d_skip.astype(F32)[:, :, None])[:, None] * (jnp.arange(c) == 0)[None, :, None, None]
    lagmat = taps.transpose(0, 3, 1, 2).reshape(g, hc, c * hc)

    bp = apow[:, c - 1 - jnp.arange(c), :][:, :, :, None] * bbar[:, None, :, :]
    bp = bp.transpose(0, 1, 3, 2).reshape(g, c * hc, p)
    bp_t = jnp.concatenate([bp.real, bp.imag], axis=-1)

    cp = cc[:, None, :, :] * apow[:, 1:c + 1, None, :]
    cp = cp.reshape(g, c * hc, p).transpose(0, 2, 1)
    cp_t = jnp.concatenate([cp.real, -cp.imag], axis=1)

    steps = (c * (2.0 ** jnp.arange(levels, dtype=F32)))
    alev = jnp.exp(ldt[:, None, :] * steps[None, :, None])

    gl = LANES // hc
    nj = g // gl
    lag_rows = lagmat.reshape(nj, gl * hc, c * hc)
    bp_rows = bp_t.reshape(nj, gl, c, hc, 2 * p).transpose(0, 2, 1, 3, 4).reshape(nj, c * LANES, 2 * p)
    cp_rows = cp_t.reshape(nj, gl, 2, p, c * hc).transpose(0, 2, 1, 3, 4).reshape(nj, 2 * gl * p, c * hc)

    def selector(n_outer, inner):
        cols, wcols = n_outer * inner, n_outer * gl * inner
        src, dst = jnp.arange(cols)[:, None], jnp.arange(wcols)[None, :]
        return ((src // inner == dst // (gl * inner)) & (src % inner == dst % inner)).astype(BF16)

    al = alev.reshape(nj, gl, levels, p).transpose(0, 2, 1, 3).reshape(nj, levels, gl * p)
    pad = ((0, 0), (0, 16 - levels), (0, 0))
    a1 = jnp.pad(jnp.concatenate([al.real, al.real], axis=-1), pad)
    a2 = jnp.pad(jnp.concatenate([-al.imag, al.imag], axis=-1), pad)
    return (lag_rows, bp_rows.astype(BF16), cp_rows.astype(BF16), selector(c, hc), selector(2, p), a1, a2)


def _shift_rows(x, sh):
    n = x.shape[0]
    if sh % 8 == 0:
        return jnp.concatenate([jnp.zeros((sh, x.shape[1]), x.dtype), x[:n - sh]], axis=0)
    rolled = pltpu.roll(x, sh, axis=0)
    rows = lax.broadcasted_iota(jnp.int32, x.shape, 0)
    return jnp.where(rows < sh, 0.0, rolled)


def _cmul(a1, a2, x):
    return a1 * x + a2 * pltpu.roll(x, x.shape[1] // 2, axis=1)


def _spread_block_diag(rows, sel_ref, out_scr, row_span, col_span):
    gl = LANES // SSM_GROUP_CH
    n_rows, width = out_scr.shape
    step = 2 * LANES
    r_grp = (lax.broadcasted_iota(jnp.int32, (n_rows, step), 0) >> (row_span.bit_length() - 1)) & (gl - 1)
    for c0 in range(0, width, step):
        wide = _dot(rows, sel_ref[:, c0:c0 + step])
        w_grp = ((lax.broadcasted_iota(jnp.int32, (n_rows, step), 1) + c0) >> (col_span.bit_length() - 1)) & (gl - 1)
        out_scr[:, c0:c0 + step] = jnp.where(r_grp == w_grp, wide, 0.0).astype(BF16)


def _s5_kernel(u_ref, um_ref, lag_ref, bprow_ref, cprow_ref, selm_ref, selb_ref, a1_ref, a2_ref, y_ref,
               mrow_scr, mj_scr, bpj_scr, cpj_scr, *, nc, levels):
    c = S5_CHUNK
    hc = SSM_GROUP_CH

    @pl.when(pl.program_id(1) == 0)
    def _():
        lag = lag_ref[0]
        col = lax.broadcasted_iota(jnp.int32, lag.shape, 1)
        for s in range(c):
            blk = lag if s == 0 else jnp.where(col >= s * hc, pltpu.roll(lag, s * hc, axis=1), 0.0)
            mrow_scr[s * LANES:(s + 1) * LANES, :] = blk.astype(BF16)
        _spread_block_diag(mrow_scr[...], selm_ref, mj_scr, hc, hc)
        _spread_block_diag(bprow_ref[0], selb_ref, bpj_scr, hc, SSM_STATE)
        _spread_block_diag(cprow_ref[0], selm_ref, cpj_scr, SSM_STATE, hc)

    u = jnp.concatenate([u_ref[pl.ds(s, nc, stride=c), :].astype(BF16) for s in range(c)], axis=1)
    bpj = bpj_scr[...]
    w = _dot(u, bpj)
    um = jnp.concatenate([um_ref[s:s + 1, :] for s in range(c)], axis=1)
    x_meta = _dot(jnp.broadcast_to(um, (8, um.shape[1])).astype(BF16), bpj)
    first = lax.broadcasted_iota(jnp.int32, x_meta.shape, 0) == 0
    inject = jnp.where(first, _cmul(a1_ref[0, 0:1, :], a2_ref[0, 0:1, :], x_meta), 0.0)
    x = jnp.concatenate([w[:8] + inject, w[8:]], axis=0)
    for lvl in range(levels):
        x = x + _cmul(a1_ref[0, lvl:lvl + 1, :], a2_ref[0, lvl:lvl + 1, :], _shift_rows(x, 2 ** lvl))
    xp = _shift_rows(x, 1)
    x_in = jnp.concatenate([xp[:8] + jnp.where(first, x_meta, 0.0), xp[8:]], axis=0).astype(BF16)
    y = jax.nn.gelu(_dot(u, mj_scr[...]) + _dot(x_in, cpj_scr[...]))
    for s in range(c):
        y_ref[pl.ds(s, nc, stride=c), :] = y[:, s * LANES:(s + 1) * LANES]


def _s5(u, u_m, lag_rows, bp_rows, cp_rows, sel_m, sel_b, a1, a2, bsz, nc, levels):
    t, ssm_w = u.shape
    nj, cl, cb = bp_rows.shape
    sw, cm = cp_rows.shape[1], cp_rows.shape[2]
    nl = a1.shape[1]
    rows = nc * S5_CHUNK
    tab = lambda j, b: (j, 0, 0)
    return pl.pallas_call(
        functools.partial(_s5_kernel, nc=nc, levels=levels),
        grid=(nj, bsz),
        in_specs=[
            pl.BlockSpec((rows, LANES), lambda j, b: (b, j)),
            pl.BlockSpec((S5_CHUNK, LANES), lambda j, b: (0, j)),
            pl.BlockSpec((1, LANES, cm), tab),
            pl.BlockSpec((1, cl, cb), tab),
            pl.BlockSpec((1, sw, cm), tab),
            _const_spec((cm, cl)), _const_spec((cb, sw)),
            pl.BlockSpec((1, nl, sw), tab),
            pl.BlockSpec((1, nl, sw), tab),
        ],
        out_specs=pl.BlockSpec((rows, LANES), lambda j, b: (b, j)),
        out_shape=jax.ShapeDtypeStruct((t, ssm_w), F32),
        scratch_shapes=[pltpu.VMEM((cl, cm), BF16),
                        pltpu.VMEM((cl, cl), BF16), pltpu.VMEM((cl, sw), BF16), pltpu.VMEM((sw, cl), BF16)],
        compiler_params=_params(("parallel", "arbitrary"), 56),
        name="s5_scan",
    )(u, u_m, lag_rows, bp_rows, cp_rows, sel_m, sel_b, a1, a2)


def _fox_kernel(qt_ref, k_ref, va_ref, ck_ref, km_ref, vam_ref, ckm_ref, o_ref,
                m_scr, acc_scr, s_scr, bm_scr, p_scr, al_scr, *, tq, dh):
    qi = pl.program_id(2)
    qt = qt_ref[...]
    reps = tq // LANES

    def scores(kb, ckb):
        return _dot(kb, qt) - jnp.concatenate([ckb] * reps, axis=1)

    def put_scores(slot, s):
        s_scr[slot] = s
        bm_scr[slot] = jnp.broadcast_to(jnp.max(s, axis=0, keepdims=True), (8, tq))

    def softmax_update(slot):
        m_prev = m_scr[0:1, :]
        m_new = jnp.maximum(m_prev, bm_scr[slot, 0:1, :])
        m_scr[...] = jnp.broadcast_to(m_new, m_scr.shape)
        return jnp.exp2(m_prev - m_new), jnp.exp2((s_scr[slot] - m_new).astype(BF16))

    s = scores(km_ref[...], ckm_ref[0])
    m0 = jnp.max(s, axis=0, keepdims=True)
    m_scr[...] = jnp.broadcast_to(m0, m_scr.shape)
    acc_scr[...] = _dot(vam_ref[0], jnp.exp2(s - m0).astype(BF16))

    start = pl.multiple_of(qi * tq, tq)
    s = scores(k_ref[pl.ds(start, tq), :], ck_ref[0, 0, pl.ds(start, tq), :])
    key = lax.broadcasted_iota(jnp.int32, (tq, tq), 0)
    qry = lax.broadcasted_iota(jnp.int32, (tq, tq), 1)
    ns = FOX_SLOTS
    put_scores(ns - 1, jnp.where(key <= qry, s, NEG_BIG))
    p_scr[ns - 2] = jnp.zeros((tq, tq), BF16)
    al_scr[ns - 2] = jnp.ones((8, tq), F32)

    def tick(t, u):
        sb, sc = (u - 1) % ns, (u - 2) % ns
        jc = jnp.where(t == 1, qi, jnp.clip(t - 2, 0, qi))
        acc_scr[...] = al_scr[sc, 0:1, :] * acc_scr[...] + _dot(va_ref[0, 0, jc], p_scr[sc])
        alpha, p = softmax_update(sb)
        p_scr[sb] = p
        al_scr[sb] = jnp.broadcast_to(alpha, (8, tq))
        ja = jnp.minimum(t, jnp.maximum(qi - 1, 0))
        start = pl.multiple_of(ja * tq, tq)
        off = jnp.where(t < qi, 0.0, -NEG_BIG)
        put_scores(u, scores(k_ref[pl.ds(start, tq), :], ck_ref[0, 0, pl.ds(start, tq), :] + off))

    def body(i, carry):
        for u in range(ns):
            tick(ns * i + u, u)
        return carry

    n_ticks = (qi + 3) // 2 * 2
    n_full = n_ticks // ns
    lax.fori_loop(0, n_full, body, 0)

    @pl.when(n_ticks % ns != 0)
    def _():
        tick(ns * n_full, 0)
        tick(ns * n_full + 1, 1)
    acc = acc_scr[...]
    o_ref[...] = (acc[:dh] / acc[dh:dh + 1]).T.astype(BF16)


def _fox(qt, k, va, ck, km, vam, ckm, bsz, seq, heads, tq):
    aw, t = qt.shape
    nq = seq // tq
    dh = ATT_HEAD_DIM
    da = va.shape[3]
    return pl.pallas_call(
        functools.partial(_fox_kernel, tq=tq, dh=dh),
        grid=(bsz, heads, nq),
        in_specs=[
            pl.BlockSpec((dh, tq), lambda b, h, i: (h, b * nq + i)),
            pl.BlockSpec((seq, dh), lambda b, h, i: (b, h)),
            pl.BlockSpec((1, 1, nq, da, tq), lambda b, h, i: (b, h, 0, 0, 0)),
            pl.BlockSpec((1, 1, seq, LANES), lambda b, h, i: (b, h, 0, 0)),
            pl.BlockSpec((LANES, dh), lambda b, h, i: (0, h)),
            pl.BlockSpec((1, da, LANES), lambda b, h, i: (h, 0, 0)),
            pl.BlockSpec((1, LANES, LANES), lambda b, h, i: (h, 0, 0)),
        ],
        out_specs=pl.BlockSpec((tq, dh), lambda b, h, i: (b * nq + i, h)),
        out_shape=jax.ShapeDtypeStruct((t, aw), BF16),
        scratch_shapes=[
            pltpu.VMEM((8, tq), F32),
            pltpu.VMEM((da, tq), F32),
            pltpu.VMEM((FOX_SLOTS, tq, tq), F32),
            pltpu.VMEM((FOX_SLOTS, 8, tq), F32),
            pltpu.VMEM((FOX_SLOTS, tq, tq), BF16),
            pltpu.VMEM((FOX_SLOTS, 8, tq), F32),
        ],
        compiler_params=_params(("parallel", "parallel", "arbitrary"), 48),
        name="fox_attention",
    )(qt, k, va, ck, km, vam, ckm)


def _mix_out_kernel(ys_ref, ya_ref, h_ref, wglu_ref, bglu_ref, wout_ref, g_ref, b_ref, wr_ref,
                    h2_ref, h2p_ref, lt_ref, *, ssm_w):
    ys = ys_ref[...]
    z = _dot(ys.astype(BF16), wglu_ref[...]) + bglu_ref[...]
    yg = (ys * jax.nn.sigmoid(z)).astype(BF16)
    mix = _dot(yg, wout_ref[0:ssm_w, :]) + _dot(ya_ref[...], wout_ref[ssm_w:, :])
    h2 = _layer_norm(DEEPNORM_ALPHA * h_ref[...] + mix, g_ref[...], b_ref[...])
    h2_ref[...] = h2
    _pack_rows(h2, h2p_ref)
    lt_ref[...] = _dot_nt(wr_ref[...], h2.astype(BF16))


def _mix_out(ys, ya, h, wglu, bglu, wout, g, b, wr_t, tm):
    t, d = h.shape
    ssm_w = ys.shape[1]
    att_w = ya.shape[1]
    e = wr_t.shape[0]
    row = lambda i: (i, 0)
    return pl.pallas_call(
        functools.partial(_mix_out_kernel, ssm_w=ssm_w),
        grid=(t // tm,),
        in_specs=[
            pl.BlockSpec((tm, ssm_w), row),
            pl.BlockSpec((tm, att_w), row),
            pl.BlockSpec((tm, d), row),
            _const_spec((ssm_w, ssm_w)), _const_spec((1, ssm_w)),
            _const_spec((d, d)), _const_spec((1, d)), _const_spec((1, d)),
            _const_spec((e, d)),
        ],
        out_specs=[
            pl.BlockSpec((tm, d), row),
            pl.BlockSpec((tm * ROW_TILE, LANES), row),
            pl.BlockSpec((e, tm), lambda i: (0, i)),
        ],
        out_shape=[
            jax.ShapeDtypeStruct((t, d), F32),
            jax.ShapeDtypeStruct((t * ROW_TILE, LANES), jnp.uint32),
            jax.ShapeDtypeStruct((e, t), F32),
        ],
        compiler_params=_params(("parallel",), 48),
        name="mix_out",
    )(ys, ya, h, wglu, bglu, wout, g, b, wr_t)


def _route_kernel(lt_ref, bias_ref, idx_ref, gate_ref, rank_ref, cnt_ref, run_scr, *, tn):
    e = N_EXPERTS
    ng = N_EXPERT_GROUPS
    gs = e // ng

    @pl.when(pl.program_id(0) == 0)
    def _():
        run_scr[...] = jnp.zeros_like(run_scr)

    scores = jax.nn.sigmoid(lt_ref[...])
    sel = scores + bias_ref[...]
    sel3 = sel.reshape(ng, gs, tn)
    mem = lax.broadcasted_iota(jnp.int32, (ng, gs, tn), 1)
    m1 = jnp.max(sel3, axis=1, keepdims=True)
    i1 = jnp.min(jnp.where(sel3 == m1, mem, gs), axis=1, keepdims=True)
    m2 = jnp.max(jnp.where(mem == i1, -jnp.inf, sel3), axis=1, keepdims=True)
    gscore = (m1 + m2).reshape(ng, tn)

    gio = lax.broadcasted_iota(jnp.int32, (ng, tn), 0)
    gmask = jnp.zeros((ng, tn), F32)
    for _ in range(TOPK_GROUPS):
        mx = jnp.max(gscore, axis=0, keepdims=True)
        gi = jnp.min(jnp.where(gscore == mx, gio, ng), axis=0, keepdims=True)
        hit = gio == gi
        gmask = jnp.where(hit, 1.0, gmask)
        gscore = jnp.where(hit, -jnp.inf, gscore)

    cand = jnp.where(gmask.reshape(ng, 1, tn) > 0.5, sel3, -jnp.inf).reshape(e, tn)
    eio = lax.broadcasted_iota(jnp.int32, (e, tn), 0)
    hits = []
    chosen = jnp.zeros((e, tn), F32)
    for _ in range(TOP_K):
        mx = jnp.max(cand, axis=0, keepdims=True)
        ei = jnp.min(jnp.where(cand == mx, eio, e), axis=0, keepdims=True)
        hit = eio == ei
        hits.append((ei, hit))
        chosen = jnp.where(hit, 1.0, chosen)
        cand = jnp.where(hit, -jnp.inf, cand)

    onehot = chosen.astype(BF16)
    r = lax.broadcasted_iota(jnp.int32, (tn, tn), 0)
    c = lax.broadcasted_iota(jnp.int32, (tn, tn), 1)
    tri = (r < c).astype(BF16)
    rank = _dot(onehot, tri) + run_scr[:, 0:1]
    run_scr[...] = run_scr[...] + jnp.sum(chosen, axis=1, keepdims=True)
    cnt_ref[...] = run_scr[...]

    gates = [jnp.sum(jnp.where(hit, scores, 0.0), axis=0, keepdims=True) for _, hit in hits]
    total = gates[0]
    for gk in gates[1:]:
        total = total + gk
    for kk, (ei, hit) in enumerate(hits):
        idx_ref[kk:kk + 1, :] = ei
        gate_ref[kk:kk + 1, :] = gates[kk] / total * ROUTED_SCALE
        rank_ref[kk:kk + 1, :] = jnp.sum(jnp.where(hit, rank, 0.0), axis=0, keepdims=True).astype(jnp.int32)


def _route(logits_t, bias, tn):
    e, t = logits_t.shape
    col = lambda i: (0, i)
    return pl.pallas_call(
        functools.partial(_route_kernel, tn=tn),
        grid=(t // tn,),
        in_specs=[pl.BlockSpec((e, tn), col), _const_spec((e, 1))],
        out_specs=[
            pl.BlockSpec((TOP_K, tn), col),
            pl.BlockSpec((TOP_K, tn), col),
            pl.BlockSpec((TOP_K, tn), col),
            pl.BlockSpec((e, LANES), lambda i: (0, 0)),
        ],
        out_shape=[
            jax.ShapeDtypeStruct((TOP_K, t), jnp.int32),
            jax.ShapeDtypeStruct((TOP_K, t), F32),
            jax.ShapeDtypeStruct((TOP_K, t), jnp.int32),
            jax.ShapeDtypeStruct((e, LANES), F32),
        ],
        scratch_shapes=[pltpu.VMEM((e, LANES), F32)],
        compiler_params=_params(("arbitrary",), 32),
        name="route",
    )(logits_t, bias)


def _tile_rows(ref, row):
    return ref.at[pl.ds(pl.multiple_of(row * ROW_TILE, ROW_TILE), ROW_TILE), :]


def _dispatch_kernel(ps_ref, pe_ref, tail_ref, pos_ref, x_ref, wgu_ref, wd_ref, xs_ref, sh_ref, zero_scr, sem,
                     *, tm, bm, n_blocks, ff):
    @pl.when(pl.program_id(0) == 0)
    def _():
        zero_scr[...] = jnp.zeros_like(zero_scr)

        def fill(start):
            def per_expert(e, carry):
                first = ps_ref[e]
                n_pad = pe_ref[e] - first
                size = bm // 2
                while size >= 1:
                    done = n_pad & ~(2 * size - 1)

                    @pl.when((n_pad & size) != 0)
                    def _(size=size, done=done):
                        rows = pl.ds(pl.multiple_of((first + done) * ROW_TILE, ROW_TILE), size * ROW_TILE)
                        cp = pltpu.make_async_copy(zero_scr.at[0:size * ROW_TILE, :], xs_ref.at[rows, :], sem)
                        cp.start() if start else cp.wait()

                    size //= 2
                return carry
            lax.fori_loop(0, N_EXPERTS, per_expert, 0)

            def per_block(b, c):
                rows = pl.ds(pl.multiple_of(b * bm * ROW_TILE, bm * ROW_TILE), bm * ROW_TILE)
                cp = pltpu.make_async_copy(zero_scr, xs_ref.at[rows, :], sem)
                cp.start() if start else cp.wait()
                return c
            lax.fori_loop(tail_ref[0], n_blocks, per_block, 0)

        fill(True)
        fill(False)

    def issue(t, carry):
        src = _tile_rows(x_ref, t)
        for kk in range(TOP_K):
            pltpu.make_async_copy(src, _tile_rows(xs_ref, pos_ref[kk, t]), sem).start(priority=kk % 2)
        return carry

    lax.fori_loop(0, tm, issue, 0)
    lo, hi = _unpack_rows(x_ref, 0, tm)
    x = jnp.concatenate([p.astype(BF16) for p in lo + hi], axis=1)
    gu = _dot(x, wgu_ref[...])
    hid = jax.nn.silu(gu[:, :ff]) * gu[:, ff:]
    sh_ref[...] = _dot(hid.astype(BF16), wd_ref[...])
    for _ in range(TOP_K):
        pltpu.make_async_copy(x_ref, xs_ref.at[pl.ds(0, tm * ROW_TILE), :], sem).wait()


def _dispatch(pad_start, pad_end, tail_block, pos, h2p, wgu, wd, n_blocks, bm, tm):
    t = pos.shape[1]
    ff, d = wd.shape
    grid_spec = pltpu.PrefetchScalarGridSpec(
        num_scalar_prefetch=3,
        grid=(t // tm,),
        in_specs=[
            pl.BlockSpec((TOP_K, tm), lambda i, ps, pe, tl: (0, i), memory_space=pltpu.SMEM),
            pl.BlockSpec((tm * ROW_TILE, LANES), lambda i, ps, pe, tl: (i, 0)),
            _const_spec((d, 2 * ff)), _const_spec((ff, d)),
        ],
        out_specs=[pl.BlockSpec(memory_space=pl.ANY), pl.BlockSpec((tm, d), lambda i, ps, pe, tl: (i, 0))],
        scratch_shapes=[pltpu.VMEM((bm * ROW_TILE, LANES), jnp.uint32), pltpu.SemaphoreType.DMA(())],
    )
    return pl.pallas_call(
        functools.partial(_dispatch_kernel, tm=tm, bm=bm, n_blocks=n_blocks, ff=ff),
        grid_spec=grid_spec,
        out_shape=[jax.ShapeDtypeStruct((n_blocks * bm * ROW_TILE, LANES), jnp.uint32),
                   jax.ShapeDtypeStruct((t, d), F32)],
        compiler_params=_params(("arbitrary",), 40),
        name="dispatch",
    )(pad_start, pad_end, tail_block, pos, h2p, wgu, wd)


def _moe_kernel(be_ref, bv_ref, bf_ref, nx_ref, sl_ref, x_ref, wg_hbm, wu_hbm, wd_hbm, y_ref,
                wg_f, wu_f, wd_f, wg_s, wu_s, wd_s, sem, *, bm):
    b = pl.program_id(0)

    def weights(e, slot, start):
        for i, (src, dst) in enumerate(((wg_hbm, wg_f), (wu_hbm, wu_f), (wd_hbm, wd_f))):
            cp = pltpu.make_async_copy(src.at[e], dst.at[slot], sem.at[slot, i])
            cp.start() if start else cp.wait()

    @pl.when(bf_ref[b] == 1)
    def _():
        slot = sl_ref[b]

        @pl.when(b == 0)
        def _():
            weights(be_ref[0], slot, True)

        weights(be_ref[b], slot, False)

        @pl.when(nx_ref[b] >= 0)
        def _():
            weights(nx_ref[b], 1 - slot, True)

        wg_s[...] = wg_f[slot].astype(BF16)
        wu_s[...] = wu_f[slot].astype(BF16)
        wd_s[...] = wd_f[slot].astype(BF16)

    @pl.when(bv_ref[b] == 1)
    def _():
        lo, hi = _unpack_rows(x_ref, 0, bm)
        x = jnp.concatenate([p.astype(BF16) for p in lo + hi], axis=1)
        hid = jax.nn.silu(_dot(x, wg_s[...])) * _dot(x, wu_s[...])
        _pack_rows(_dot(hid.astype(BF16), wd_s[...]), y_ref)

    @pl.when(bv_ref[b] == 0)
    def _():
        y_ref[...] = jnp.zeros_like(y_ref)


def _moe_grouped(block_e, block_valid, block_first, next_e, slot_of, xs, wg, wu, wd, bm):
    rows = xs.shape[0] // ROW_TILE
    d, ff = wg.shape[1], wg.shape[2]
    n_blocks = rows // bm
    blk = lambda b, *_: (b, 0)
    grid_spec = pltpu.PrefetchScalarGridSpec(
        num_scalar_prefetch=5,
        grid=(n_blocks,),
        in_specs=[
            pl.BlockSpec((bm * ROW_TILE, LANES), blk),
            pl.BlockSpec(memory_space=pl.ANY), pl.BlockSpec(memory_space=pl.ANY), pl.BlockSpec(memory_space=pl.ANY),
        ],
        out_specs=pl.BlockSpec((bm * ROW_TILE, LANES), blk),
        scratch_shapes=[
            pltpu.VMEM((2, d, ff), F32), pltpu.VMEM((2, d, ff), F32), pltpu.VMEM((2, ff, d), F32),
            pltpu.VMEM((d, ff), BF16), pltpu.VMEM((d, ff), BF16), pltpu.VMEM((ff, d), BF16),
            pltpu.SemaphoreType.DMA((2, 3)),
        ],
    )
    return pl.pallas_call(
        functools.partial(_moe_kernel, bm=bm),
        grid_spec=grid_spec,
        out_shape=jax.ShapeDtypeStruct(xs.shape, jnp.uint32),
        compiler_params=_params(("arbitrary",), 56),
        name="moe_grouped",
    )(block_e, block_valid, block_first, next_e, slot_of, xs, wg, wu, wd)


def _final_kernel(pos_ref, posn_ref, h2_ref, sh_ref, gate_ref, g_ref, b_ref, ys_ref, o_ref, ybuf, sem, *, tm):
    i = pl.program_id(0)
    slot = i % 2
    tile = TOP_K * tm * ROW_TILE

    def gather(p_ref, dst_slot):
        def issue(t, carry):
            for kk in range(TOP_K):
                dst = ybuf.at[pl.ds(pl.multiple_of(dst_slot * tile + (kk * tm + t) * ROW_TILE, ROW_TILE), ROW_TILE), :]
                pltpu.make_async_copy(_tile_rows(ys_ref, p_ref[kk, t]), dst, sem.at[dst_slot]).start(priority=kk % 2)
            return carry
        lax.fori_loop(0, tm, issue, 0)

    @pl.when(i == 0)
    def _():
        gather(pos_ref, slot)

    @pl.when(i + 1 < pl.num_programs(0))
    def _():
        gather(posn_ref, 1 - slot)

    base = pl.multiple_of(slot * tile, tile)
    for kk in range(TOP_K):
        pltpu.make_async_copy(ys_ref.at[pl.ds(0, tm * ROW_TILE), :],
                              ybuf.at[pl.ds(base + kk * tm * ROW_TILE, tm * ROW_TILE), :], sem.at[slot]).wait()
    gate = gate_ref[...]
    acc = None
    for kk in range(TOP_K):
        lo, hi = _unpack_rows(ybuf, base + kk * tm * ROW_TILE, tm)
        gk = gate[:, kk:kk + 1]
        terms = [gk * p for p in lo + hi]
        acc = terms if acc is None else [a + b for a, b in zip(acc, terms)]
    ffn = jnp.concatenate(acc, axis=1) + sh_ref[...]
    o_ref[...] = _layer_norm(DEEPNORM_ALPHA * h2_ref[...] + ffn, g_ref[...], b_ref[...])


def _final(pos, h2, shared, gate, ys, g, b, tm):
    t, d = h2.shape
    row = lambda i: (i, 0)
    return pl.pallas_call(
        functools.partial(_final_kernel, tm=tm),
        grid=(t // tm,),
        in_specs=[
            pl.BlockSpec((TOP_K, tm), lambda i: (0, i), memory_space=pltpu.SMEM),
            pl.BlockSpec((TOP_K, tm), lambda i: (0, jnp.minimum(i + 1, t // tm - 1)), memory_space=pltpu.SMEM),
            pl.BlockSpec((tm, d), row),
            pl.BlockSpec((tm, d), row),
            pl.BlockSpec((tm, TOP_K), row),
            _const_spec((1, d)), _const_spec((1, d)),
            pl.BlockSpec(memory_space=pl.ANY),
        ],
        out_specs=pl.BlockSpec((tm, d), row),
        out_shape=jax.ShapeDtypeStruct((t, d), F32),
        scratch_shapes=[pltpu.VMEM((2 * TOP_K * tm * ROW_TILE, LANES), jnp.uint32), pltpu.SemaphoreType.DMA((2,))],
        compiler_params=_params(("arbitrary",), 56),
        name="final",
    )(pos, pos, h2, shared, gate, g, b, ys)


def kernel(x, meta_tokens, ln_in_g, ln_in_b, w_in, b_forget, ssm_a_re, ssm_a_im, ssm_log_dt, ssm_b_re, ssm_b_im, ssm_c_re, ssm_c_im, ssm_d, w_glu, b_glu, w_out, ln_mix_g, ln_mix_b, w_router, router_bias, w_gate_exp, w_up_exp, w_down_exp, w_gate_sh, w_up_sh, w_down_sh, ln_ffn_g, ln_ffn_b):
    bsz, seq, d = x.shape
    t = bsz * seq
    ssm_w = w_glu.shape[1]
    heads = b_forget.shape[1]
    att_w = (w_in.shape[2] - ssm_w - heads) // 3
    groups = ssm_a_re.shape[1]
    assert meta_tokens.shape[0] == N_META == S5_CHUNK and att_w == heads * ATT_HEAD_DIM
    nc = seq // S5_CHUNK
    levels = int(math.log2(nc))
    assert 2 ** levels == nc and nc % 8 == 0
    row2 = lambda a: a.reshape(1, -1).astype(F32)

    w_u, w_q, w_k, w_v, w_f = jnp.split(w_in[0], [ssm_w, ssm_w + att_w, ssm_w + 2 * att_w, ssm_w + 3 * att_w], axis=1)
    w_f = jnp.pad(w_f, ((0, 0), (0, LANES - heads)))
    w_in_bf = jnp.concatenate([w_u, w_k, w_f], axis=1).astype(BF16)
    wqvt_bf = jnp.concatenate([w_q, w_v], axis=1).T.astype(BF16)
    bf_pad = jnp.pad(b_forget[0].astype(F32), (0, LANES - heads)).reshape(1, LANES)
    g_in, b_in = row2(ln_in_g), row2(ln_in_b)
    tm = min(256, t)
    tq = min(512, seq)
    h, u, qt, k, va, lf = _ln_inproj(x.reshape(t, d), g_in, b_in, w_in_bf, wqvt_bf, bf_pad, tm, ssm_w, att_w,
                                     seq, tq)
    _, u_m, _, k_m, va_m, lf_m = _ln_inproj(meta_tokens.astype(F32), g_in, b_in, w_in_bf, wqvt_bf, bf_pad, N_META,
                                            ssm_w, att_w, N_META, N_META)

    s5_tabs = _s5_tables(ssm_a_re[0], ssm_a_im[0], ssm_log_dt[0], ssm_b_re[0], ssm_b_im[0],
                         ssm_c_re[0], ssm_c_im[0], ssm_d[0], levels)
    y_ssm = _s5(u, u_m, *s5_tabs, bsz, nc, levels)

    lfm = lf_m[:, :heads] * LOG2E
    c_meta = jnp.cumsum(lfm, axis=0) - jnp.sum(lfm, axis=0, keepdims=True)
    ckm = jnp.full((heads, LANES), -NEG_BIG, F32).at[:, :N_META].set(c_meta.T)
    ckm = jnp.broadcast_to(ckm[:, :, None], (heads, LANES, LANES))
    c_main = jnp.cumsum(lf[:, :heads].reshape(bsz, seq, heads) * LOG2E, axis=1)
    ck = jnp.broadcast_to(c_main.transpose(0, 2, 1)[..., None], (bsz, heads, seq, LANES))
    km = jnp.pad(k_m, ((0, LANES - N_META), (0, 0)))
    vam = jnp.pad(va_m[0, :, 0], ((0, 0), (0, 0), (0, LANES - N_META)))
    y_att = _fox(qt, k, va, ck, km, vam, ckm, bsz, seq, heads, tq)

    h2, h2p, logits_t = _mix_out(
        y_ssm, y_att, h, w_glu[0].astype(BF16), row2(b_glu[0]), w_out[0].astype(BF16),
        row2(ln_mix_g[0]), row2(ln_mix_b[0]), w_router[0].T.astype(BF16), tm)

    tn = min(512, t)
    idx_t, gate_t, rank_t, counts = _route(logits_t, router_bias[0].astype(F32).reshape(N_EXPERTS, 1), tn)

    bm = 256
    counts = counts[:, 0].astype(jnp.int32)
    pcounts = (counts + bm - 1) // bm * bm
    pends = jnp.cumsum(pcounts)
    pstarts = pends - pcounts
    start_of = jnp.sum(jnp.where(idx_t[:, :, None] == jnp.arange(N_EXPERTS, dtype=jnp.int32), pstarts, 0), axis=-1)
    pos = (start_of + rank_t).astype(jnp.int32)
    n_blocks = t * TOP_K // bm + N_EXPERTS
    bstart = jnp.arange(n_blocks, dtype=jnp.int32) * bm
    block_e = jnp.minimum(jnp.sum(pends[None, :] <= bstart[:, None], axis=1), N_EXPERTS - 1).astype(jnp.int32)
    block_valid = (bstart < pends[-1]).astype(jnp.int32)
    block_first = block_valid * jnp.concatenate([jnp.ones((1,), jnp.int32),
                                                 (block_e[1:] != block_e[:-1]).astype(jnp.int32)])
    slot_of = ((jnp.cumsum(block_first) - 1) % 2).astype(jnp.int32)
    experts = jnp.arange(N_EXPERTS, dtype=jnp.int32)
    later_used = (experts[None, :] > experts[:, None]) & (counts[None, :] > 0)
    next_used = jnp.min(jnp.where(later_used, experts[None, :], N_EXPERTS), axis=1)
    next_used = jnp.where(next_used < N_EXPERTS, next_used, -1)
    next_e = jnp.sum(jnp.where(block_e[:, None] == experts[None, :], next_used[None, :], 0), axis=1).astype(jnp.int32)
    wgu_sh = jnp.concatenate([w_gate_sh[0], w_up_sh[0]], axis=1).astype(BF16)
    xs, shared = _dispatch((pstarts + counts).astype(jnp.int32), pends.astype(jnp.int32),
                           (pends[-1:] // bm).astype(jnp.int32), pos, h2p, wgu_sh, w_down_sh[0].astype(BF16),
                           n_blocks, bm, tm)
    ys = _moe_grouped(block_e, block_valid, block_first, next_e, slot_of, xs,
                      w_gate_exp[0], w_up_exp[0], w_down_exp[0], bm)

    out = _final(pos, h2, shared, gate_t.T, ys, row2(ln_ffn_g[0]), row2(ln_ffn_b[0]), tm)
    return out.reshape(bsz, seq, d)
```
